```python
import math
import jax, jax.numpy as jnp
from jax import lax
import numpy as np

D_MODEL = 1024
BATCH = 8
SEQ = 2048
DEPTH = 4
DEC_BATCH = 128
DEC_SEQ = 1
PAST_LEN = 16384
PAGE_SIZE = 128

N_META = 16
D_RG = D_MODEL // 2
RG_BLOCKS = 8
RG_BLOCK = D_RG // RG_BLOCKS
RG_C = 8.0
D_ML = D_MODEL // 2
ML_HEADS = 4
ML_HD = D_ML // ML_HEADS
CONV_W = 4
D_FF = 4 * D_MODEL
CHUNK = 64
EPS = 1e-6
D_IN = 2 * D_RG + 2 * D_ML + 2 * ML_HEADS

kernel_name = "hymba_rglru_mlstm_decoder_step"


def _rmsnorm(x, g):
    x32 = x.astype(jnp.float32)
    y = x32 * lax.rsqrt(jnp.mean(x32 * x32, axis=-1, keepdims=True) + EPS)
    return (y * g.astype(jnp.float32)).astype(x.dtype)


def _causal_conv(x, buf, w, b):
    T = x.shape[1]
    xp = jnp.concatenate([buf.astype(x.dtype), x], axis=1)
    y = b
    for j in range(CONV_W):
        y = y + w[j] * xp[:, j:j + T]
    return y, xp[:, T:]


def _rg_lru(x, h0, w_a, w_x, b_a, b_x, lam):
    B, T, _ = x.shape
    xb = x.reshape(B, T, RG_BLOCKS, RG_BLOCK)
    r = jax.nn.sigmoid(jnp.einsum('btgi,gij->btgj', xb, w_a).reshape(B, T, D_RG) + b_a)
    i = jax.nn.sigmoid(jnp.einsum('btgi,gij->btgj', xb, w_x).reshape(B, T, D_RG) + b_x)
    log_a = (RG_C * r.astype(jnp.float32)) * jax.nn.log_sigmoid(lam.astype(jnp.float32))
    a = jnp.exp(log_a)
    gx = jnp.sqrt(-jnp.expm1(2.0 * log_a)) * (i * x).astype(jnp.float32)

    def comb(lhs, rhs):
        a1, b1 = lhs
        a2, b2 = rhs
        return a1 * a2, a2 * b1 + b2

    A, Bc = lax.associative_scan(comb, (a, gx), axis=1)
    h = A * h0.astype(jnp.float32)[:, None] + Bc
    return h.astype(x.dtype), h[:, -1].astype(h0.dtype)


def _mlstm_chunk(carry, inp):
    C0, n0, m0 = carry
    q, k, v, ig, lf = inp
    L = q.shape[1]
    b = jnp.cumsum(lf, axis=1)
    g = ig - b
    m = b + jnp.maximum(m0[:, None], lax.cummax(g, axis=1))
    inter = jnp.exp(b + m0[:, None] - m)
    mask = jnp.tril(jnp.ones((L, L), dtype=bool))[None, :, :, None]
    log_d = (b - m)[:, :, None, :] + g[:, None, :, :]
    d = jnp.exp(jnp.where(mask, log_d, -jnp.inf))
    s = jnp.einsum('bthd,bshd->btsh', q, k) * d
    num = jnp.einsum('btsh,bshd->bthd', s, v) + inter[..., None] * jnp.einsum('bthk,bhkv->bthv', q, C0)
    den = jnp.sum(s, axis=2) + inter * jnp.einsum('bthk,bhk->bth', q, n0)
    h = num / jnp.maximum(jnp.abs(den), jnp.exp(-m))[..., None]
    mL = m[:, -1]
    sc = jnp.exp(b[:, -1] + m0 - mL)
    w = jnp.exp(b[:, -1:] - mL[:, None] + g)
    C1 = sc[..., None, None] * C0 + jnp.einsum('bsh,bshk,bshv->bhkv', w, k, v)
    n1 = sc[..., None] * n0 + jnp.einsum('bsh,bshk->bhk', w, k)
    return (C1, n1, mL), h


def _mlstm(q, k, v, ig, lf, carry, lead):
    B = q.shape[0]
    outs = []
    if lead > 0:
        carry, h0 = _mlstm_chunk(carry, (q[:, :lead], k[:, :lead], v[:, :lead], ig[:, :lead], lf[:, :lead]))
        outs.append(h0)
    T = q.shape[1] - lead
    L = CHUNK if T % CHUNK == 0 else T
    nc = T // L

    def to_chunks(a):
        a = a[:, lead:]
        return a.reshape((B, nc, L) + a.shape[2:]).swapaxes(0, 1)

    carry, hs = lax.scan(_mlstm_chunk, carry, tuple(to_chunks(a) for a in (q, k, v, ig, lf)))
    outs.append(hs.swapaxes(0, 1).reshape((B, T) + hs.shape[3:]))
    return jnp.concatenate(outs, axis=1), carry


def _layer(x, st, w, lead):
    rg_h, rg_conv, m_conv, mC, mn, mm = st
    (ln1_g, w_in, rg_conv_w, rg_conv_b, rg_w_a, rg_w_x, rg_b_a, rg_b_x, rg_lambda,
     m_conv_w, m_conv_b, m_w_q, m_w_k, m_w_v, m_b_i, m_b_f, m_norm_g,
     w_out, ln2_g, w_ff1, w_ff2) = w
    B, T, _ = x.shape
    u = _rmsnorm(x, ln1_g)
    proj = u @ w_in
    o1 = D_RG; o2 = o1 + D_RG; o3 = o2 + D_ML; o4 = o3 + D_ML; o5 = o4 + ML_HEADS
    rg_x, rg_g, m_x, m_o = proj[..., :o1], proj[..., o1:o2], proj[..., o2:o3], proj[..., o3:o4]
    m_i, m_f = proj[..., o4:o5], proj[..., o5:]
    xc, rg_conv_new = _causal_conv(rg_x, rg_conv, rg_conv_w, rg_conv_b)
    hr, rg_h_new = _rg_lru(xc, rg_h, rg_w_a, rg_w_x, rg_b_a, rg_b_x, rg_lambda)
    y_rg = hr * jax.nn.gelu(rg_g)
    mc, m_conv_new = _causal_conv(m_x, m_conv, m_conv_w, m_conv_b)
    mc = jax.nn.silu(mc).reshape(B, T, ML_HEADS, ML_HD)
    mv = m_x.reshape(B, T, ML_HEADS, ML_HD)
    f32 = jnp.float32
    q = jnp.einsum('bthi,hij->bthj', mc, m_w_q).astype(f32) * (ML_HD ** -0.5)
    k = jnp.einsum('bthi,hij->bthj', mc, m_w_k).astype(f32)
    v = jnp.einsum('bthi,hij->bthj', mv, m_w_v).astype(f32)
    ig = (m_i + m_b_i).astype(f32)
    lf = jax.nn.log_sigmoid((m_f + m_b_f).astype(f32))
    carry = (mC.astype(f32), mn.astype(f32), mm.astype(f32))
    hm, (C1, n1, m1) = _mlstm(q, k, v, ig, lf, carry, lead)
    hm = jax.nn.sigmoid(m_o.astype(f32)).reshape(B, T, ML_HEADS, ML_HD) * hm
    hm = hm * lax.rsqrt(jnp.mean(hm * hm, axis=-1, keepdims=True) + EPS)
    hm = (hm.reshape(B, T, D_ML) * m_norm_g.astype(f32)).astype(x.dtype)
    x = x + jnp.concatenate([y_rg, hm], axis=-1) @ w_out
    u2 = _rmsnorm(x, ln2_g)
    x = x + jnp.square(jax.nn.relu(u2 @ w_ff1)) @ w_ff2
    new = (rg_h_new, rg_conv_new.astype(rg_conv.dtype), m_conv_new.astype(m_conv.dtype),
           C1.astype(mC.dtype), n1.astype(mn.dtype), m1.astype(mm.dtype))
    return x, new


def _trunk(x, states, weights, ln_f_g, lead):
    new = [[] for _ in range(len(states))]
    for l in range(DEPTH):
        x, st_new = _layer(x, [s[l] for s in states], [a[l] for a in weights], lead)
        for lst, s in zip(new, st_new):
            lst.append(s)
    return _rmsnorm(x, ln_f_g), [jnp.stack(s) for s in new]


def setup_inputs(seed: int = 0) -> dict:
    key = jax.random.key(seed)
    ks = jax.random.split(key, 40)
    nrm = lambda i, shape, s: jax.random.normal(ks[i], shape, jnp.float32) * s
    u_a = jax.random.uniform(ks[20], (DEPTH, D_RG), jnp.float32, 0.9, 0.999)
    a0 = u_a ** (1.0 / RG_C)
    rg_lambda = jnp.log(a0) - jnp.log1p(-a0)
    m_b_f = jnp.linspace(3.0, 6.0, ML_HEADS, dtype=jnp.float32)[None] + nrm(26, (DEPTH, ML_HEADS), 0.1)
    return {
        "x_prompt": nrm(0, (BATCH, SEQ, D_MODEL), 1.0),
        "x_sample": nrm(1, (DEC_BATCH, DEC_SEQ, D_MODEL), 1.0),
        "state_rg_h": nrm(2, (DEPTH, DEC_BATCH, D_RG), 0.5),
        "state_rg_conv": nrm(3, (DEPTH, DEC_BATCH, CONV_W - 1, D_RG), 1.0),
        "state_m_conv": nrm(4, (DEPTH, DEC_BATCH, CONV_W - 1, D_ML), 1.0),
        "state_m_C": nrm(5, (DEPTH, DEC_BATCH, ML_HEADS, ML_HD, ML_HD), 0.1),
        "state_m_n": nrm(6, (DEPTH, DEC_BATCH, ML_HEADS, ML_HD), 0.1),
        "state_m_m": nrm(7, (DEPTH, DEC_BATCH, ML_HEADS), 1.0),
        "meta_tokens": nrm(8, (N_META, D_MODEL), 1.0),
        "ln1_g": 1.0 + nrm(9, (DEPTH, D_MODEL), 0.02),
        "w_in": nrm(10, (DEPTH, D_MODEL, D_IN), D_MODEL ** -0.5),
        "rg_conv_w": nrm(11, (DEPTH, CONV_W, D_RG), CONV_W ** -0.5),
        "rg_conv_b": nrm(12, (DEPTH, D_RG), 0.01),
        "rg_w_a": nrm(13, (DEPTH, RG_BLOCKS, RG_BLOCK, RG_BLOCK), RG_BLOCK ** -0.5),
        "rg_w_x": nrm(14, (DEPTH, RG_BLOCKS, RG_BLOCK, RG_BLOCK), RG_BLOCK ** -0.5),
        "rg_b_a": nrm(15, (DEPTH, D_RG), 0.1),
        "rg_b_x": nrm(16, (DEPTH, D_RG), 0.1),
        "rg_lambda": rg_lambda,
        "m_conv_w": nrm(17, (DEPTH, CONV_W, D_ML), CONV_W ** -0.5),
        "m_conv_b": nrm(18, (DEPTH, D_ML), 0.01),
        "m_w_q": nrm(19, (DEPTH, ML_HEADS, ML_HD, ML_HD), ML_HD ** -0.5),
        "m_w_k": nrm(21, (DEPTH, ML_HEADS, ML_HD, ML_HD), ML_HD ** -0.5),
        "m_w_v": nrm(22, (DEPTH, ML_HEADS, ML_HD, ML_HD), ML_HD ** -0.5),
        "m_b_i": nrm(23, (DEPTH, ML_HEADS), 0.1),
        "m_b_f": m_b_f,
        "m_norm_g": 1.0 + nrm(24, (DEPTH, D_ML), 0.02),
        "w_out": nrm(25, (DEPTH, D_RG + D_ML, D_MODEL), (D_RG + D_ML) ** -0.5),
        "ln2_g": 1.0 + nrm(27, (DEPTH, D_MODEL), 0.02),
        "w_ff1": nrm(28, (DEPTH, D_MODEL, D_FF), D_MODEL ** -0.5),
        "w_ff2": nrm(29, (DEPTH, D_FF, D_MODEL), D_FF ** -0.5),
        "ln_f_g": 1.0 + nrm(30, (D_MODEL,), 0.02),
    }


def reference(x_prompt, x_sample, state_rg_h, state_rg_conv, state_m_conv, state_m_C, state_m_n, state_m_m,
              meta_tokens, ln1_g, w_in, rg_conv_w, rg_conv_b, rg_w_a, rg_w_x, rg_b_a, rg_b_x, rg_lambda,
              m_conv_w, m_conv_b, m_w_q, m_w_k, m_w_v, m_b_i, m_b_f, m_norm_g, w_out, ln2_g,
              w_ff1, w_ff2, ln_f_g):
    weights = (ln1_g, w_in, rg_conv_w, rg_conv_b, rg_w_a, rg_w_x, rg_b_a, rg_b_x, rg_lambda,
               m_conv_w, m_conv_b, m_w_q, m_w_k, m_w_v, m_b_i, m_b_f, m_norm_g,
               w_out, ln2_g, w_ff1, w_ff2)
    B = x_prompt.shape[0]
    dt = x_prompt.dtype
    meta = jnp.broadcast_to(meta_tokens.astype(dt)[None], (B, N_META, D_MODEL))
    xp = jnp.concatenate([meta, x_prompt], axis=1)
    zero_states = [
        jnp.zeros((DEPTH, B, D_RG), state_rg_h.dtype),
        jnp.zeros((DEPTH, B, CONV_W - 1, D_RG), state_rg_conv.dtype),
        jnp.zeros((DEPTH, B, CONV_W - 1, D_ML), state_m_conv.dtype),
        jnp.zeros((DEPTH, B, ML_HEADS, ML_HD, ML_HD), state_m_C.dtype),
        jnp.zeros((DEPTH, B, ML_HEADS, ML_HD), state_m_n.dtype),
        jnp.zeros((DEPTH, B, ML_HEADS), state_m_m.dtype),
    ]
    yp, ps = _trunk(xp, zero_states, weights, ln_f_g, N_META)
    y_prompt = yp[:, N_META:]
    s_in = [state_rg_h, state_rg_conv, state_m_conv, state_m_C, state_m_n, state_m_m]
    y_sample, ss = _trunk(x_sample, s_in, weights, ln_f_g, 0)
    return (y_prompt, y_sample, ps[0], ps[1], ps[2], ps[3], ps[4], ps[5], ss[0], ss[1], ss[2], ss[3], ss[4], ss[5])
```

```python
import functools

import jax
import jax.numpy as jnp
from jax import lax
from jax.experimental import pallas as pl
from jax.experimental.pallas import tpu as pltpu

F32 = jnp.float32
BF16 = jnp.bfloat16

D_MODEL = 1024
N_META = 16
D_RG = 512
RG_BLOCK = 64
RG_C = 8.0
D_ML = 512
ML_HEADS = 4
ML_HD = 128
CONV_W = 4
D_FF = 4096
EPS = 1e-6
DEPTH = 4

LANES = 128
SUBLANES = 8
CHUNK = 128
PITCH = CHUNK + SUBLANES
N_GROUPS = D_RG // LANES
D_MAIN = 2 * D_RG + 2 * D_ML
GATE_ROWS = SUBLANES
S_COLS = 2 * ML_HD
FF_TILE = 1024
VMEM_LIMIT = 56 * 1024 * 1024

_NT = (((1,), (1,)), ((), ()))


def _const_spec(shape):
    zeros = (0,) * len(shape)
    return pl.BlockSpec(shape, lambda *_: zeros, pipeline_mode=pl.Buffered(1))


def _rms(x, g):
    return x * lax.rsqrt(jnp.mean(x * x, axis=-1, keepdims=True) + EPS) * g


def _inproj_body(x_ref, g_ref, w_ref, wgi_ref, wgf_ref, p_ref, gi_ref, gf_ref):
    nb, lt, d = x_ref.shape
    x = x_ref[...].reshape(nb * lt, d)
    u = _rms(x, g_ref[...]).astype(BF16)
    for n in range(D_MAIN // 512):
        cols = slice(n * 512, (n + 1) * 512)
        p_ref[:, cols] = jnp.dot(u, w_ref[:, cols], preferred_element_type=F32)
    for b in range(nb):
        ub = u[b * lt:(b + 1) * lt]
        rows = slice(b * GATE_ROWS, (b + 1) * GATE_ROWS)
        gi_ref[rows, :] = lax.dot_general(wgi_ref[...], ub, _NT, preferred_element_type=F32)
        gf_ref[rows, :] = lax.dot_general(wgf_ref[...], ub, _NT, preferred_element_type=F32)


def _inproj(x, g, w, wgi, wgf, *, lt, name):
    nb, t, d = x.shape
    nc = t // lt
    rows = nb * lt
    return pl.pallas_call(
        _inproj_body,
        grid=(nc,),
        in_specs=[
            pl.BlockSpec((nb, lt, d), lambda c: (0, c, 0)),
            _const_spec((1, d)),
            _const_spec((d, D_MAIN)),
            _const_spec((GATE_ROWS, d)),
            _const_spec((GATE_ROWS, d)),
        ],
        out_specs=[
            pl.BlockSpec((rows, D_MAIN), lambda c: (c, 0)),
            pl.BlockSpec((nb * GATE_ROWS, lt), lambda c: (c, 0)),
            pl.BlockSpec((nb * GATE_ROWS, lt), lambda c: (c, 0)),
        ],
        out_shape=[
            jax.ShapeDtypeStruct((nc * rows, D_MAIN), F32),
            jax.ShapeDtypeStruct((nc * nb * GATE_ROWS, lt), F32),
            jax.ShapeDtypeStruct((nc * nb * GATE_ROWS, lt), F32),
        ],
        compiler_params=pltpu.CompilerParams(dimension_semantics=("arbitrary",), vmem_limit_bytes=VMEM_LIMIT),
        name=name,
    )(x, g, w, wgi, wgf)


def _outmlp_body(x_ref, y_ref, wo_ref, g2_ref, w1_ref, w2_ref, gf_ref, o_ref, *, final):
    nb, lt, d = x_ref.shape
    rows = nb * lt
    x = x_ref[...].reshape(rows, d)
    y = y_ref[...].reshape(rows, d)
    x1 = x + jnp.dot(y, wo_ref[...], preferred_element_type=F32)
    u2 = _rms(x1, g2_ref[...]).astype(BF16)
    acc = x1
    for c in range(D_FF // FF_TILE):
        cols = slice(c * FF_TILE, (c + 1) * FF_TILE)
        h = jnp.dot(u2, w1_ref[:, cols], preferred_element_type=F32)
        h = jnp.square(jnp.maximum(h, 0.0)).astype(BF16)
        acc = acc + jnp.dot(h, w2_ref[cols, :], preferred_element_type=F32)
    if final:
        acc = _rms(acc, gf_ref[...])
    o_ref[...] = acc.reshape(nb, lt, d)


def _outmlp(x, y, wo, g2, w1, w2, gf, *, lt, final, skip, name):
    nb, t, d = x.shape
    nsteps = t // lt - skip
    return pl.pallas_call(
        functools.partial(_outmlp_body, final=final),
        grid=(nsteps,),
        in_specs=[
            pl.BlockSpec((nb, lt, d), lambda i: (0, i + skip, 0)),
            pl.BlockSpec((nb, lt, d), lambda i: (0, i + skip, 0)),
            _const_spec((d, d)),
            _const_spec((1, d)),
            _const_spec((d, D_FF)),
            _const_spec((D_FF, d)),
            _const_spec((1, d)),
        ],
        out_specs=pl.BlockSpec((nb, lt, d), lambda i: (0, i, 0)),
        out_shape=jax.ShapeDtypeStruct((nb, nsteps * lt, d), F32),
        compiler_params=pltpu.CompilerParams(dimension_semantics=("arbitrary",), vmem_limit_bytes=VMEM_LIMIT),
        name=name,
    )(x, y, wo, g2, w1, w2, gf)


def _rg_gates(xc, wa_ref, wx_ref, rba_ref, rbx_ref, lam_ref, g):
    sl = slice(g * LANES, (g + 1) * LANES)
    xg = xc[:, sl]
    xb = xg.astype(BF16)
    r = jax.nn.sigmoid(jnp.dot(xb, wa_ref[g], preferred_element_type=F32) + rba_ref[:, sl])
    i = jax.nn.sigmoid(jnp.dot(xb, wx_ref[g], preferred_element_type=F32) + rbx_ref[:, sl])
    log_a = (RG_C * r) * jax.nn.log_sigmoid(lam_ref[:, sl])
    a = jnp.exp(log_a)
    gx = jnp.sqrt(1.0 - a * a) * (i * xg)
    return a, gx


def _lane_scan(x, op, fill):
    lane = lax.broadcasted_iota(jnp.int32, x.shape, 1)
    sh = 1
    while sh < x.shape[1]:
        x = op(x, jnp.where(lane >= sh, pltpu.roll(x, sh, 1), fill))
        sh *= 2
    return x


def _pmix_body(p_ref, gi_ref, gf_ref, rcw_ref, rcb_ref, wa_ref, wx_ref, rba_ref, rbx_ref, lam_ref,
               mcw_ref, mcb_ref, wq_ref, wkt_ref, wv_ref, bi_ref, bf_ref, mng_ref,
               y_ref, rgh_ref, rgc_ref, mcv_ref, st_ref, m_ref,
               cbuf, a_s, g_s, q_s, kt_s, v_s, row_s, *, pad):
    nb = y_ref.shape[0]
    lt = CHUNK
    rows = nb * lt
    c = pl.program_id(0)

    @pl.when(c == 0)
    def _init():
        rgh_ref[...] = jnp.zeros_like(rgh_ref)
        rgc_ref[...] = jnp.zeros_like(rgc_ref)
        mcv_ref[...] = jnp.zeros_like(mcv_ref)
        st_ref[...] = jnp.zeros_like(st_ref)
        m_ref[...] = jnp.zeros_like(m_ref)
        ones_col = (lax.broadcasted_iota(jnp.int32, (rows, ML_HD), 1) == 0).astype(BF16)
        for h in range(ML_HEADS):
            v_s[h, :, ML_HD:S_COLS] = ones_col

    def conv(col0, w_ref, b_ref, tail_ref):
        x = p_ref[:, col0:col0 + 512].reshape(nb, lt, 512)
        cbuf[:, 0:SUBLANES, :] = tail_ref[...]
        cbuf[:, SUBLANES:SUBLANES + lt, :] = x
        tail_ref[...] = x[:, lt - SUBLANES:lt, :]
        acc = b_ref[...].reshape(1, 1, 512) + w_ref[CONV_W - 1:CONV_W, :].reshape(1, 1, 512) * x
        for k in range(1, CONV_W):
            wk = w_ref[CONV_W - 1 - k:CONV_W - k, :].reshape(1, 1, 512)
            acc = acc + wk * cbuf[:, SUBLANES - k:SUBLANES - k + lt, :]
        return acc.reshape(rows, 512), x.reshape(rows, 512)

    xc, _ = conv(0, rcw_ref, rcb_ref, rgc_ref)
    for g in range(N_GROUPS):
        a, gx = _rg_gates(xc, wa_ref, wx_ref, rba_ref, rbx_ref, lam_ref, g)
        for b in range(nb):
            a_s[g, b * PITCH:b * PITCH + lt, :] = a[b * lt:(b + 1) * lt]
            g_s[g, b * PITCH:b * PITCH + lt, :] = gx[b * lt:(b + 1) * lt]

    if pad:
        @pl.when(c == 0)
        def _mask_scan_input():
            for g in range(N_GROUPS):
                for b in range(nb):
                    g_s[g, b * PITCH:b * PITCH + pad, :] = jnp.zeros((pad, LANES), F32)

    def scan_step(t, hs):
        out = []
        for g in range(N_GROUPS):
            idx = pl.ds(t, nb, stride=PITCH)
            hn = a_s[g, idx, :] * hs[g] + g_s[g, idx, :]
            g_s[g, idx, :] = hn
            out.append(hn)
        return tuple(out)

    h0 = tuple(rgh_ref[:, g * LANES:(g + 1) * LANES] for g in range(N_GROUPS))
    hs = lax.fori_loop(0, lt, scan_step, h0, unroll=8)
    for g in range(N_GROUPS):
        rgh_ref[:, g * LANES:(g + 1) * LANES] = hs[g]

    for b in range(nb):
        hb = jnp.concatenate([g_s[g, b * PITCH:b * PITCH + lt, :] for g in range(N_GROUPS)], axis=1)
        gate = jax.nn.gelu(p_ref[b * lt:(b + 1) * lt, D_RG:2 * D_RG])
        y_ref[b, :, 0:D_RG] = (hb * gate).astype(BF16)

    mconv, mx = conv(2 * D_RG, mcw_ref, mcb_ref, mcv_ref)
    mcb = jax.nn.silu(mconv).astype(BF16)
    mxb = mx.astype(BF16)
    for h in range(ML_HEADS):
        sl = slice(h * ML_HD, (h + 1) * ML_HD)
        q_s[h] = jnp.dot(mcb[:, sl], wq_ref[h], preferred_element_type=F32) * (ML_HD ** -0.5)
        kt = lax.dot_general(wkt_ref[h], mcb[:, sl], _NT, preferred_element_type=F32)
        for b in range(nb):
            kt_s[h, b] = kt[:, b * lt:(b + 1) * lt]
        v_s[h, :, 0:ML_HD] = jnp.dot(mxb[:, sl], wv_ref[h], preferred_element_type=F32).astype(BF16)

    ig = gi_ref[...] + bi_ref[...]
    lf = jax.nn.log_sigmoid(gf_ref[...] + bf_ref[...])
    if pad:
        lane = lax.broadcasted_iota(jnp.int32, ig.shape, 1)
        is_pad = lane < jnp.where(c == 0, pad, 0)
        ig = jnp.where(is_pad, -1e30, ig)
        lf = jnp.where(is_pad, 0.0, lf)
    bcs = _lane_scan(lf, jnp.add, 0.0)
    gg = ig - bcs
    cm = _lane_scan(gg, jnp.maximum, -jnp.inf)
    m0 = m_ref[...]
    mm = jnp.maximum(m0, cm)
    mt = bcs + mm
    b_last = bcs[:, lt - 1:lt]
    m_last = mt[:, lt - 1:lt]
    row_s[0] = gg
    row_s[1] = mm
    row_s[2] = jnp.exp(m0 - mm)
    row_s[3] = jnp.exp(-mt)
    row_s[4] = jnp.exp(b_last - m_last + gg)
    row_s[5] = jnp.broadcast_to(jnp.exp(b_last + m0[:, 0:1] - m_last), m0.shape)
    m_ref[...] = jnp.broadcast_to(m_last, m0.shape)

    t_i = lax.broadcasted_iota(jnp.int32, (lt, lt), 0)
    s_i = lax.broadcasted_iota(jnp.int32, (lt, lt), 1)
    causal = s_i <= t_i

    def seq_body(b, carry):
        r0 = pl.multiple_of(b * GATE_ROWS, GATE_ROWS)
        trows = pl.ds(pl.multiple_of(b * lt, lt), lt)
        gsl = pl.ds(r0, GATE_ROWS)
        gg_b = row_s[0, gsl, :]
        w_b = row_s[4, gsl, :]
        sc_b = row_s[5, gsl, :]
        z = jnp.concatenate([row_s[1, gsl, :], row_s[2, gsl, :], row_s[3, gsl, :],
                             jnp.zeros((lt - 3 * GATE_ROWS, lt), F32)], axis=0)
        zt = z.T
        for h in range(ML_HEADS):
            sl = slice(h * ML_HD, (h + 1) * ML_HD)
            qb = q_s[h, trows, :].astype(BF16)
            kt = kt_s[h, b]
            vext = v_s[h, trows, :]
            sidx = b * ML_HEADS + h
            s0 = st_ref[sidx]
            mm_c = zt[:, h:h + 1]
            inter_c = zt[:, GATE_ROWS + h:GATE_ROWS + h + 1]
            emm_c = zt[:, 2 * GATE_ROWS + h:2 * GATE_ROWS + h + 1]
            sc = jnp.dot(qb, kt.astype(BF16), preferred_element_type=F32)
            d = jnp.exp(jnp.where(causal, gg_b[h:h + 1, :] - mm_c, -jnp.inf))
            pm = (sc * d).astype(BF16)
            numx = (jnp.dot(pm, vext, preferred_element_type=F32)
                    + inter_c * jnp.dot(qb, s0.astype(BF16), preferred_element_type=F32))
            den = numx[:, ML_HD:ML_HD + 1]
            hout = numx[:, 0:ML_HD] / jnp.maximum(jnp.abs(den), emm_c)
            hm = jax.nn.sigmoid(p_ref[trows, 3 * D_RG + h * ML_HD:3 * D_RG + (h + 1) * ML_HD]) * hout
            hm = hm * lax.rsqrt(jnp.mean(hm * hm, axis=-1, keepdims=True) + EPS)
            y_ref[b, :, D_RG + h * ML_HD:D_RG + (h + 1) * ML_HD] = (hm * mng_ref[:, sl]).astype(BF16)
            wkt = (kt * w_b[h:h + 1, :]).astype(BF16)
            sc_row = jnp.concatenate([sc_b[h:h + 1, :], sc_b[h:h + 1, :]], axis=1)
            st_ref[sidx] = sc_row * s0 + jnp.dot(wkt, vext, preferred_element_type=F32)
        return carry

    lax.fori_loop(0, nb, seq_body, 0)

    if pad:
        @pl.when(c == 0)
        def _zero_pad_rows():
            y_ref[:, 0:pad, :] = jnp.zeros((nb, pad, y_ref.shape[2]), BF16)


def _pmix(p, gi, gf, lw, *, nb, nc, pad, name):
    rows = nb * CHUNK
    ng = nb * GATE_ROWS
    in_arrays = [p, gi, gf, lw["rcw"], lw["rcb"], lw["wa"], lw["wx"], lw["rba"], lw["rbx"], lw["lam"],
                 lw["mcw"], lw["mcb"], lw["wq"], lw["wkt"], lw["wv"], lw["bi"], lw["bf"], lw["mng"]]
    in_specs = [
        pl.BlockSpec((rows, D_MAIN), lambda c: (c, 0)),
        pl.BlockSpec((ng, CHUNK), lambda c: (c, 0)),
        pl.BlockSpec((ng, CHUNK), lambda c: (c, 0)),
    ] + [_const_spec(a.shape) for a in in_arrays[3:]]
    out_shape = [
        jax.ShapeDtypeStruct((nb, nc * CHUNK, D_MODEL), BF16),
        jax.ShapeDtypeStruct((nb, D_RG), F32),
        jax.ShapeDtypeStruct((nb, SUBLANES, D_RG), F32),
        jax.ShapeDtypeStruct((nb, SUBLANES, D_ML), F32),
        jax.ShapeDtypeStruct((nb * ML_HEADS, ML_HD, S_COLS), F32),
        jax.ShapeDtypeStruct((ng, CHUNK), F32),
    ]
    out_specs = [pl.BlockSpec((nb, CHUNK, D_MODEL), lambda c: (0, c, 0))] + [
        pl.BlockSpec(s.shape, lambda c, n=len(s.shape): (0,) * n) for s in out_shape[1:]]
    scratch = [
        pltpu.VMEM((nb, PITCH, 512), F32),
        pltpu.VMEM((N_GROUPS, nb * PITCH, LANES), F32),
        pltpu.VMEM((N_GROUPS, nb * PITCH, LANES), F32),
        pltpu.VMEM((ML_HEADS, rows, ML_HD), F32),
        pltpu.VMEM((ML_HEADS, nb, ML_HD, CHUNK), F32),
        pltpu.VMEM((ML_HEADS, rows, S_COLS), BF16),
        pltpu.VMEM((6, ng, CHUNK), F32),
    ]
    return pl.pallas_call(
        functools.partial(_pmix_body, pad=pad),
        grid=(nc,),
        in_specs=in_specs,
        out_specs=out_specs,
        out_shape=out_shape,
        scratch_shapes=scratch,
        compiler_params=pltpu.CompilerParams(dimension_semantics=("arbitrary",), vmem_limit_bytes=VMEM_LIMIT),
        name=name,
    )(*in_arrays)


SBLK = SUBLANES


def _smix_body(p_ref, gi_ref, gf_ref, rgh_ref, rgc_ref, mcv_ref, c_ref, n_ref, mcol_ref, mrow_ref,
               rcw_ref, rcb_ref, wa_ref, wx_ref, rba_ref, rbx_ref, lam_ref,
               mcw_ref, mcb_ref, wq_ref, wk_ref, wv_ref, bi_ref, bf_ref, bic_ref, bfc_ref, mng_ref,
               y_ref, rgh_o, rgc_o, mcv_o, c_o, n_o, m_o,
               q_s, dk_s, v_s, qc_s, col_s):
    i = pl.program_id(0)
    ns = p_ref.shape[0]

    def conv(col0, w_ref, b_ref, tail_ref, tail_o):
        x = p_ref[:, col0:col0 + 512]
        acc = b_ref[...] + w_ref[CONV_W - 1:CONV_W, :] * x
        for j in range(CONV_W - 1):
            acc = acc + w_ref[j:j + 1, :] * tail_ref[j]
        for j in range(CONV_W - 2):
            tail_o[j] = tail_ref[j + 1]
        tail_o[CONV_W - 2] = x
        return acc, x

    @pl.when(i == 0)
    def _rowwise():
        xc, _ = conv(0, rcw_ref, rcb_ref, rgc_ref, rgc_o)
        for g in range(N_GROUPS):
            sl = slice(g * LANES, (g + 1) * LANES)
            a, gx = _rg_gates(xc, wa_ref, wx_ref, rba_ref, rbx_ref, lam_ref, g)
            hn = a * rgh_ref[:, sl] + gx
            rgh_o[:, sl] = hn
            y_ref[0, :, sl] = (hn * jax.nn.gelu(p_ref[:, D_RG + g * LANES:D_RG + (g + 1) * LANES])).astype(BF16)

        mconv, mx = conv(2 * D_RG, mcw_ref, mcb_ref, mcv_ref, mcv_o)
        mcb = jax.nn.silu(mconv).astype(BF16)
        mxb = mx.astype(BF16)

        ig_r = gi_ref[...] + bi_ref[...]
        lf_r = jax.nn.log_sigmoid(gf_ref[...] + bf_ref[...])
        m0_r = mrow_ref[...]
        m_o[...] = jnp.maximum(lf_r + m0_r, ig_r)
        z = jnp.concatenate([gi_ref[...], gf_ref[...], jnp.zeros((ns - 2 * GATE_ROWS, ns), F32)], axis=0).T
        ig_c = z[:, 0:GATE_ROWS] + bic_ref[...]
        lf_c = jax.nn.log_sigmoid(z[:, GATE_ROWS:2 * GATE_ROWS] + bfc_ref[...])
        m0_c = mcol_ref[...]
        m_c = jnp.maximum(lf_c + m0_c, ig_c)
        inter_c = jnp.exp(lf_c + m0_c - m_c)
        dd_c = jnp.exp(ig_c - m_c)
        col_s[0] = inter_c
        col_s[1] = dd_c
        col_s[2] = jnp.exp(-m_c)
        for h in range(ML_HEADS):
            sl = slice(h * ML_HD, (h + 1) * ML_HD)
            q = jnp.dot(mcb[:, sl], wq_ref[h], preferred_element_type=F32) * (ML_HD ** -0.5)
            k = jnp.dot(mcb[:, sl], wk_ref[h], preferred_element_type=F32)
            v = jnp.dot(mxb[:, sl], wv_ref[h], preferred_element_type=F32)
            q_s[h] = q
            dk_s[h] = dd_c[:, h:h + 1] * k
            v_s[h] = v
            col_s[3, :, h:h + 1] = jnp.sum(q * k, axis=-1, keepdims=True) * dd_c[:, h:h + 1]
            n0 = n_ref[h]
            col_s[4, :, h:h + 1] = jnp.sum(q * n0, axis=-1, keepdims=True)
            n_o[h] = inter_c[:, h:h + 1] * n0 + dd_c[:, h:h + 1] * k

    r0 = pl.multiple_of(i * SBLK, SBLK)
    blk = pl.ds(r0, SBLK)
    inter_rows = [jnp.broadcast_to(col_s[0, blk, h:h + 1], (SBLK, LANES)) for h in range(ML_HEADS)]
    for h in range(ML_HEADS):
        z = jnp.concatenate([q_s[h, blk, :], dk_s[h, blk, :],
                             jnp.zeros((ML_HD - 2 * SBLK, ML_HD), F32)], axis=0).T
        vblk = v_s[h, blk, :]
        qc_rows = []
        for j in range(SBLK):
            c0 = c_ref[j, h]
            qcol = z[:, j:j + 1]
            kcol = z[:, SBLK + j:SBLK + j + 1]
            qc_rows.append(jnp.sum(qcol * c0, axis=0, keepdims=True))
            c_o[j, h] = inter_rows[h][j:j + 1, :] * c0 + kcol * vblk[j:j + 1, :]
        qc_s[h, blk, :] = jnp.concatenate(qc_rows, axis=0)

    @pl.when(i == pl.num_programs(0) - 1)
    def _finish():
        for h in range(ML_HEADS):
            sl = slice(h * ML_HD, (h + 1) * ML_HD)
            inter = col_s[0, :, h:h + 1]
            s = col_s[3, :, h:h + 1]
            num = s * v_s[h] + inter * qc_s[h]
            den = s + inter * col_s[4, :, h:h + 1]
            hout = num / jnp.maximum(jnp.abs(den), col_s[2, :, h:h + 1])
            hm = jax.nn.sigmoid(p_ref[:, 3 * D_RG + h * ML_HD:3 * D_RG + (h + 1) * ML_HD]) * hout
            hm = hm * lax.rsqrt(jnp.mean(hm * hm, axis=-1, keepdims=True) + EPS)
            y_ref[0, :, D_RG + h * ML_HD:D_RG + (h + 1) * ML_HD] = (hm * mng_ref[:, sl]).astype(BF16)


def _smix(p, gi, gf, st, lw, *, name):
    ns = p.shape[0]
    rgh, rgc, mcv, mc, mn, mcol, mrow = st
    in_arrays = [p, gi, gf, rgh, rgc, mcv, mc, mn, mcol, mrow,
                 lw["rcw"], lw["rcb"], lw["wa"], lw["wx"], lw["rba"], lw["rbx"], lw["lam"],
                 lw["mcw"], lw["mcb"], lw["wq"], lw["wk"], lw["wv"], lw["bi_s"], lw["bf_s"],
                 lw["bi_c"], lw["bf_c"], lw["mng"]]
    cspec = pl.BlockSpec((SBLK, ML_HEADS, ML_HD, ML_HD), lambda i: (i, 0, 0, 0))
    in_specs = [_const_spec(a.shape) for a in in_arrays]
    in_specs[6] = cspec
    out_shape = [
        jax.ShapeDtypeStruct((1, ns, D_MODEL), BF16),
        jax.ShapeDtypeStruct((ns, D_RG), F32),
        jax.ShapeDtypeStruct((CONV_W - 1, ns, D_RG), F32),
        jax.ShapeDtypeStruct((CONV_W - 1, ns, D_ML), F32),
        jax.ShapeDtypeStruct((ns, ML_HEADS, ML_HD, ML_HD), F32),
        jax.ShapeDtypeStruct((ML_HEADS, ns, ML_HD), F32),
        jax.ShapeDtypeStruct((GATE_ROWS, ns), F32),
    ]
    out_specs = [pl.BlockSpec(s.shape, lambda i, n=len(s.shape): (0,) * n) for s in out_shape]
    out_specs[4] = cspec
    scratch = [
        pltpu.VMEM((ML_HEADS, ns, ML_HD), F32),
        pltpu.VMEM((ML_HEADS, ns, ML_HD), F32),
        pltpu.VMEM((ML_HEADS, ns, ML_HD), F32),
        pltpu.VMEM((ML_HEADS, ns, ML_HD), F32),
        pltpu.VMEM((5, ns, GATE_ROWS), F32),
    ]
    return pl.pallas_call(
        _smix_body,
        grid=(ns // SBLK,),
        in_specs=in_specs,
        out_specs=out_specs,
        out_shape=out_shape,
        scratch_shapes=scratch,
        compiler_params=pltpu.CompilerParams(dimension_semantics=("arbitrary",), vmem_limit_bytes=VMEM_LIMIT),
        name=name,
    )(*in_arrays)


def _block_diag_pairs(w):
    w = w.reshape(N_GROUPS, 2, RG_BLOCK, RG_BLOCK)
    z = jnp.zeros((N_GROUPS, 2, RG_BLOCK, 2, RG_BLOCK), w.dtype)
    z = z.at[:, 0, :, 0, :].set(w[:, 0]).at[:, 1, :, 1, :].set(w[:, 1])
    return z.reshape(N_GROUPS, LANES, LANES)


def _gate_rows(w_cols):
    return jnp.concatenate([w_cols.T, jnp.zeros((GATE_ROWS - ML_HEADS, D_MODEL), w_cols.dtype)], axis=0)


def _layer_weights(l, ln1_g, w_in, rg_conv_w, rg_conv_b, rg_w_a, rg_w_x, rg_b_a, rg_b_x, rg_lambda,
                   m_conv_w, m_conv_b, m_w_q, m_w_k, m_w_v, m_b_i, m_b_f, m_norm_g, w_out, ln2_g,
                   w_ff1, w_ff2, nb_prompt, ns):
    row = lambda a: a[l].reshape(1, -1)
    bias8 = jnp.concatenate([m_b_i[l], jnp.zeros((GATE_ROWS - ML_HEADS,), F32)])
    bfor8 = jnp.concatenate([m_b_f[l], jnp.zeros((GATE_ROWS - ML_HEADS,), F32)])
    return dict(
        ln1=row(ln1_g),
        w_main=w_in[l, :, :D_MAIN].astype(BF16),
        wgi=_gate_rows(w_in[l, :, D_MAIN:D_MAIN + ML_HEADS]).astype(BF16),
        wgf=_gate_rows(w_in[l, :, D_MAIN + ML_HEADS:]).astype(BF16),
        rcw=rg_conv_w[l], rcb=row(rg_conv_b),
        wa=_block_diag_pairs(rg_w_a[l]).astype(BF16), wx=_block_diag_pairs(rg_w_x[l]).astype(BF16),
        rba=row(rg_b_a), rbx=row(rg_b_x), lam=row(rg_lambda),
        mcw=m_conv_w[l], mcb=row(m_conv_b),
        wq=m_w_q[l].astype(BF16), wk=m_w_k[l].astype(BF16), wkt=jnp.swapaxes(m_w_k[l], 1, 2).astype(BF16),
        wv=m_w_v[l].astype(BF16),
        bi=jnp.broadcast_to(jnp.tile(bias8, nb_prompt)[:, None], (nb_prompt * GATE_ROWS, CHUNK)),
        bf=jnp.broadcast_to(jnp.tile(bfor8, nb_prompt)[:, None], (nb_prompt * GATE_ROWS, CHUNK)),
        bi_s=jnp.broadcast_to(bias8[:, None], (GATE_ROWS, ns)),
        bf_s=jnp.broadcast_to(bfor8[:, None], (GATE_ROWS, ns)),
        bi_c=bias8.reshape(1, GATE_ROWS), bf_c=bfor8.reshape(1, GATE_ROWS),
        mng=row(m_norm_g),
        w_out=w_out[l].astype(BF16), ln2=row(ln2_g),
        w_ff1=w_ff1[l].astype(BF16), w_ff2=w_ff2[l].astype(BF16),
    )


def kernel(x_prompt, x_sample, state_rg_h, state_rg_conv, state_m_conv, state_m_C, state_m_n, state_m_m,
           meta_tokens, ln1_g, w_in, rg_conv_w, rg_conv_b, rg_w_a, rg_w_x, rg_b_a, rg_b_x, rg_lambda,
           m_conv_w, m_conv_b, m_w_q, m_w_k, m_w_v, m_b_i, m_b_f, m_norm_g, w_out, ln2_g,
           w_ff1, w_ff2, ln_f_g):
    nb, seq, d = x_prompt.shape
    ns = x_sample.shape[0]
    t_real = N_META + seq
    nc = -(-t_real // CHUNK)
    pad = nc * CHUNK - t_real
    assert (pad + N_META) % CHUNK == 0 and x_sample.shape[1] == 1 and ns == LANES

    lnf = ln_f_g.reshape(1, d)
    xp = jnp.concatenate([jnp.zeros((nb, pad, d), F32),
                          jnp.broadcast_to(meta_tokens.astype(F32)[None], (nb, N_META, d)),
                          x_prompt], axis=1)
    xs = x_sample.reshape(1, ns, d)

    p_states = [[] for _ in range(6)]
    s_states = [[] for _ in range(6)]
    for l in range(DEPTH):
        lw = _layer_weights(l, ln1_g, w_in, rg_conv_w, rg_conv_b, rg_w_a, rg_w_x, rg_b_a, rg_b_x, rg_lambda,
                            m_conv_w, m_conv_b, m_w_q, m_w_k, m_w_v, m_b_i, m_b_f, m_norm_g, w_out, ln2_g,
                            w_ff1, w_ff2, nb, ns)
        last = l == DEPTH - 1

        p, gi, gf = _inproj(xp, lw["ln1"], lw["w_main"], lw["wgi"], lw["wgf"], lt=CHUNK, name=f"inproj_p{l}")
        y, rgh, rgc, mcv, st, mrow = _pmix(p, gi, gf, lw, nb=nb, nc=nc, pad=pad, name=f"mixer_p{l}")
        lt_out = CHUNK // 2
        skip = (pad + N_META) // lt_out if last else 0
        xp = _outmlp(xp, y, lw["w_out"], lw["ln2"], lw["w_ff1"], lw["w_ff2"], lnf,
                     lt=lt_out, final=last, skip=skip, name=f"outmlp_p{l}")
        st = st.reshape(nb, ML_HEADS, ML_HD, S_COLS)
        p_states[0].append(rgh)
        p_states[1].append(rgc[:, SUBLANES - (CONV_W - 1):, :])
        p_states[2].append(mcv[:, SUBLANES - (CONV_W - 1):, :])
        p_states[3].append(st[..., :ML_HD])
        p_states[4].append(st[..., ML_HD])
        p_states[5].append(mrow.reshape(nb, GATE_ROWS, CHUNK)[:, :ML_HEADS, 0])

        ps, gis, gfs = _inproj(xs, lw["ln1"], lw["w_main"], lw["wgi"], lw["wgf"], lt=ns, name=f"inproj_s{l}")
        m_cols = jnp.concatenate([state_m_m[l], jnp.zeros((ns, GATE_ROWS - ML_HEADS), F32)], axis=1)
        st_in = (state_rg_h[l], jnp.swapaxes(state_rg_conv[l], 0, 1), jnp.swapaxes(state_m_conv[l], 0, 1),
                 state_m_C[l], jnp.swapaxes(state_m_n[l], 0, 1), m_cols, m_cols.T)
        ys, srgh, srgc, smcv, smc, smn, smrow = _smix(ps, gis, gfs, st_in, lw, name=f"mixer_s{l}")
        xs = _outmlp(xs, ys, lw["w_out"], lw["ln2"], lw["w_ff1"], lw["w_ff2"], lnf,
                     lt=ns, final=last, skip=0, name=f"outmlp_s{l}")
        s_states[0].append(srgh)
        s_states[1].append(jnp.swapaxes(srgc, 0, 1))
        s_states[2].append(jnp.swapaxes(smcv, 0, 1))
        s_states[3].append(smc)
        s_states[4].append(jnp.swapaxes(smn, 0, 1))
        s_states[5].append(smrow[:ML_HEADS].T)

    y_prompt = xp
    y_sample = xs.reshape(ns, 1, d)
    ps_out = [jnp.stack(s) for s in p_states]
    ss_out = [jnp.stack(s) for s in s_states]
    return (y_prompt, y_sample, *ps_out, *ss_out)
```

```python
import functools

import jax
import jax.numpy as jnp
from jax import lax
from jax.experimental import pallas as pl
from jax.experimental.pallas import tpu as pltpu

F32 = jnp.float32
BF16 = jnp.bfloat16

D_MODEL = 1024
N_META = 16
D_RG = 512
RG_BLOCK = 64
RG_C = 8.0
D_ML = 512
ML_HEADS = 4
ML_HD = 128
CONV_W = 4
D_FF = 4096
EPS = 1e-6
DEPTH = 4

LANES = 128
SUBLANES = 8
CHUNK = 128
PITCH = CHUNK + SUBLANES
N_GROUPS = D_RG // LANES
D_MAIN = 2 * D_RG + 2 * D_ML
GATE_ROWS = SUBLANES
S_COLS = 2 * ML_HD
FF_TILE = 1024
VMEM_LIMIT = 56 * 1024 * 1024

_NT = (((1,), (1,)), ((), ()))


def _const_spec(shape):
    zeros = (0,) * len(shape)
    return pl.BlockSpec(shape, lambda *_: zeros, pipeline_mode=pl.Buffered(1))


def _rms(x, g):
    return x * lax.rsqrt(jnp.mean(x * x, axis=-1, keepdims=True) + EPS) * g


def _sigmoid(x):
    return 0.5 * jnp.tanh(0.5 * x) + 0.5


def _sqrt_nonneg(s):
    return jnp.where(s > 0.0, s * lax.rsqrt(s), 0.0)


def _inproj_body(x_ref, g_ref, w_ref, wgi_ref, wgf_ref, p_ref, gi_ref, gf_ref):
    nb, lt, d = x_ref.shape
    x = x_ref[...].reshape(nb * lt, d)
    u = _rms(x, g_ref[...]).astype(BF16)
    for n in range(D_MAIN // 512):
        cols = slice(n * 512, (n + 1) * 512)
        p_ref[:, cols] = jnp.dot(u, w_ref[:, cols], preferred_element_type=F32)
    for b in range(nb):
        ub = u[b * lt:(b + 1) * lt]
        rows = slice(b * GATE_ROWS, (b + 1) * GATE_ROWS)
        gi_ref[rows, :] = lax.dot_general(wgi_ref[...], ub, _NT, preferred_element_type=F32)
        gf_ref[rows, :] = lax.dot_general(wgf_ref[...], ub, _NT, preferred_element_type=F32)


def _inproj(x, g, w, wgi, wgf, *, lt, name):
    nb, t, d = x.shape
    nc = t // lt
    rows = nb * lt
    return pl.pallas_call(
        _inproj_body,
        grid=(nc,),
        in_specs=[
            pl.BlockSpec((nb, lt, d), lambda c: (0, c, 0)),
            _const_spec((1, d)),
            _const_spec((d, D_MAIN)),
            _const_spec((GATE_ROWS, d)),
            _const_spec((GATE_ROWS, d)),
        ],
        out_specs=[
            pl.BlockSpec((rows, D_MAIN), lambda c: (c, 0)),
            pl.BlockSpec((nb * GATE_ROWS, lt), lambda c: (c, 0)),
            pl.BlockSpec((nb * GATE_ROWS, lt), lambda c: (c, 0)),
        ],
        out_shape=[
            jax.ShapeDtypeStruct((nc * rows, D_MAIN), F32),
            jax.ShapeDtypeStruct((nc * nb * GATE_ROWS, lt), F32),
            jax.ShapeDtypeStruct((nc * nb * GATE_ROWS, lt), F32),
        ],
        compiler_params=pltpu.CompilerParams(dimension_semantics=("arbitrary",), vmem_limit_bytes=VMEM_LIMIT),
        name=name,
    )(x, g, w, wgi, wgf)


def _outmlp_body(x_ref, y_ref, wo_ref, g2_ref, w1_ref, w2_ref, gf_ref, o_ref, *, final, nzero):
    nb, lt, d = x_ref.shape
    rows = nb * lt

    def compute():
        x = x_ref[...].reshape(rows, d)
        y = y_ref[...].reshape(rows, d)
        x1 = x + jnp.dot(y, wo_ref[...], preferred_element_type=F32)
        u2 = _rms(x1, g2_ref[...]).astype(BF16)
        acc = x1
        for c in range(D_FF // FF_TILE):
            cols = slice(c * FF_TILE, (c + 1) * FF_TILE)
            h = jnp.dot(u2, w1_ref[:, cols], preferred_element_type=F32)
            h = jnp.square(jnp.maximum(h, 0.0)).astype(BF16)
            acc = acc + jnp.dot(h, w2_ref[cols, :], preferred_element_type=F32)
        if final:
            acc = _rms(acc, gf_ref[...])
        o_ref[...] = acc.reshape(nb, lt, d)

    def zero():
        o_ref[...] = jnp.zeros(o_ref.shape, F32)

    if nzero:
        i = pl.program_id(0)
        pl.when(i < nzero)(zero)
        pl.when(i >= nzero)(compute)
    else:
        compute()


def _outmlp(x, y, wo, g2, w1, w2, gf, *, lt, final, skip, nzero, name):
    nb, t, d = x.shape
    nsteps = t // lt - skip
    return pl.pallas_call(
        functools.partial(_outmlp_body, final=final, nzero=nzero),
        grid=(nsteps,),
        in_specs=[
            pl.BlockSpec((nb, lt, d), lambda i: (0, i + skip, 0)),
            pl.BlockSpec((nb, lt, d), lambda i: (0, i + skip, 0)),
            _const_spec((d, d)),
            _const_spec((1, d)),
            _const_spec((d, D_FF)),
            _const_spec((D_FF, d)),
            _const_spec((1, d)),
        ],
        out_specs=pl.BlockSpec((nb, lt, d), lambda i: (0, i, 0)),
        out_shape=jax.ShapeDtypeStruct((nb, nsteps * lt, d), F32),
        compiler_params=pltpu.CompilerParams(dimension_semantics=("arbitrary",), vmem_limit_bytes=VMEM_LIMIT),
        name=name,
    )(x, y, wo, g2, w1, w2, gf)


def _rg_gates(xc, wa_ref, wx_ref, rba_ref, rbx_ref, lam_ref, g):
    sl = slice(g * LANES, (g + 1) * LANES)
    xg = xc[:, sl]
    xb = xg.astype(BF16)
    r = _sigmoid(jnp.dot(xb, wa_ref[g], preferred_element_type=F32) + rba_ref[:, sl])
    i = _sigmoid(jnp.dot(xb, wx_ref[g], preferred_element_type=F32) + rbx_ref[:, sl])
    log_a = (RG_C * r) * jax.nn.log_sigmoid(lam_ref[:, sl])
    a = jnp.exp(log_a)
    gx = _sqrt_nonneg(1.0 - a * a) * (i * xg)
    return a, gx


def _lane_scan(x, op, fill):
    lane = lax.broadcasted_iota(jnp.int32, x.shape, 1)
    sh = 1
    while sh < x.shape[1]:
        x = op(x, jnp.where(lane >= sh, pltpu.roll(x, sh, 1), fill))
        sh *= 2
    return x


def _pmix_body(p_ref, gi_ref, gf_ref, rcw_ref, rcb_ref, wa_ref, wx_ref, rba_ref, rbx_ref, lam_ref,
               mcw_ref, mcb_ref, wq_ref, wkt_ref, wv_ref, bi_ref, bf_ref, mng_ref,
               y_ref, rgh_ref, rgc_ref, mcv_ref, st_ref, m_ref,
               cbuf, a_s, g_s, q_s, kt_s, v_s, row_s, *, pad):
    nb = y_ref.shape[0]
    lt = CHUNK
    rows = nb * lt
    c = pl.program_id(0)

    @pl.when(c == 0)
    def _init():
        rgh_ref[...] = jnp.zeros_like(rgh_ref)
        rgc_ref[...] = jnp.zeros_like(rgc_ref)
        mcv_ref[...] = jnp.zeros_like(mcv_ref)
        st_ref[...] = jnp.zeros_like(st_ref)
        m_ref[...] = jnp.zeros_like(m_ref)
        ones_col = (lax.broadcasted_iota(jnp.int32, (rows, ML_HD), 1) == 0).astype(BF16)
        for h in range(ML_HEADS):
            v_s[h, :, ML_HD:S_COLS] = ones_col

    def conv(col0, w_ref, b_ref, tail_ref):
        x = p_ref[:, col0:col0 + 512].reshape(nb, lt, 512)
        cbuf[:, 0:SUBLANES, :] = tail_ref[...]
        cbuf[:, SUBLANES:SUBLANES + lt, :] = x
        tail_ref[...] = x[:, lt - SUBLANES:lt, :]
        acc = b_ref[...].reshape(1, 1, 512) + w_ref[CONV_W - 1:CONV_W, :].reshape(1, 1, 512) * x
        for k in range(1, CONV_W):
            wk = w_ref[CONV_W - 1 - k:CONV_W - k, :].reshape(1, 1, 512)
            acc = acc + wk * cbuf[:, SUBLANES - k:SUBLANES - k + lt, :]
        return acc.reshape(rows, 512), x.reshape(rows, 512)

    xc, _ = conv(0, rcw_ref, rcb_ref, rgc_ref)
    for g in range(N_GROUPS):
        a, gx = _rg_gates(xc, wa_ref, wx_ref, rba_ref, rbx_ref, lam_ref, g)
        for b in range(nb):
            a_s[g, b * PITCH:b * PITCH + lt, :] = a[b * lt:(b + 1) * lt]
            g_s[g, b * PITCH:b * PITCH + lt, :] = gx[b * lt:(b + 1) * lt]

    if pad:
        @pl.when(c == 0)
        def _mask_scan_input():
            for g in range(N_GROUPS):
                for b in range(nb):
                    g_s[g, b * PITCH:b * PITCH + pad, :] = jnp.zeros((pad, LANES), F32)

    def scan_step(t, hs):
        out = []
        for g in range(N_GROUPS):
            idx = pl.ds(t, nb, stride=PITCH)
            hn = a_s[g, idx, :] * hs[g] + g_s[g, idx, :]
            g_s[g, idx, :] = hn
            out.append(hn)
        return tuple(out)

    h0 = tuple(rgh_ref[:, g * LANES:(g + 1) * LANES] for g in range(N_GROUPS))
    hs = lax.fori_loop(0, lt, scan_step, h0, unroll=8)
    for g in range(N_GROUPS):
        rgh_ref[:, g * LANES:(g + 1) * LANES] = hs[g]

    for b in range(nb):
        hb = jnp.concatenate([g_s[g, b * PITCH:b * PITCH + lt, :] for g in range(N_GROUPS)], axis=1)
        gate = jax.nn.gelu(p_ref[b * lt:(b + 1) * lt, D_RG:2 * D_RG])
        y_ref[b, :, 0:D_RG] = (hb * gate).astype(BF16)

    mconv, mx = conv(2 * D_RG, mcw_ref, mcb_ref, mcv_ref)
    mcb = (mconv * _sigmoid(mconv)).astype(BF16)
    mxb = mx.astype(BF16)
    for h in range(ML_HEADS):
        sl = slice(h * ML_HD, (h + 1) * ML_HD)
        q_s[h] = jnp.dot(mcb[:, sl], wq_ref[h], preferred_element_type=F32) * (ML_HD ** -0.5)
        kt = lax.dot_general(wkt_ref[h], mcb[:, sl], _NT, preferred_element_type=F32)
        for b in range(nb):
            kt_s[h, b] = kt[:, b * lt:(b + 1) * lt]
        v_s[h, :, 0:ML_HD] = jnp.dot(mxb[:, sl], wv_ref[h], preferred_element_type=F32).astype(BF16)

    ig = gi_ref[...] + bi_ref[...]
    lf = jax.nn.log_sigmoid(gf_ref[...] + bf_ref[...])
    if pad:
        lane = lax.broadcasted_iota(jnp.int32, ig.shape, 1)
        is_pad = lane < jnp.where(c == 0, pad, 0)
        ig = jnp.where(is_pad, -1e30, ig)
        lf = jnp.where(is_pad, 0.0, lf)
    bcs = _lane_scan(lf, jnp.add, 0.0)
    gg = ig - bcs
    cm = _lane_scan(gg, jnp.maximum, -jnp.inf)
    m0 = m_ref[...]
    mm = jnp.maximum(m0, cm)
    mt = bcs + mm
    b_last = bcs[:, lt - 1:lt]
    m_last = mt[:, lt - 1:lt]
    row_s[0] = gg
    row_s[1] = mm
    row_s[2] = jnp.exp(m0 - mm)
    row_s[3] = jnp.exp(-mt)
    row_s[4] = jnp.exp(b_last - m_last + gg)
    row_s[5] = jnp.broadcast_to(jnp.exp(b_last + m0[:, 0:1] - m_last), m0.shape)
    m_ref[...] = jnp.broadcast_to(m_last, m0.shape)

    t_i = lax.broadcasted_iota(jnp.int32, (lt, lt), 0)
    s_i = lax.broadcasted_iota(jnp.int32, (lt, lt), 1)
    causal = s_i <= t_i

    def seq_body(b, carry):
        trows = pl.ds(pl.multiple_of(b * lt, lt), lt)
        gsl = pl.ds(pl.multiple_of(b * GATE_ROWS, GATE_ROWS), GATE_ROWS)
        gg_b = row_s[0, gsl, :]
        w_b = row_s[4, gsl, :]
        sc_b = row_s[5, gsl, :]
        z = jnp.concatenate([row_s[1, gsl, :], row_s[2, gsl, :], row_s[3, gsl, :],
                             jnp.zeros((lt - 3 * GATE_ROWS, lt), F32)], axis=0)
        zt = z.T
        for h in range(ML_HEADS):
            sl = slice(h * ML_HD, (h + 1) * ML_HD)
            qb = q_s[h, trows, :].astype(BF16)
            kt = kt_s[h, b]
            vext = v_s[h, trows, :]
            sidx = b * ML_HEADS + h
            s0 = st_ref[sidx]
            mm_c = zt[:, h:h + 1]
            inter_c = zt[:, GATE_ROWS + h:GATE_ROWS + h + 1]
            emm_c = zt[:, 2 * GATE_ROWS + h:2 * GATE_ROWS + h + 1]
            sc = jnp.dot(qb, kt.astype(BF16), preferred_element_type=F32)
            d = jnp.exp(jnp.where(causal, gg_b[h:h + 1, :] - mm_c, -jnp.inf))
            pm = (sc * d).astype(BF16)
            numx = (jnp.dot(pm, vext, preferred_element_type=F32)
                    + inter_c * jnp.dot(qb, s0.astype(BF16), preferred_element_type=F32))
            den = numx[:, ML_HD:ML_HD + 1]
            hout = numx[:, 0:ML_HD] / jnp.maximum(jnp.abs(den), emm_c)
            hm = _sigmoid(p_ref[trows, 3 * D_RG + h * ML_HD:3 * D_RG + (h + 1) * ML_HD]) * hout
            hm = hm * lax.rsqrt(jnp.mean(hm * hm, axis=-1, keepdims=True) + EPS)
            y_ref[b, :, D_RG + h * ML_HD:D_RG + (h + 1) * ML_HD] = (hm * mng_ref[:, sl]).astype(BF16)
            wkt = (kt * w_b[h:h + 1, :]).astype(BF16)
            sc_row = jnp.concatenate([sc_b[h:h + 1, :], sc_b[h:h + 1, :]], axis=1)
            st_ref[sidx] = sc_row * s0 + jnp.dot(wkt, vext, preferred_element_type=F32)
        return carry

    lax.fori_loop(0, nb, seq_body, 0, unroll=2)

    if pad:
        @pl.when(c == 0)
        def _zero_pad_rows():
            y_ref[:, 0:pad, :] = jnp.zeros((nb, pad, y_ref.shape[2]), BF16)


def _pmix(p, gi, gf, lw, *, nb, nc, pad, name):
    rows = nb * CHUNK
    ng = nb * GATE_ROWS
    in_arrays = [p, gi, gf, lw["rcw"], lw["rcb"], lw["wa"], lw["wx"], lw["rba"], lw["rbx"], lw["lam"],
                 lw["mcw"], lw["mcb"], lw["wq"], lw["wkt"], lw["wv"], lw["bi"], lw["bf"], lw["mng"]]
    in_specs = [
        pl.BlockSpec((rows, D_MAIN), lambda c: (c, 0)),
        pl.BlockSpec((ng, CHUNK), lambda c: (c, 0)),
        pl.BlockSpec((ng, CHUNK), lambda c: (c, 0)),
    ] + [_const_spec(a.shape) for a in in_arrays[3:]]
    out_shape = [
        jax.ShapeDtypeStruct((nb, nc * CHUNK, D_MODEL), BF16),
        jax.ShapeDtypeStruct((nb, D_RG), F32),
        jax.ShapeDtypeStruct((nb, SUBLANES, D_RG), F32),
        jax.ShapeDtypeStruct((nb, SUBLANES, D_ML), F32),
        jax.ShapeDtypeStruct((nb * ML_HEADS, ML_HD, S_COLS), F32),
        jax.ShapeDtypeStruct((ng, CHUNK), F32),
    ]
    out_specs = [pl.BlockSpec((nb, CHUNK, D_MODEL), lambda c: (0, c, 0))] + [
        pl.BlockSpec(s.shape, lambda c, n=len(s.shape): (0,) * n) for s in out_shape[1:]]
    scratch = [
        pltpu.VMEM((nb, PITCH, 512), F32),
        pltpu.VMEM((N_GROUPS, nb * PITCH, LANES), F32),
        pltpu.VMEM((N_GROUPS, nb * PITCH, LANES), F32),
        pltpu.VMEM((ML_HEADS, rows, ML_HD), F32),
        pltpu.VMEM((ML_HEADS, nb, ML_HD, CHUNK), F32),
        pltpu.VMEM((ML_HEADS, rows, S_COLS), BF16),
        pltpu.VMEM((6, ng, CHUNK), F32),
    ]
    return pl.pallas_call(
        functools.partial(_pmix_body, pad=pad),
        grid=(nc,),
        in_specs=in_specs,
        out_specs=out_specs,
        out_shape=out_shape,
        scratch_shapes=scratch,
        compiler_params=pltpu.CompilerParams(dimension_semantics=("arbitrary",), vmem_limit_bytes=VMEM_LIMIT),
        name=name,
    )(*in_arrays)


SBLK = SUBLANES


def _smix_body(p_ref, gi_ref, gf_ref, rgh_ref, rgc_ref, mcv_ref, c_ref, n_ref, mcol_ref, mrow_ref,
               rcw_ref, rcb_ref, wa_ref, wx_ref, rba_ref, rbx_ref, lam_ref,
               mcw_ref, mcb_ref, wq_ref, wk_ref, wv_ref, bi_ref, bf_ref, bic_ref, bfc_ref, mng_ref,
               y_ref, rgh_o, rgc_o, mcv_o, n_o, m_o, inter_o, dk_o, v_o,
               q_s, qc_s, col_s):
    i = pl.program_id(0)
    ns = p_ref.shape[0]

    def conv(col0, w_ref, b_ref, tail_ref, tail_o):
        x = p_ref[:, col0:col0 + 512]
        acc = b_ref[...] + w_ref[CONV_W - 1:CONV_W, :] * x
        for j in range(CONV_W - 1):
            acc = acc + w_ref[j:j + 1, :] * tail_ref[j]
        for j in range(CONV_W - 2):
            tail_o[j] = tail_ref[j + 1]
        tail_o[CONV_W - 2] = x
        return acc, x

    @pl.when(i == 0)
    def _rowwise():
        xc, _ = conv(0, rcw_ref, rcb_ref, rgc_ref, rgc_o)
        for g in range(N_GROUPS):
            sl = slice(g * LANES, (g + 1) * LANES)
            a, gx = _rg_gates(xc, wa_ref, wx_ref, rba_ref, rbx_ref, lam_ref, g)
            hn = a * rgh_ref[:, sl] + gx
            rgh_o[:, sl] = hn
            y_ref[0, :, sl] = (hn * jax.nn.gelu(p_ref[:, D_RG + g * LANES:D_RG + (g + 1) * LANES])).astype(BF16)

        mconv, mx = conv(2 * D_RG, mcw_ref, mcb_ref, mcv_ref, mcv_o)
        mcb = (mconv * _sigmoid(mconv)).astype(BF16)
        mxb = mx.astype(BF16)

        ig_r = gi_ref[...] + bi_ref[...]
        lf_r = jax.nn.log_sigmoid(gf_ref[...] + bf_ref[...])
        m0_r = mrow_ref[...]
        m_o[...] = jnp.maximum(lf_r + m0_r, ig_r)
        z = jnp.concatenate([gi_ref[...], gf_ref[...], jnp.zeros((ns - 2 * GATE_ROWS, ns), F32)], axis=0).T
        ig_c = z[:, 0:GATE_ROWS] + bic_ref[...]
        lf_c = jax.nn.log_sigmoid(z[:, GATE_ROWS:2 * GATE_ROWS] + bfc_ref[...])
        m0_c = mcol_ref[...]
        m_c = jnp.maximum(lf_c + m0_c, ig_c)
        inter_c = jnp.exp(lf_c + m0_c - m_c)
        dd_c = jnp.exp(ig_c - m_c)
        inter_o[...] = inter_c
        col_s[0] = inter_c
        col_s[1] = jnp.exp(-m_c)
        for h in range(ML_HEADS):
            sl = slice(h * ML_HD, (h + 1) * ML_HD)
            q = jnp.dot(mcb[:, sl], wq_ref[h], preferred_element_type=F32) * (ML_HD ** -0.5)
            k = jnp.dot(mcb[:, sl], wk_ref[h], preferred_element_type=F32)
            v = jnp.dot(mxb[:, sl], wv_ref[h], preferred_element_type=F32)
            q_s[h] = q
            dk_o[h] = dd_c[:, h:h + 1] * k
            v_o[h] = v
            col_s[2, :, h:h + 1] = jnp.sum(q * k, axis=-1, keepdims=True) * dd_c[:, h:h + 1]
            n0 = n_ref[h]
            col_s[3, :, h:h + 1] = jnp.sum(q * n0, axis=-1, keepdims=True)
            n_o[h] = inter_c[:, h:h + 1] * n0 + dd_c[:, h:h + 1] * k

    blk = pl.ds(pl.multiple_of(i * SBLK, SBLK), SBLK)
    for h in range(ML_HEADS):
        z = jnp.concatenate([q_s[h, blk, :], jnp.zeros((ML_HD - SBLK, ML_HD), F32)], axis=0).T
        qc_s[h, blk, :] = jnp.concatenate(
            [jnp.sum(z[:, j:j + 1] * c_ref[0, j, h], axis=0, keepdims=True) for j in range(SBLK)], axis=0)

    @pl.when(i == pl.num_programs(0) - 1)
    def _finish():
        for h in range(ML_HEADS):
            sl = slice(h * ML_HD, (h + 1) * ML_HD)
            inter = col_s[0, :, h:h + 1]
            s = col_s[2, :, h:h + 1]
            num = s * v_o[h] + inter * qc_s[h]
            den = s + inter * col_s[3, :, h:h + 1]
            hout = num / jnp.maximum(jnp.abs(den), col_s[1, :, h:h + 1])
            hm = _sigmoid(p_ref[:, 3 * D_RG + h * ML_HD:3 * D_RG + (h + 1) * ML_HD]) * hout
            hm = hm * lax.rsqrt(jnp.mean(hm * hm, axis=-1, keepdims=True) + EPS)
            y_ref[0, :, D_RG + h * ML_HD:D_RG + (h + 1) * ML_HD] = (hm * mng_ref[:, sl]).astype(BF16)


def _smix(p, gi, gf, st, mc_all, l, lw, *, name):
    ns = p.shape[0]
    rgh, rgc, mcv, mn, mcol, mrow = st
    in_arrays = [p, gi, gf, rgh, rgc, mcv, mc_all, mn, mcol, mrow,
                 lw["rcw"], lw["rcb"], lw["wa"], lw["wx"], lw["rba"], lw["rbx"], lw["lam"],
                 lw["mcw"], lw["mcb"], lw["wq"], lw["wk"], lw["wv"], lw["bi_s"], lw["bf_s"],
                 lw["bi_c"], lw["bf_c"], lw["mng"]]
    in_specs = [_const_spec(a.shape) for a in in_arrays]
    in_specs[6] = pl.BlockSpec((1, SBLK, ML_HEADS, ML_HD, ML_HD), lambda i: (l, i, 0, 0, 0))
    out_shape = [
        jax.ShapeDtypeStruct((1, ns, D_MODEL), BF16),
        jax.ShapeDtypeStruct((ns, D_RG), F32),
        jax.ShapeDtypeStruct((CONV_W - 1, ns, D_RG), F32),
        jax.ShapeDtypeStruct((CONV_W - 1, ns, D_ML), F32),
        jax.ShapeDtypeStruct((ML_HEADS, ns, ML_HD), F32),
        jax.ShapeDtypeStruct((GATE_ROWS, ns), F32),
        jax.ShapeDtypeStruct((ns, GATE_ROWS), F32),
        jax.ShapeDtypeStruct((ML_HEADS, ns, ML_HD), F32),
        jax.ShapeDtypeStruct((ML_HEADS, ns, ML_HD), F32),
    ]
    out_specs = [pl.BlockSpec(s.shape, lambda i, n=len(s.shape): (0,) * n) for s in out_shape]
    scratch = [
        pltpu.VMEM((ML_HEADS, ns, ML_HD), F32),
        pltpu.VMEM((ML_HEADS, ns, ML_HD), F32),
        pltpu.VMEM((4, ns, GATE_ROWS), F32),
    ]
    return pl.pallas_call(
        _smix_body,
        grid=(ns // SBLK,),
        in_specs=in_specs,
        out_specs=out_specs,
        out_shape=out_shape,
        scratch_shapes=scratch,
        compiler_params=pltpu.CompilerParams(dimension_semantics=("arbitrary",), vmem_limit_bytes=VMEM_LIMIT),
        name=name,
    )(*in_arrays)


def _cupdate_body(c_ref, inter_ref, dk_ref, v_ref, o_ref):
    for h in range(ML_HEADS):
        z = jnp.concatenate([dk_ref[0, h], jnp.zeros((ML_HD - SBLK, ML_HD), F32)], axis=0).T
        inter_rows = jnp.broadcast_to(inter_ref[0, :, h:h + 1], (SBLK, LANES))
        vblk = v_ref[0, h]
        for j in range(SBLK):
            o_ref[0, j, h] = inter_rows[j:j + 1, :] * c_ref[0, j, h] + z[:, j:j + 1] * vblk[j:j + 1, :]


def _cupdate(mc_all, inter_all, dk_all, v_all):
    depth, ns = mc_all.shape[:2]
    cspec = pl.BlockSpec((1, SBLK, ML_HEADS, ML_HD, ML_HD), lambda l, i: (l, i, 0, 0, 0))
    kvspec = pl.BlockSpec((1, ML_HEADS, SBLK, ML_HD), lambda l, i: (l, 0, i, 0))
    return pl.pallas_call(
        _cupdate_body,
        grid=(depth, ns // SBLK),
        in_specs=[cspec, pl.BlockSpec((1, SBLK, GATE_ROWS), lambda l, i: (l, i, 0)), kvspec, kvspec],
        out_specs=cspec,
        out_shape=jax.ShapeDtypeStruct(mc_all.shape, F32),
        compiler_params=pltpu.CompilerParams(dimension_semantics=("arbitrary", "arbitrary"),
                                             vmem_limit_bytes=VMEM_LIMIT),
        name="cupdate_s",
    )(mc_all, inter_all, dk_all, v_all)


def _block_diag_pairs(w):
    w = w.reshape(N_GROUPS, 2, RG_BLOCK, RG_BLOCK)
    z = jnp.zeros((N_GROUPS, 2, RG_BLOCK, 2, RG_BLOCK), w.dtype)
    z = z.at[:, 0, :, 0, :].set(w[:, 0]).at[:, 1, :, 1, :].set(w[:, 1])
    return z.reshape(N_GROUPS, LANES, LANES)


def _gate_rows(w_cols):
    return jnp.concatenate([w_cols.T, jnp.zeros((GATE_ROWS - ML_HEADS, D_MODEL), w_cols.dtype)], axis=0)


def _layer_weights(l, ln1_g, w_in, rg_conv_w, rg_conv_b, rg_w_a, rg_w_x, rg_b_a, rg_b_x, rg_lambda,
                   m_conv_w, m_conv_b, m_w_q, m_w_k, m_w_v, m_b_i, m_b_f, m_norm_g, w_out, ln2_g,
                   w_ff1, w_ff2, nb_prompt, ns):
    row = lambda a: a[l].reshape(1, -1)
    bias8 = jnp.concatenate([m_b_i[l], jnp.zeros((GATE_ROWS - ML_HEADS,), F32)])
    bfor8 = jnp.concatenate([m_b_f[l], jnp.zeros((GATE_ROWS - ML_HEADS,), F32)])
    return dict(
        ln1=row(ln1_g),
        w_main=w_in[l, :, :D_MAIN].astype(BF16),
        wgi=_gate_rows(w_in[l, :, D_MAIN:D_MAIN + ML_HEADS]).astype(BF16),
        wgf=_gate_rows(w_in[l, :, D_MAIN + ML_HEADS:]).astype(BF16),
        rcw=rg_conv_w[l], rcb=row(rg_conv_b),
        wa=_block_diag_pairs(rg_w_a[l]).astype(BF16), wx=_block_diag_pairs(rg_w_x[l]).astype(BF16),
        rba=row(rg_b_a), rbx=row(rg_b_x), lam=row(rg_lambda),
        mcw=m_conv_w[l], mcb=row(m_conv_b),
        wq=m_w_q[l].astype(BF16), wk=m_w_k[l].astype(BF16), wkt=jnp.swapaxes(m_w_k[l], 1, 2).astype(BF16),
        wv=m_w_v[l].astype(BF16),
        bi=jnp.broadcast_to(jnp.tile(bias8, nb_prompt)[:, None], (nb_prompt * GATE_ROWS, CHUNK)),
        bf=jnp.broadcast_to(jnp.tile(bfor8, nb_prompt)[:, None], (nb_prompt * GATE_ROWS, CHUNK)),
        bi_s=jnp.broadcast_to(bias8[:, None], (GATE_ROWS, ns)),
        bf_s=jnp.broadcast_to(bfor8[:, None], (GATE_ROWS, ns)),
        bi_c=bias8.reshape(1, GATE_ROWS), bf_c=bfor8.reshape(1, GATE_ROWS),
        mng=row(m_norm_g),
        w_out=w_out[l].astype(BF16), ln2=row(ln2_g),
        w_ff1=w_ff1[l].astype(BF16), w_ff2=w_ff2[l].astype(BF16),
    )


def kernel(x_prompt, x_sample, state_rg_h, state_rg_conv, state_m_conv, state_m_C, state_m_n, state_m_m,
           meta_tokens, ln1_g, w_in, rg_conv_w, rg_conv_b, rg_w_a, rg_w_x, rg_b_a, rg_b_x, rg_lambda,
           m_conv_w, m_conv_b, m_w_q, m_w_k, m_w_v, m_b_i, m_b_f, m_norm_g, w_out, ln2_g,
           w_ff1, w_ff2, ln_f_g):
    nb, seq, d = x_prompt.shape
    ns = x_sample.shape[0]
    t_real = N_META + seq
    nc = -(-t_real // CHUNK)
    pad = nc * CHUNK - t_real
    assert (pad + N_META) % CHUNK == 0 and x_sample.shape[1] == 1 and ns == LANES

    lnf = ln_f_g.reshape(1, d)
    xp = jnp.concatenate([jnp.zeros((nb, pad, d), F32),
                          jnp.broadcast_to(meta_tokens.astype(F32)[None], (nb, N_META, d)),
                          x_prompt], axis=1)
    xs = x_sample.reshape(1, ns, d)

    p_states = [[] for _ in range(6)]
    s_states = [[] for _ in range(6)]
    for l in range(DEPTH):
        lw = _layer_weights(l, ln1_g, w_in, rg_conv_w, rg_conv_b, rg_w_a, rg_w_x, rg_b_a, rg_b_x, rg_lambda,
                            m_conv_w, m_conv_b, m_w_q, m_w_k, m_w_v, m_b_i, m_b_f, m_norm_g, w_out, ln2_g,
                            w_ff1, w_ff2, nb, ns)
        last = l == DEPTH - 1

        p, gi, gf = _inproj(xp, lw["ln1"], lw["w_main"], lw["wgi"], lw["wgf"], lt=CHUNK, name=f"inproj_p{l}")
        y, rgh, rgc, mcv, st, mrow = _pmix(p, gi, gf, lw, nb=nb, nc=nc, pad=pad, name=f"mixer_p{l}")
        lt_out = CHUNK // 2
        skip = (pad + N_META) // lt_out if last else 0
        xp = _outmlp(xp, y, lw["w_out"], lw["ln2"], lw["w_ff1"], lw["w_ff2"], lnf,
                     lt=lt_out, final=last, skip=skip, nzero=0 if last else pad // lt_out, name=f"outmlp_p{l}")
        st = st.reshape(nb, ML_HEADS, ML_HD, S_COLS)
        p_states[0].append(rgh)
        p_states[1].append(rgc[:, SUBLANES - (CONV_W - 1):, :])
        p_states[2].append(mcv[:, SUBLANES - (CONV_W - 1):, :])
        p_states[3].append(st[..., :ML_HD])
        p_states[4].append(st[..., ML_HD])
        p_states[5].append(mrow.reshape(nb, GATE_ROWS, CHUNK)[:, :ML_HEADS, 0])

        ps, gis, gfs = _inproj(xs, lw["ln1"], lw["w_main"], lw["wgi"], lw["wgf"], lt=ns, name=f"inproj_s{l}")
        m_cols = jnp.concatenate([state_m_m[l], jnp.zeros((ns, GATE_ROWS - ML_HEADS), F32)], axis=1)
        st_in = (state_rg_h[l], jnp.swapaxes(state_rg_conv[l], 0, 1), jnp.swapaxes(state_m_conv[l], 0, 1),
                 jnp.swapaxes(state_m_n[l], 0, 1), m_cols, m_cols.T)
        ys, srgh, srgc, smcv, smn, smrow, sinter, sdk, sv = _smix(ps, gis, gfs, st_in, state_m_C, l, lw,
                                                                  name=f"mixer_s{l}")
        xs = _outmlp(xs, ys, lw["w_out"], lw["ln2"], lw["w_ff1"], lw["w_ff2"], lnf,
                     lt=ns, final=last, skip=0, nzero=0, name=f"outmlp_s{l}")
        s_states[0].append(srgh)
        s_states[1].append(jnp.swapaxes(srgc, 0, 1))
        s_states[2].append(jnp.swapaxes(smcv, 0, 1))
        s_states[3].append((sinter, sdk, sv))
        s_states[4].append(jnp.swapaxes(smn, 0, 1))
        s_states[5].append(smrow[:ML_HEADS].T)

    y_prompt = xp
    y_sample = xs.reshape(ns, 1, d)
    ps_out = [jnp.stack(s) for s in p_states]
    c_new = _cupdate(state_m_C, *(jnp.stack([t[j] for t in s_states[3]]) for j in range(3)))
    ss_out = [c_new if j == 3 else jnp.stack(s) for j, s in enumerate(s_states)]
    return (y_prompt, y_sample, *ps_out, *ss_out)
```

```python
import functools

import jax
import jax.numpy as jnp
from jax import lax
from jax.experimental import pallas as pl
from jax.experimental.pallas import tpu as pltpu

F32 = jnp.float32
BF16 = jnp.bfloat16

D_MODEL = 1024
N_META = 16
D_RG = 512
RG_BLOCK = 64
RG_C = 8.0
D_ML = 512
ML_HEADS = 4
ML_HD = 128
CONV_W = 4
D_FF = 4096
EPS = 1e-6
DEPTH = 4

LANES = 128
SUBLANES = 8
CHUNK = 128
PITCH = CHUNK + SUBLANES
N_GROUPS = D_RG // LANES
D_MAIN = 2 * D_RG + 2 * D_ML
GATE_ROWS = SUBLANES
S_COLS = 2 * ML_HD
FF_TILE = 1024
VMEM_LIMIT = 56 * 1024 * 1024

_NT = (((1,), (1,)), ((), ()))


def _const_spec(shape):
    zeros = (0,) * len(shape)
    return pl.BlockSpec(shape, lambda *_: zeros, pipeline_mode=pl.Buffered(1))


def _layer_spec(arr, l):
    tail = (0,) * (arr.ndim - 1)
    return pl.BlockSpec((None,) + arr.shape[1:], lambda *_: (l,) + tail, pipeline_mode=pl.Buffered(1))


def _rms(x, g):
    return x * lax.rsqrt(jnp.mean(x * x, axis=-1, keepdims=True) + EPS) * g


def _sigmoid(x):
    return 0.5 * jnp.tanh(0.5 * x) + 0.5


def _sqrt_nonneg(s):
    return jnp.where(s > 0.0, s * lax.rsqrt(s), 0.0)


def _inproj_body(x_ref, g_ref, w_ref, wgi_ref, wgf_ref, p_ref, gi_ref, gf_ref):
    nb, lt, d = x_ref.shape
    x = x_ref[...].reshape(nb * lt, d)
    u = _rms(x, g_ref[...]).astype(BF16)
    for n in range(D_MAIN // 512):
        cols = slice(n * 512, (n + 1) * 512)
        p_ref[:, cols] = jnp.dot(u, w_ref[:, cols], preferred_element_type=F32)
    for b in range(nb):
        ub = u[b * lt:(b + 1) * lt]
        rows = slice(b * GATE_ROWS, (b + 1) * GATE_ROWS)
        gi_ref[rows, :] = lax.dot_general(wgi_ref[...], ub, _NT, preferred_element_type=F32)
        gf_ref[rows, :] = lax.dot_general(wgf_ref[...], ub, _NT, preferred_element_type=F32)


def _inproj(x, lw, l, *, lt, name):
    nb, t, d = x.shape
    nc = t // lt
    rows = nb * lt
    return pl.pallas_call(
        _inproj_body,
        grid=(nc,),
        in_specs=[
            pl.BlockSpec((nb, lt, d), lambda c: (0, c, 0)),
            _layer_spec(lw["ln1"], l),
            _layer_spec(lw["w_main"], l),
            _layer_spec(lw["wgi"], l),
            _layer_spec(lw["wgf"], l),
        ],
        out_specs=[
            pl.BlockSpec((rows, D_MAIN), lambda c: (c, 0)),
            pl.BlockSpec((nb * GATE_ROWS, lt), lambda c: (c, 0)),
            pl.BlockSpec((nb * GATE_ROWS, lt), lambda c: (c, 0)),
        ],
        out_shape=[
            jax.ShapeDtypeStruct((nc * rows, D_MAIN), F32),
            jax.ShapeDtypeStruct((nc * nb * GATE_ROWS, lt), F32),
            jax.ShapeDtypeStruct((nc * nb * GATE_ROWS, lt), F32),
        ],
        compiler_params=pltpu.CompilerParams(dimension_semantics=("arbitrary",), vmem_limit_bytes=VMEM_LIMIT),
        name=name,
    )(x, lw["ln1"], lw["w_main"], lw["wgi"], lw["wgf"])


def _outmlp_body(x_ref, y_ref, wo_ref, g2_ref, w1_ref, w2_ref, gf_ref, o_ref, *, final, nzero):
    nb, lt, d = x_ref.shape
    rows = nb * lt

    def compute():
        x = x_ref[...].reshape(rows, d)
        y = y_ref[...].reshape(rows, d)
        x1 = x + jnp.dot(y, wo_ref[...], preferred_element_type=F32)
        u2 = _rms(x1, g2_ref[...]).astype(BF16)
        acc = x1
        for c in range(D_FF // FF_TILE):
            cols = slice(c * FF_TILE, (c + 1) * FF_TILE)
            h = jnp.dot(u2, w1_ref[:, cols], preferred_element_type=F32)
            h = jnp.square(jnp.maximum(h, 0.0)).astype(BF16)
            acc = acc + jnp.dot(h, w2_ref[cols, :], preferred_element_type=F32)
        if final:
            acc = _rms(acc, gf_ref[...])
        o_ref[...] = acc.reshape(nb, lt, d)

    def zero():
        o_ref[...] = jnp.zeros(o_ref.shape, F32)

    if nzero:
        i = pl.program_id(0)
        pl.when(i < nzero)(zero)
        pl.when(i >= nzero)(compute)
    else:
        compute()


def _outmlp(x, y, lw, l, gf, *, lt, final, skip, nzero, name):
    nb, t, d = x.shape
    nsteps = t // lt - skip
    return pl.pallas_call(
        functools.partial(_outmlp_body, final=final, nzero=nzero),
        grid=(nsteps,),
        in_specs=[
            pl.BlockSpec((nb, lt, d), lambda i: (0, i + skip, 0)),
            pl.BlockSpec((nb, lt, d), lambda i: (0, i + skip, 0)),
            _layer_spec(lw["w_out"], l),
            _layer_spec(lw["ln2"], l),
            _layer_spec(lw["w_ff1"], l),
            _layer_spec(lw["w_ff2"], l),
            _const_spec((1, d)),
        ],
        out_specs=pl.BlockSpec((nb, lt, d), lambda i: (0, i, 0)),
        out_shape=jax.ShapeDtypeStruct((nb, nsteps * lt, d), F32),
        compiler_params=pltpu.CompilerParams(dimension_semantics=("arbitrary",), vmem_limit_bytes=VMEM_LIMIT),
        name=name,
    )(x, y, lw["w_out"], lw["ln2"], lw["w_ff1"], lw["w_ff2"], gf)


def _rg_gates(xc, wa_ref, wx_ref, rba_ref, rbx_ref, lam_ref, g):
    sl = slice(g * LANES, (g + 1) * LANES)
    xg = xc[:, sl]
    xb = xg.astype(BF16)
    r = _sigmoid(jnp.dot(xb, wa_ref[g], preferred_element_type=F32) + rba_ref[:, sl])
    i = _sigmoid(jnp.dot(xb, wx_ref[g], preferred_element_type=F32) + rbx_ref[:, sl])
    log_a = (RG_C * r) * jax.nn.log_sigmoid(lam_ref[:, sl])
    a = jnp.exp(log_a)
    gx = _sqrt_nonneg(1.0 - a * a) * (i * xg)
    return a, gx


def _lane_scan(x, op, fill):
    lane = lax.broadcasted_iota(jnp.int32, x.shape, 1)
    sh = 1
    while sh < x.shape[1]:
        x = op(x, jnp.where(lane >= sh, pltpu.roll(x, sh, 1), fill))
        sh *= 2
    return x


def _pmix_body(p_ref, gi_ref, gf_ref, rcw_ref, rcb_ref, wa_ref, wx_ref, rba_ref, rbx_ref, lam_ref,
               mcw_ref, mcb_ref, wq_ref, wkt_ref, wv_ref, bi_ref, bf_ref, mng_ref,
               y_ref, rgh_ref, rgc_ref, mcv_ref, st_ref, m_ref,
               cbuf, a_s, g_s, q_s, kt_s, v_s, row_s, *, pad):
    nb = y_ref.shape[0]
    lt = CHUNK
    rows = nb * lt
    c = pl.program_id(0)

    @pl.when(c == 0)
    def _init():
        rgh_ref[...] = jnp.zeros_like(rgh_ref)
        rgc_ref[...] = jnp.zeros_like(rgc_ref)
        mcv_ref[...] = jnp.zeros_like(mcv_ref)
        st_ref[...] = jnp.zeros_like(st_ref)
        m_ref[...] = jnp.zeros_like(m_ref)
        ones_col = (lax.broadcasted_iota(jnp.int32, (rows, ML_HD), 1) == 0).astype(BF16)
        for h in range(ML_HEADS):
            v_s[h, :, ML_HD:S_COLS] = ones_col

    def conv(col0, w_ref, b_ref, tail_ref):
        x = p_ref[:, col0:col0 + 512].reshape(nb, lt, 512)
        cbuf[:, 0:SUBLANES, :] = tail_ref[...]
        cbuf[:, SUBLANES:SUBLANES + lt, :] = x
        tail_ref[...] = x[:, lt - SUBLANES:lt, :]
        acc = b_ref[...].reshape(1, 1, 512) + w_ref[CONV_W - 1:CONV_W, :].reshape(1, 1, 512) * x
        for k in range(1, CONV_W):
            wk = w_ref[CONV_W - 1 - k:CONV_W - k, :].reshape(1, 1, 512)
            acc = acc + wk * cbuf[:, SUBLANES - k:SUBLANES - k + lt, :]
        return acc.reshape(rows, 512), x.reshape(rows, 512)

    mconv, mx = conv(2 * D_RG, mcw_ref, mcb_ref, mcv_ref)
    mcb = (mconv * _sigmoid(mconv)).astype(BF16)
    mxb = mx.astype(BF16)
    for h in range(ML_HEADS):
        sl = slice(h * ML_HD, (h + 1) * ML_HD)
        q_s[h] = jnp.dot(mcb[:, sl], wq_ref[h], preferred_element_type=F32) * (ML_HD ** -0.5)
        kt = lax.dot_general(wkt_ref[h], mcb[:, sl], _NT, preferred_element_type=F32)
        for b in range(nb):
            kt_s[h, b] = kt[:, b * lt:(b + 1) * lt]
        v_s[h, :, 0:ML_HD] = jnp.dot(mxb[:, sl], wv_ref[h], preferred_element_type=F32).astype(BF16)

    ig = gi_ref[...] + bi_ref[...]
    lf = jax.nn.log_sigmoid(gf_ref[...] + bf_ref[...])
    if pad:
        lane = lax.broadcasted_iota(jnp.int32, ig.shape, 1)
        is_pad = lane < jnp.where(c == 0, pad, 0)
        ig = jnp.where(is_pad, -1e30, ig)
        lf = jnp.where(is_pad, 0.0, lf)
    bcs = _lane_scan(lf, jnp.add, 0.0)
    gg = ig - bcs
    cm = _lane_scan(gg, jnp.maximum, -jnp.inf)
    m0 = m_ref[...]
    mm = jnp.maximum(m0, cm)
    mt = bcs + mm
    b_last = bcs[:, lt - 1:lt]
    m_last = mt[:, lt - 1:lt]
    row_s[0] = gg
    row_s[1] = mm
    row_s[2] = jnp.exp(m0 - mm)
    row_s[3] = jnp.exp(-mt)
    row_s[4] = jnp.exp(b_last - m_last + gg)
    row_s[5] = jnp.broadcast_to(jnp.exp(b_last + m0[:, 0:1] - m_last), m0.shape)
    m_ref[...] = jnp.broadcast_to(m_last, m0.shape)

    rg = {}

    def rg_conv():
        rg["xc"] = conv(0, rcw_ref, rcb_ref, rgc_ref)[0]
        if pad:
            t_col = lax.broadcasted_iota(jnp.int32, (nb, lt, 1), 1).reshape(rows, 1)
            rg["keep"] = t_col >= jnp.where(c == 0, pad, 0)

    def rg_gates(g):
        a, gx = _rg_gates(rg["xc"], wa_ref, wx_ref, rba_ref, rbx_ref, lam_ref, g)
        if pad:
            gx = jnp.where(rg["keep"], gx, 0.0)
        for b in range(nb):
            a_s[g, b * PITCH:b * PITCH + lt, :] = a[b * lt:(b + 1) * lt]
            g_s[g, b * PITCH:b * PITCH + lt, :] = gx[b * lt:(b + 1) * lt]

    def rg_scan(g):
        sl = slice(g * LANES, (g + 1) * LANES)
        h = rgh_ref[:, sl]
        for t in range(lt):
            idx = pl.ds(t, nb, stride=PITCH)
            h = a_s[g, idx, :] * h + g_s[g, idx, :]
            g_s[g, idx, :] = h
        rgh_ref[:, sl] = h

    def rg_out(g):
        for b in range(nb):
            gate = jax.nn.gelu(p_ref[b * lt:(b + 1) * lt, D_RG + g * LANES:D_RG + (g + 1) * LANES])
            y_ref[b, :, g * LANES:(g + 1) * LANES] = (g_s[g, b * PITCH:b * PITCH + lt, :] * gate).astype(BF16)

    rg_pieces = [[] for _ in range(nb)]
    rg_pieces[0].append(rg_conv)
    for g in range(N_GROUPS):
        rg_pieces[g].append(functools.partial(rg_gates, g))
        rg_pieces[g + 1].append(functools.partial(rg_scan, g))
        rg_pieces[g + 2].append(functools.partial(rg_out, g))

    t_i = lax.broadcasted_iota(jnp.int32, (lt, lt), 0)
    s_i = lax.broadcasted_iota(jnp.int32, (lt, lt), 1)
    causal = s_i <= t_i

    for b in range(nb):
        for piece in rg_pieces[b]:
            piece()
        trows = slice(b * lt, (b + 1) * lt)
        gsl = slice(b * GATE_ROWS, (b + 1) * GATE_ROWS)
        gg_b = row_s[0, gsl, :]
        w_b = row_s[4, gsl, :]
        sc_b = row_s[5, gsl, :]
        z = jnp.concatenate([row_s[1, gsl, :], row_s[2, gsl, :], row_s[3, gsl, :],
                             jnp.zeros((lt - 3 * GATE_ROWS, lt), F32)], axis=0)
        zt = z.T
        for h in range(ML_HEADS):
            sl = slice(h * ML_HD, (h + 1) * ML_HD)
            qb = q_s[h, trows, :].astype(BF16)
            kt = kt_s[h, b]
            vext = v_s[h, trows, :]
            sidx = b * ML_HEADS + h
            s0 = st_ref[sidx]
            mm_c = zt[:, h:h + 1]
            inter_c = zt[:, GATE_ROWS + h:GATE_ROWS + h + 1]
            emm_c = zt[:, 2 * GATE_ROWS + h:2 * GATE_ROWS + h + 1]
            sc = jnp.dot(qb, kt.astype(BF16), preferred_element_type=F32)
            d = jnp.exp(jnp.where(causal, gg_b[h:h + 1, :] - mm_c, -jnp.inf))
            pm = (sc * d).astype(BF16)
            numx = (jnp.dot(pm, vext, preferred_element_type=F32)
                    + inter_c * jnp.dot(qb, s0.astype(BF16), preferred_element_type=F32))
            den = numx[:, ML_HD:ML_HD + 1]
            hout = numx[:, 0:ML_HD] / jnp.maximum(jnp.abs(den), emm_c)
            hm = _sigmoid(p_ref[trows, 3 * D_RG + h * ML_HD:3 * D_RG + (h + 1) * ML_HD]) * hout
            hm = hm * lax.rsqrt(jnp.mean(hm * hm, axis=-1, keepdims=True) + EPS)
            y_ref[b, :, D_RG + h * ML_HD:D_RG + (h + 1) * ML_HD] = (hm * mng_ref[:, sl]).astype(BF16)
            wkt = (kt * w_b[h:h + 1, :]).astype(BF16)
            sc_row = jnp.concatenate([sc_b[h:h + 1, :], sc_b[h:h + 1, :]], axis=1)
            st_ref[sidx] = sc_row * s0 + jnp.dot(wkt, vext, preferred_element_type=F32)

    if pad:
        @pl.when(c == 0)
        def _zero_pad_rows():
            y_ref[:, 0:pad, :] = jnp.zeros((nb, pad, y_ref.shape[2]), BF16)


def _pmix(p, gi, gf, lw, l, *, nb, nc, pad, name):
    rows = nb * CHUNK
    ng = nb * GATE_ROWS
    in_arrays = [p, gi, gf, lw["rcw"], lw["rcb"], lw["wa"], lw["wx"], lw["rba"], lw["rbx"], lw["lam"],
                 lw["mcw"], lw["mcb"], lw["wq"], lw["wkt"], lw["wv"], lw["bi"], lw["bf"], lw["mng"]]
    in_specs = [
        pl.BlockSpec((rows, D_MAIN), lambda c: (c, 0)),
        pl.BlockSpec((ng, CHUNK), lambda c: (c, 0)),
        pl.BlockSpec((ng, CHUNK), lambda c: (c, 0)),
    ] + [_layer_spec(a, l) for a in in_arrays[3:]]
    out_shape = [
        jax.ShapeDtypeStruct((nb, nc * CHUNK, D_MODEL), BF16),
        jax.ShapeDtypeStruct((nb, D_RG), F32),
        jax.ShapeDtypeStruct((nb, SUBLANES, D_RG), F32),
        jax.ShapeDtypeStruct((nb, SUBLANES, D_ML), F32),
        jax.ShapeDtypeStruct((nb * ML_HEADS, ML_HD, S_COLS), F32),
        jax.ShapeDtypeStruct((ng, CHUNK), F32),
    ]
    out_specs = [pl.BlockSpec((nb, CHUNK, D_MODEL), lambda c: (0, c, 0))] + [
        pl.BlockSpec(s.shape, lambda c, n=len(s.shape): (0,) * n) for s in out_shape[1:]]
    scratch = [
        pltpu.VMEM((nb, PITCH, 512), F32),
        pltpu.VMEM((N_GROUPS, nb * PITCH, LANES), F32),
        pltpu.VMEM((N_GROUPS, nb * PITCH, LANES), F32),
        pltpu.VMEM((ML_HEADS, rows, ML_HD), F32),
        pltpu.VMEM((ML_HEADS, nb, ML_HD, CHUNK), F32),
        pltpu.VMEM((ML_HEADS, rows, S_COLS), BF16),
        pltpu.VMEM((6, ng, CHUNK), F32),
    ]
    return pl.pallas_call(
        functools.partial(_pmix_body, pad=pad),
        grid=(nc,),
        in_specs=in_specs,
        out_specs=out_specs,
        out_shape=out_shape,
        scratch_shapes=scratch,
        compiler_params=pltpu.CompilerParams(dimension_semantics=("arbitrary",), vmem_limit_bytes=VMEM_LIMIT),
        name=name,
    )(*in_arrays)


SBLK = SUBLANES


def _smix_body(p_ref, gi_ref, gf_ref, rgh_ref, rgc_ref, mcv_ref, c_ref, n_ref, mcol_ref, mrow_ref,
               rcw_ref, rcb_ref, wa_ref, wx_ref, rba_ref, rbx_ref, lam_ref,
               mcw_ref, mcb_ref, wq_ref, wk_ref, wv_ref, bi_ref, bf_ref, bic_ref, bfc_ref, mng_ref,
               y_ref, rgh_o, rgc_o, mcv_o, n_o, m_o, inter_o, dk_o, v_o,
               q_s, qc_s, col_s):
    i = pl.program_id(0)
    ns = p_ref.shape[0]

    def conv(col0, w_ref, b_ref, tail_ref, tail_o):
        x = p_ref[:, col0:col0 + 512]
        acc = b_ref[...] + w_ref[CONV_W - 1:CONV_W, :] * x
        for j in range(CONV_W - 1):
            acc = acc + w_ref[j:j + 1, :] * tail_ref[j]
        for j in range(CONV_W - 2):
            tail_o[j] = tail_ref[j + 1]
        tail_o[CONV_W - 2] = x
        return acc, x

    @pl.when(i == 0)
    def _rowwise():
        xc, _ = conv(0, rcw_ref, rcb_ref, rgc_ref, rgc_o)
        for g in range(N_GROUPS):
            sl = slice(g * LANES, (g + 1) * LANES)
            a, gx = _rg_gates(xc, wa_ref, wx_ref, rba_ref, rbx_ref, lam_ref, g)
            hn = a * rgh_ref[:, sl] + gx
            rgh_o[:, sl] = hn
            y_ref[0, :, sl] = (hn * jax.nn.gelu(p_ref[:, D_RG + g * LANES:D_RG + (g + 1) * LANES])).astype(BF16)

        mconv, mx = conv(2 * D_RG, mcw_ref, mcb_ref, mcv_ref, mcv_o)
        mcb = (mconv * _sigmoid(mconv)).astype(BF16)
        mxb = mx.astype(BF16)

        ig_r = gi_ref[...] + bi_ref[...]
        lf_r = jax.nn.log_sigmoid(gf_ref[...] + bf_ref[...])
        m0_r = mrow_ref[...]
        m_o[...] = jnp.maximum(lf_r + m0_r, ig_r)
        z = jnp.concatenate([gi_ref[...], gf_ref[...], jnp.zeros((ns - 2 * GATE_ROWS, ns), F32)], axis=0).T
        ig_c = z[:, 0:GATE_ROWS] + bic_ref[...]
        lf_c = jax.nn.log_sigmoid(z[:, GATE_ROWS:2 * GATE_ROWS] + bfc_ref[...])
        m0_c = mcol_ref[...]
        m_c = jnp.maximum(lf_c + m0_c, ig_c)
        inter_c = jnp.exp(lf_c + m0_c - m_c)
        dd_c = jnp.exp(ig_c - m_c)
        inter_o[...] = inter_c
        col_s[0] = inter_c
        col_s[1] = jnp.exp(-m_c)
        for h in range(ML_HEADS):
            sl = slice(h * ML_HD, (h + 1) * ML_HD)
            q = jnp.dot(mcb[:, sl], wq_ref[h], preferred_element_type=F32) * (ML_HD ** -0.5)
            k = jnp.dot(mcb[:, sl], wk_ref[h], preferred_element_type=F32)
            v = jnp.dot(mxb[:, sl], wv_ref[h], preferred_element_type=F32)
            q_s[h] = q
            dk_o[h] = dd_c[:, h:h + 1] * k
            v_o[h] = v
            col_s[2, :, h:h + 1] = jnp.sum(q * k, axis=-1, keepdims=True) * dd_c[:, h:h + 1]
            n0 = n_ref[h]
            col_s[3, :, h:h + 1] = jnp.sum(q * n0, axis=-1, keepdims=True)
            n_o[h] = inter_c[:, h:h + 1] * n0 + dd_c[:, h:h + 1] * k

    blk = pl.ds(pl.multiple_of(i * SBLK, SBLK), SBLK)
    for h in range(ML_HEADS):
        z = jnp.concatenate([q_s[h, blk, :], jnp.zeros((ML_HD - SBLK, ML_HD), F32)], axis=0).T
        qc_s[h, blk, :] = jnp.concatenate(
            [jnp.sum(z[:, j:j + 1] * c_ref[0, j, h], axis=0, keepdims=True) for j in range(SBLK)], axis=0)

    @pl.when(i == pl.num_programs(0) - 1)
    def _finish():
        for h in range(ML_HEADS):
            sl = slice(h * ML_HD, (h + 1) * ML_HD)
            inter = col_s[0, :, h:h + 1]
            s = col_s[2, :, h:h + 1]
            num = s * v_o[h] + inter * qc_s[h]
            den = s + inter * col_s[3, :, h:h + 1]
            hout = num / jnp.maximum(jnp.abs(den), col_s[1, :, h:h + 1])
            hm = _sigmoid(p_ref[:, 3 * D_RG + h * ML_HD:3 * D_RG + (h + 1) * ML_HD]) * hout
            hm = hm * lax.rsqrt(jnp.mean(hm * hm, axis=-1, keepdims=True) + EPS)
            y_ref[0, :, D_RG + h * ML_HD:D_RG + (h + 1) * ML_HD] = (hm * mng_ref[:, sl]).astype(BF16)


def _smix(p, gi, gf, st, lw, l, *, name):
    ns = p.shape[0]
    rgh, rgc, mcv, mc_all, mn, mcol, mrow = st
    in_arrays = [p, gi, gf, rgh, rgc, mcv, mc_all, mn, mcol, mrow,
                 lw["rcw"], lw["rcb"], lw["wa"], lw["wx"], lw["rba"], lw["rbx"], lw["lam"],
                 lw["mcw"], lw["mcb"], lw["wq"], lw["wk"], lw["wv"], lw["bi_s"], lw["bf_s"],
                 lw["bi_c"], lw["bf_c"], lw["mng"]]
    in_specs = [_const_spec(a.shape) for a in in_arrays[:3]] + [_layer_spec(a, l) for a in in_arrays[3:]]
    in_specs[6] = pl.BlockSpec((1, SBLK, ML_HEADS, ML_HD, ML_HD), lambda i: (l, i, 0, 0, 0))
    out_shape = [
        jax.ShapeDtypeStruct((1, ns, D_MODEL), BF16),
        jax.ShapeDtypeStruct((ns, D_RG), F32),
        jax.ShapeDtypeStruct((CONV_W - 1, ns, D_RG), F32),
        jax.ShapeDtypeStruct((CONV_W - 1, ns, D_ML), F32),
        jax.ShapeDtypeStruct((ML_HEADS, ns, ML_HD), F32),
        jax.ShapeDtypeStruct((GATE_ROWS, ns), F32),
        jax.ShapeDtypeStruct((ns, GATE_ROWS), F32),
        jax.ShapeDtypeStruct((ML_HEADS, ns, ML_HD), F32),
        jax.ShapeDtypeStruct((ML_HEADS, ns, ML_HD), F32),
    ]
    out_specs = [pl.BlockSpec(s.shape, lambda i, n=len(s.shape): (0,) * n) for s in out_shape]
    scratch = [
        pltpu.VMEM((ML_HEADS, ns, ML_HD), F32),
        pltpu.VMEM((ML_HEADS, ns, ML_HD), F32),
        pltpu.VMEM((4, ns, GATE_ROWS), F32),
    ]
    return pl.pallas_call(
        _smix_body,
        grid=(ns // SBLK,),
        in_specs=in_specs,
        out_specs=out_specs,
        out_shape=out_shape,
        scratch_shapes=scratch,
        compiler_params=pltpu.CompilerParams(dimension_semantics=("arbitrary",), vmem_limit_bytes=VMEM_LIMIT),
        name=name,
    )(*in_arrays)


CBLK = 2 * SUBLANES


def _cupdate_body(c_ref, inter_ref, dk_ref, v_ref, o_ref):
    for h in range(ML_HEADS):
        z = jnp.concatenate([dk_ref[0, h], jnp.zeros((ML_HD - CBLK, ML_HD), F32)], axis=0).T
        inter_rows = jnp.broadcast_to(inter_ref[0, :, h:h + 1], (CBLK, LANES))
        vblk = v_ref[0, h]
        for j in range(CBLK):
            o_ref[0, j, h] = inter_rows[j:j + 1, :] * c_ref[0, j, h] + z[:, j:j + 1] * vblk[j:j + 1, :]


def _cupdate(mc_all, inter_all, dk_all, v_all):
    depth, ns = mc_all.shape[:2]
    cspec = pl.BlockSpec((1, CBLK, ML_HEADS, ML_HD, ML_HD), lambda l, i: (l, i, 0, 0, 0))
    kvspec = pl.BlockSpec((1, ML_HEADS, CBLK, ML_HD), lambda l, i: (l, 0, i, 0))
    return pl.pallas_call(
        _cupdate_body,
        grid=(depth, ns // CBLK),
        in_specs=[cspec, pl.BlockSpec((1, CBLK, GATE_ROWS), lambda l, i: (l, i, 0)), kvspec, kvspec],
        out_specs=cspec,
        out_shape=jax.ShapeDtypeStruct(mc_all.shape, F32),
        compiler_params=pltpu.CompilerParams(dimension_semantics=("arbitrary", "arbitrary"),
                                             vmem_limit_bytes=VMEM_LIMIT),
        name="cupdate_s",
    )(mc_all, inter_all, dk_all, v_all)


def _block_diag_pairs(w):
    depth = w.shape[0]
    w = w.reshape(depth, N_GROUPS, 2, RG_BLOCK, RG_BLOCK)
    z = jnp.zeros((depth, N_GROUPS, 2, RG_BLOCK, 2, RG_BLOCK), w.dtype)
    z = z.at[:, :, 0, :, 0, :].set(w[:, :, 0]).at[:, :, 1, :, 1, :].set(w[:, :, 1])
    return z.reshape(depth, N_GROUPS, LANES, LANES)


def _gate_rows(w_cols):
    depth = w_cols.shape[0]
    return jnp.concatenate([jnp.swapaxes(w_cols, 1, 2),
                            jnp.zeros((depth, GATE_ROWS - ML_HEADS, D_MODEL), w_cols.dtype)], axis=1)


def _stacked_weights(ln1_g, w_in, rg_conv_w, rg_conv_b, rg_w_a, rg_w_x, rg_b_a, rg_b_x, rg_lambda,
                     m_conv_w, m_conv_b, m_w_q, m_w_k, m_w_v, m_b_i, m_b_f, m_norm_g, w_out, ln2_g,
                     w_ff1, w_ff2, nb_prompt, ns):
    depth = ln1_g.shape[0]
    row = lambda a: a.reshape(depth, 1, -1)
    pad8 = lambda a: jnp.concatenate([a, jnp.zeros((depth, GATE_ROWS - ML_HEADS), F32)], axis=1)
    bias8, bfor8 = pad8(m_b_i), pad8(m_b_f)
    prompt_rows = lambda a: jnp.broadcast_to(jnp.tile(a, (1, nb_prompt))[:, :, None],
                                             (depth, nb_prompt * GATE_ROWS, CHUNK))
    return dict(
        ln1=row(ln1_g),
        w_main=w_in[:, :, :D_MAIN].astype(BF16),
        wgi=_gate_rows(w_in[:, :, D_MAIN:D_MAIN + ML_HEADS]).astype(BF16),
        wgf=_gate_rows(w_in[:, :, D_MAIN + ML_HEADS:]).astype(BF16),
        rcw=rg_conv_w, rcb=row(rg_conv_b),
        wa=_block_diag_pairs(rg_w_a).astype(BF16), wx=_block_diag_pairs(rg_w_x).astype(BF16),
        rba=row(rg_b_a), rbx=row(rg_b_x), lam=row(rg_lambda),
        mcw=m_conv_w, mcb=row(m_conv_b),
        wq=m_w_q.astype(BF16), wk=m_w_k.astype(BF16), wkt=jnp.swapaxes(m_w_k, 2, 3).astype(BF16),
        wv=m_w_v.astype(BF16),
        bi=prompt_rows(bias8), bf=prompt_rows(bfor8),
        bi_s=jnp.broadcast_to(bias8[:, :, None], (depth, GATE_ROWS, ns)),
        bf_s=jnp.broadcast_to(bfor8[:, :, None], (depth, GATE_ROWS, ns)),
        bi_c=bias8.reshape(depth, 1, GATE_ROWS), bf_c=bfor8.reshape(depth, 1, GATE_ROWS),
        mng=row(m_norm_g),
        w_out=w_out.astype(BF16), ln2=row(ln2_g),
        w_ff1=w_ff1.astype(BF16), w_ff2=w_ff2.astype(BF16),
    )


def kernel(x_prompt, x_sample, state_rg_h, state_rg_conv, state_m_conv, state_m_C, state_m_n, state_m_m,
           meta_tokens, ln1_g, w_in, rg_conv_w, rg_conv_b, rg_w_a, rg_w_x, rg_b_a, rg_b_x, rg_lambda,
           m_conv_w, m_conv_b, m_w_q, m_w_k, m_w_v, m_b_i, m_b_f, m_norm_g, w_out, ln2_g,
           w_ff1, w_ff2, ln_f_g):
    nb, seq, d = x_prompt.shape
    ns = x_sample.shape[0]
    depth = ln1_g.shape[0]
    t_real = N_META + seq
    nc = -(-t_real // CHUNK)
    pad = nc * CHUNK - t_real
    assert (pad + N_META) % CHUNK == 0 and x_sample.shape[1] == 1 and ns == LANES and depth == DEPTH

    lw = _stacked_weights(ln1_g, w_in, rg_conv_w, rg_conv_b, rg_w_a, rg_w_x, rg_b_a, rg_b_x, rg_lambda,
                          m_conv_w, m_conv_b, m_w_q, m_w_k, m_w_v, m_b_i, m_b_f, m_norm_g, w_out, ln2_g,
                          w_ff1, w_ff2, nb, ns)
    lnf = ln_f_g.reshape(1, d)
    xp = jnp.concatenate([jnp.zeros((nb, pad, d), F32),
                          jnp.broadcast_to(meta_tokens.astype(F32)[None], (nb, N_META, d)),
                          x_prompt], axis=1)
    xs = x_sample.reshape(1, ns, d)
    m_cols = jnp.concatenate([state_m_m, jnp.zeros((depth, ns, GATE_ROWS - ML_HEADS), F32)], axis=2)
    st_in = (state_rg_h, jnp.swapaxes(state_rg_conv, 1, 2), jnp.swapaxes(state_m_conv, 1, 2), state_m_C,
             jnp.swapaxes(state_m_n, 1, 2), m_cols, jnp.swapaxes(m_cols, 1, 2))

    p_states = [[] for _ in range(6)]
    s_states = [[] for _ in range(6)]
    for l in range(depth):
        last = l == depth - 1

        p, gi, gf = _inproj(xp, lw, l, lt=CHUNK, name=f"inproj_p{l}")
        y, rgh, rgc, mcv, st, mrow = _pmix(p, gi, gf, lw, l, nb=nb, nc=nc, pad=pad, name=f"mixer_p{l}")
        lt_out = CHUNK // 2
        skip = (pad + N_META) // lt_out if last else 0
        xp = _outmlp(xp, y, lw, l, lnf, lt=lt_out, final=last, skip=skip,
                     nzero=0 if last else pad // lt_out, name=f"outmlp_p{l}")
        st = st.reshape(nb, ML_HEADS, ML_HD, S_COLS)
        p_states[0].append(rgh)
        p_states[1].append(rgc[:, SUBLANES - (CONV_W - 1):, :])
        p_states[2].append(mcv[:, SUBLANES - (CONV_W - 1):, :])
        p_states[3].append(st[..., :ML_HD])
        p_states[4].append(st[..., ML_HD])
        p_states[5].append(mrow.reshape(nb, GATE_ROWS, CHUNK)[:, :ML_HEADS, 0])

        ps, gis, gfs = _inproj(xs, lw, l, lt=ns, name=f"inproj_s{l}")
        ys, srgh, srgc, smcv, smn, smrow, sinter, sdk, sv = _smix(ps, gis, gfs, st_in, lw, l, name=f"mixer_s{l}")
        xs = _outmlp(xs, ys, lw, l, lnf, lt=ns, final=last, skip=0, nzero=0, name=f"outmlp_s{l}")
        s_states[0].append(srgh)
        s_states[1].append(jnp.swapaxes(srgc, 0, 1))
        s_states[2].append(jnp.swapaxes(smcv, 0, 1))
        s_states[3].append((sinter, sdk, sv))
        s_states[4].append(jnp.swapaxes(smn, 0, 1))
        s_states[5].append(smrow[:ML_HEADS].T)

    y_prompt = xp
    y_sample = xs.reshape(ns, 1, d)
    ps_out = [jnp.stack(s) for s in p_states]
    c_new = _cupdate(state_m_C, *(jnp.stack([t[j] for t in s_states[3]]) for j in range(3)))
    ss_out = [c_new if j == 3 else jnp.stack(s) for j, s in enumerate(s_states)]
    return (y_prompt, y_sample, *ps_out, *ss_out)
```

```python
import functools

import jax
import jax.numpy as jnp
from jax import lax
from jax.experimental import pallas as pl
from jax.experimental.pallas import tpu as pltpu

F32 = jnp.float32
BF16 = jnp.bfloat16

D_MODEL = 1024
N_META = 16
D_RG = 512
RG_BLOCK = 64
RG_C = 8.0
D_ML = 512
ML_HEADS = 4
ML_HD = 128
CONV_W = 4
D_FF = 4096
EPS = 1e-6
DEPTH = 4

LANES = 128
SUBLANES = 8
CHUNK = 128
PITCH = CHUNK + SUBLANES
N_GROUPS = D_RG // LANES
D_MAIN = 2 * D_RG + 2 * D_ML
GATE_ROWS = SUBLANES
S_COLS = 2 * ML_HD
N_SLABS = 3 * N_GROUPS
FF_TILE = 1024
VMEM_LIMIT = 56 * 1024 * 1024

_NT = (((1,), (1,)), ((), ()))


def _const_spec(shape):
    zeros = (0,) * len(shape)
    return pl.BlockSpec(shape, lambda *_: zeros, pipeline_mode=pl.Buffered(1))


def _layer_spec(arr, l):
    tail = (0,) * (arr.ndim - 1)
    return pl.BlockSpec((None,) + arr.shape[1:], lambda *_: (l,) + tail, pipeline_mode=pl.Buffered(1))


def _rms(x, g):
    return x * lax.rsqrt(jnp.mean(x * x, axis=-1, keepdims=True) + EPS) * g


def _sigmoid(x):
    return 0.5 * jnp.tanh(0.5 * x) + 0.5


def _sqrt_nonneg(s):
    return jnp.where(s > 0.0, s * lax.rsqrt(s), 0.0)


def _inproj_body(x_ref, g_ref, w_ref, wgi_ref, wgf_ref, ptm_ref, pbm_ref, gi_ref, gf_ref):
    nb, lt, d = x_ref.shape
    x = x_ref[...].reshape(nb * lt, d)
    u = _rms(x, g_ref[...]).astype(BF16)
    for n in range(D_MAIN // 512):
        pr = jnp.dot(u, w_ref[:, n * 512:(n + 1) * 512], preferred_element_type=F32)
        if n == D_MAIN // 512 - 1:
            pbm_ref[...] = pr
            continue
        for g in range(N_GROUPS):
            cols = slice(g * LANES, (g + 1) * LANES)
            if nb == 1:
                ptm_ref[n * N_GROUPS + g] = pr[:, cols]
                continue
            for b in range(nb):
                ptm_ref[n * N_GROUPS + g, pl.ds(b, lt, stride=nb), :] = pr[b * lt:(b + 1) * lt, cols]
    for b in range(nb):
        ub = u[b * lt:(b + 1) * lt]
        rows = slice(b * GATE_ROWS, (b + 1) * GATE_ROWS)
        gi_ref[rows, :] = lax.dot_general(wgi_ref[...], ub, _NT, preferred_element_type=F32)
        gf_ref[rows, :] = lax.dot_general(wgf_ref[...], ub, _NT, preferred_element_type=F32)


def _inproj(x, lw, l, *, lt, name):
    nb, t, d = x.shape
    nc = t // lt
    rows = nb * lt
    return pl.pallas_call(
        _inproj_body,
        grid=(nc,),
        in_specs=[
            pl.BlockSpec((nb, lt, d), lambda c: (0, c, 0)),
            _layer_spec(lw["ln1"], l),
            pl.BlockSpec((None, d, D_MAIN), lambda c: (l, 0, 0), pipeline_mode=pl.Buffered(1)),
            _layer_spec(lw["wgi"], l),
            _layer_spec(lw["wgf"], l),
        ],
        out_specs=[
            pl.BlockSpec((N_SLABS, rows, LANES), lambda c: (0, c, 0)),
            pl.BlockSpec((rows, D_ML), lambda c: (c, 0)),
            pl.BlockSpec((nb * GATE_ROWS, lt), lambda c: (c, 0)),
            pl.BlockSpec((nb * GATE_ROWS, lt), lambda c: (c, 0)),
        ],
        out_shape=[
            jax.ShapeDtypeStruct((N_SLABS, nc * rows, LANES), F32),
            jax.ShapeDtypeStruct((nc * rows, D_ML), F32),
            jax.ShapeDtypeStruct((nc * nb * GATE_ROWS, lt), F32),
            jax.ShapeDtypeStruct((nc * nb * GATE_ROWS, lt), F32),
        ],
        compiler_params=pltpu.CompilerParams(dimension_semantics=("arbitrary",), vmem_limit_bytes=VMEM_LIMIT),
        name=name,
    )(x, lw["ln1"], lw["w_in"], lw["wgi"], lw["wgf"])


def _outmlp_body(x_ref, y_ref, wo_ref, g2_ref, w1_ref, w2_ref, gf_ref, o_ref, *, final, nzero):
    nb, lt, d = x_ref.shape
    rows = nb * lt

    def compute():
        x = x_ref[...].reshape(rows, d)
        y = y_ref[...].reshape(rows, d)
        x1 = x + jnp.dot(y, wo_ref[...], preferred_element_type=F32)
        u2 = _rms(x1, g2_ref[...]).astype(BF16)
        acc = x1
        for c in range(D_FF // FF_TILE):
            cols = slice(c * FF_TILE, (c + 1) * FF_TILE)
            h = jnp.dot(u2, w1_ref[:, cols], preferred_element_type=F32)
            h = jnp.square(jnp.maximum(h, 0.0)).astype(BF16)
            acc = acc + jnp.dot(h, w2_ref[cols, :], preferred_element_type=F32)
        if final:
            acc = _rms(acc, gf_ref[...])
        o_ref[...] = acc.reshape(nb, lt, d)

    def zero():
        o_ref[...] = jnp.zeros(o_ref.shape, F32)

    if nzero:
        i = pl.program_id(0)
        pl.when(i < nzero)(zero)
        pl.when(i >= nzero)(compute)
    else:
        compute()


def _outmlp(x, y, lw, l, gf, *, lt, final, skip, nzero, name):
    nb, t, d = x.shape
    nsteps = t // lt - skip
    return pl.pallas_call(
        functools.partial(_outmlp_body, final=final, nzero=nzero),
        grid=(nsteps,),
        in_specs=[
            pl.BlockSpec((nb, lt, d), lambda i: (0, i + skip, 0)),
            pl.BlockSpec((nb, lt, d), lambda i: (0, i + skip, 0)),
            _layer_spec(lw["w_out"], l),
            _layer_spec(lw["ln2"], l),
            _layer_spec(lw["w_ff1"], l),
            _layer_spec(lw["w_ff2"], l),
            _const_spec((1, d)),
        ],
        out_specs=pl.BlockSpec((nb, lt, d), lambda i: (0, i, 0)),
        out_shape=jax.ShapeDtypeStruct((nb, nsteps * lt, d), F32),
        compiler_params=pltpu.CompilerParams(dimension_semantics=("arbitrary",), vmem_limit_bytes=VMEM_LIMIT),
        name=name,
    )(x, y, lw["w_out"], lw["ln2"], lw["w_ff1"], lw["w_ff2"], gf)


def _rg_gates(xc, wa_ref, wx_ref, rba_ref, rbx_ref, lam_ref, g):
    sl = slice(g * LANES, (g + 1) * LANES)
    xg = xc[:, sl]
    xb = xg.astype(BF16)
    r = _sigmoid(jnp.dot(xb, wa_ref[g], preferred_element_type=F32) + rba_ref[:, sl])
    i = _sigmoid(jnp.dot(xb, wx_ref[g], preferred_element_type=F32) + rbx_ref[:, sl])
    log_a = (RG_C * r) * jax.nn.log_sigmoid(lam_ref[:, sl])
    a = jnp.exp(log_a)
    gx = _sqrt_nonneg(1.0 - a * a) * (i * xg)
    return a, gx


def _row_scan(x, op, fill):
    row = lax.broadcasted_iota(jnp.int32, x.shape, 0)
    sh = 1
    while sh < x.shape[0]:
        x = op(x, jnp.where(row >= sh, pltpu.roll(x, sh, 0), fill))
        sh *= 2
    return x


def _slabs(ptm_ref, first):
    return jnp.concatenate([ptm_ref[first + g] for g in range(N_GROUPS)], axis=1)


def _pmix_body(ptm_ref, pbm_ref, gi_ref, gf_ref, rcw_ref, rcb_ref, wa_ref, wx_ref, rba_ref, rbx_ref, lam_ref,
               mcw_ref, mcb_ref, wq_ref, wkt_ref, wv_ref, bi_ref, bf_ref, mng_ref,
               y_ref, rgh_ref, rgc_ref, mcv_ref, st_ref, m_ref,
               a_s, g_s, bm_s, q_s, kt_s, v_s, col_s, row_s, *, pad):
    nb = y_ref.shape[0]
    lt = CHUNK
    rows = nb * lt
    ntail = (CONV_W - 1) * nb
    c = pl.program_id(0)

    @pl.when(c == 0)
    def _init():
        rgh_ref[...] = jnp.zeros_like(rgh_ref)
        rgc_ref[...] = jnp.zeros_like(rgc_ref)
        mcv_ref[...] = jnp.zeros_like(mcv_ref)
        st_ref[...] = jnp.zeros_like(st_ref)
        m_ref[...] = jnp.zeros_like(m_ref)
        ones_col = (lax.broadcasted_iota(jnp.int32, (rows, ML_HD), 1) == 0).astype(BF16)
        for h in range(ML_HEADS):
            v_s[h, :, ML_HD:S_COLS] = ones_col

    def conv(x, w_ref, b_ref, tail_ref):
        tail = tail_ref[...]
        acc = b_ref[...] + w_ref[CONV_W - 1:CONV_W, :] * x
        for k in range(1, CONV_W):
            shifted = jnp.concatenate([tail[ntail - k * nb:], x[:rows - k * nb]], axis=0)
            acc = acc + w_ref[CONV_W - 1 - k:CONV_W - k, :] * shifted
        tail_ref[...] = x[rows - ntail:]
        return acc

    mx = _slabs(ptm_ref, 2 * N_GROUPS)
    mconv = conv(mx, mcw_ref, mcb_ref, mcv_ref)
    mc = mconv * _sigmoid(mconv)
    for t in range(lt):
        rs = slice(t * nb, (t + 1) * nb)
        idx = pl.ds(t, nb, stride=PITCH)
        for g in range(N_GROUPS):
            cols = slice(g * LANES, (g + 1) * LANES)
            bm_s[g, idx, :] = mc[rs, cols]
            bm_s[N_GROUPS + g, idx, :] = mx[rs, cols]

    def seq_major(first):
        return jnp.concatenate(
            [jnp.concatenate([bm_s[first + g, b * PITCH:b * PITCH + lt, :] for b in range(nb)], axis=0)
             for g in range(N_GROUPS)], axis=1).astype(BF16)

    mcb = seq_major(0)
    mxb = seq_major(N_GROUPS)
    for h in range(ML_HEADS):
        sl = slice(h * ML_HD, (h + 1) * ML_HD)
        q_s[h] = jnp.dot(mcb[:, sl], wq_ref[h], preferred_element_type=F32) * (ML_HD ** -0.5)
        kt = lax.dot_general(wkt_ref[h], mcb[:, sl], _NT, preferred_element_type=F32)
        for b in range(nb):
            kt_s[h, b] = kt[:, b * lt:(b + 1) * lt]
        v_s[h, :, 0:ML_HD] = jnp.dot(mxb[:, sl], wv_ref[h], preferred_element_type=F32).astype(BF16)

    def to_cols(r):
        return jnp.concatenate([r, jnp.zeros((LANES - r.shape[0], lt), F32)], axis=0).T

    ig = to_cols(gi_ref[...]) + bi_ref[...]
    lf = jax.nn.log_sigmoid(to_cols(gf_ref[...]) + bf_ref[...])
    if pad:
        trow = lax.broadcasted_iota(jnp.int32, ig.shape, 0)
        is_pad = trow < jnp.where(c == 0, pad, 0)
        ig = jnp.where(is_pad, -1e30, ig)
        lf = jnp.where(is_pad, 0.0, lf)
    bcs = _row_scan(lf, jnp.add, 0.0)
    gg = ig - bcs
    cm = _row_scan(gg, jnp.maximum, -jnp.inf)
    m0 = m_ref[0:1, :]
    mm = jnp.maximum(m0, cm)
    mt = bcs + mm
    b_last = bcs[lt - 1:lt, :]
    m_last = mt[lt - 1:lt, :]
    col_s[0] = mm
    col_s[1] = jnp.exp(m0 - mm)
    col_s[2] = jnp.exp(-mt)
    row_s[0] = gg.T
    row_s[1] = jnp.exp(b_last - m_last + gg).T
    row_s[2] = jnp.broadcast_to(jnp.exp(b_last + m0 - m_last), (lt, LANES)).T
    m_ref[...] = jnp.broadcast_to(m_last, m_ref.shape)

    rg = {}

    def rg_conv():
        rg["xc"] = conv(_slabs(ptm_ref, 0), rcw_ref, rcb_ref, rgc_ref)
        if pad:
            rowi = lax.broadcasted_iota(jnp.int32, (rows, 1), 0)
            rg["keep"] = rowi >= jnp.where(c == 0, pad * nb, 0)

    def rg_gates(g):
        a, gx = _rg_gates(rg["xc"], wa_ref, wx_ref, rba_ref, rbx_ref, lam_ref, g)
        if pad:
            gx = jnp.where(rg["keep"], gx, 0.0)
        a_s[g] = a
        g_s[g] = gx

    def rg_scan(g):
        sl = slice(g * LANES, (g + 1) * LANES)
        gate = jax.nn.gelu(ptm_ref[N_GROUPS + g])
        h = rgh_ref[:, sl]
        for t in range(lt):
            rs = slice(t * nb, (t + 1) * nb)
            h = a_s[g, rs, :] * h + g_s[g, rs, :]
            bm_s[g, pl.ds(t, nb, stride=PITCH), :] = h * gate[rs]
        rgh_ref[:, sl] = h

    def rg_out(g):
        for b in range(nb):
            y_ref[b, :, g * LANES:(g + 1) * LANES] = bm_s[g, b * PITCH:b * PITCH + lt, :].astype(BF16)

    rg_pieces = [[] for _ in range(nb)]
    rg_pieces[0].append(rg_conv)
    for g in range(N_GROUPS):
        rg_pieces[g].append(functools.partial(rg_gates, g))
        rg_pieces[g + 1].append(functools.partial(rg_scan, g))
        rg_pieces[g + 2].append(functools.partial(rg_out, g))

    t_i = lax.broadcasted_iota(jnp.int32, (lt, lt), 0)
    s_i = lax.broadcasted_iota(jnp.int32, (lt, lt), 1)
    causal = s_i <= t_i

    for b in range(nb):
        for piece in rg_pieces[b]:
            piece()
        trows = slice(b * lt, (b + 1) * lt)
        for h in range(ML_HEADS):
            sl = slice(h * ML_HD, (h + 1) * ML_HD)
            r = b * GATE_ROWS + h
            qb = q_s[h, trows, :].astype(BF16)
            kt = kt_s[h, b]
            vext = v_s[h, trows, :]
            sidx = b * ML_HEADS + h
            s0 = st_ref[sidx]
            mm_c = col_s[0, :, r:r + 1]
            inter_c = col_s[1, :, r:r + 1]
            emm_c = col_s[2, :, r:r + 1]
            sc = jnp.dot(qb, kt.astype(BF16), preferred_element_type=F32)
            d = jnp.exp(jnp.where(causal, row_s[0, r:r + 1, :] - mm_c, -jnp.inf))
            pm = (sc * d).astype(BF16)
            numx = (jnp.dot(pm, vext, preferred_element_type=F32)
                    + inter_c * jnp.dot(qb, s0.astype(BF16), preferred_element_type=F32))
            den = numx[:, ML_HD:ML_HD + 1]
            hout = numx[:, 0:ML_HD] / jnp.maximum(jnp.abs(den), emm_c)
            hm = _sigmoid(pbm_ref[trows, sl]) * hout
            hm = hm * lax.rsqrt(jnp.mean(hm * hm, axis=-1, keepdims=True) + EPS)
            y_ref[b, :, D_RG + h * ML_HD:D_RG + (h + 1) * ML_HD] = (hm * mng_ref[:, sl]).astype(BF16)
            wkt = (kt * row_s[1, r:r + 1, :]).astype(BF16)
            sc_row = jnp.concatenate([row_s[2, r:r + 1, :], row_s[2, r:r + 1, :]], axis=1)
            st_ref[sidx] = sc_row * s0 + jnp.dot(wkt, vext, preferred_element_type=F32)

    if pad:
        @pl.when(c == 0)
        def _zero_pad_rows():
            y_ref[:, 0:pad, :] = jnp.zeros((nb, pad, y_ref.shape[2]), BF16)


def _pmix(ptm, pbm, gi, gf, lw, l, *, nb, nc, pad, name):
    rows = nb * CHUNK
    ng = nb * GATE_ROWS
    ntail = (CONV_W - 1) * nb
    in_arrays = [ptm, pbm, gi, gf, lw["rcw"], lw["rcb"], lw["wa"], lw["wx"], lw["rba"], lw["rbx"], lw["lam"],
                 lw["mcw"], lw["mcb"], lw["wq"], lw["wkt"], lw["wv"], lw["bi"], lw["bf"], lw["mng"]]
    in_specs = [
        pl.BlockSpec((N_SLABS, rows, LANES), lambda c: (0, c, 0)),
        pl.BlockSpec((rows, D_ML), lambda c: (c, 0)),
        pl.BlockSpec((ng, CHUNK), lambda c: (c, 0)),
        pl.BlockSpec((ng, CHUNK), lambda c: (c, 0)),
    ] + [_layer_spec(a, l) for a in in_arrays[4:]]
    out_shape = [
        jax.ShapeDtypeStruct((nb, nc * CHUNK, D_MODEL), BF16),
        jax.ShapeDtypeStruct((nb, D_RG), F32),
        jax.ShapeDtypeStruct((ntail, D_RG), F32),
        jax.ShapeDtypeStruct((ntail, D_ML), F32),
        jax.ShapeDtypeStruct((nb * ML_HEADS, ML_HD, S_COLS), F32),
        jax.ShapeDtypeStruct((SUBLANES, LANES), F32),
    ]
    out_specs = [pl.BlockSpec((nb, CHUNK, D_MODEL), lambda c: (0, c, 0))] + [
        pl.BlockSpec(s.shape, lambda c, n=len(s.shape): (0,) * n) for s in out_shape[1:]]
    scratch = [
        pltpu.VMEM((N_GROUPS, rows, LANES), F32),
        pltpu.VMEM((N_GROUPS, rows, LANES), F32),
        pltpu.VMEM((2 * N_GROUPS, nb * PITCH, LANES), F32),
        pltpu.VMEM((ML_HEADS, rows, ML_HD), F32),
        pltpu.VMEM((ML_HEADS, nb, ML_HD, CHUNK), F32),
        pltpu.VMEM((ML_HEADS, rows, S_COLS), BF16),
        pltpu.VMEM((3, CHUNK, LANES), F32),
        pltpu.VMEM((3, LANES, CHUNK), F32),
    ]
    return pl.pallas_call(
        functools.partial(_pmix_body, pad=pad),
        grid=(nc,),
        in_specs=in_specs,
        out_specs=out_specs,
        out_shape=out_shape,
        scratch_shapes=scratch,
        compiler_params=pltpu.CompilerParams(dimension_semantics=("arbitrary",), vmem_limit_bytes=VMEM_LIMIT),
        name=name,
    )(*in_arrays)


SBLK = SUBLANES


def _smix_body(ptm_ref, pbm_ref, gi_ref, gf_ref, rgh_ref, rgc_ref, mcv_ref, c_ref, n_ref, mcol_ref, mrow_ref,
               rcw_ref, rcb_ref, wa_ref, wx_ref, rba_ref, rbx_ref, lam_ref,
               mcw_ref, mcb_ref, wq_ref, wk_ref, wv_ref, bi_ref, bf_ref, bic_ref, bfc_ref, mng_ref,
               y_ref, rgh_o, rgc_o, mcv_o, n_o, m_o, inter_o, dk_o, v_o,
               q_s, qc_s, col_s):
    i = pl.program_id(0)
    ns = pbm_ref.shape[0]

    def conv(first, w_ref, b_ref, tail_ref, tail_o):
        x = _slabs(ptm_ref, first)
        acc = b_ref[...] + w_ref[CONV_W - 1:CONV_W, :] * x
        for j in range(CONV_W - 1):
            acc = acc + w_ref[j:j + 1, :] * tail_ref[j]
        for j in range(CONV_W - 2):
            tail_o[j] = tail_ref[j + 1]
        tail_o[CONV_W - 2] = x
        return acc, x

    @pl.when(i == 0)
    def _rowwise():
        xc, _ = conv(0, rcw_ref, rcb_ref, rgc_ref, rgc_o)
        for g in range(N_GROUPS):
            sl = slice(g * LANES, (g + 1) * LANES)
            a, gx = _rg_gates(xc, wa_ref, wx_ref, rba_ref, rbx_ref, lam_ref, g)
            hn = a * rgh_ref[:, sl] + gx
            rgh_o[:, sl] = hn
            y_ref[0, :, sl] = (hn * jax.nn.gelu(ptm_ref[N_GROUPS + g])).astype(BF16)

        mconv, mx = conv(2 * N_GROUPS, mcw_ref, mcb_ref, mcv_ref, mcv_o)
        mcb = (mconv * _sigmoid(mconv)).astype(BF16)
        mxb = mx.astype(BF16)

        ig_r = gi_ref[...] + bi_ref[...]
        lf_r = jax.nn.log_sigmoid(gf_ref[...] + bf_ref[...])
        m0_r = mrow_ref[...]
        m_o[...] = jnp.maximum(lf_r + m0_r, ig_r)
        z = jnp.concatenate([gi_ref[...], gf_ref[...], jnp.zeros((ns - 2 * GATE_ROWS, ns), F32)], axis=0).T
        ig_c = z[:, 0:GATE_ROWS] + bic_ref[...]
        lf_c = jax.nn.log_sigmoid(z[:, GATE_ROWS:2 * GATE_ROWS] + bfc_ref[...])
        m0_c = mcol_ref[...]
        m_c = jnp.maximum(lf_c + m0_c, ig_c)
        inter_c = jnp.exp(lf_c + m0_c - m_c)
        dd_c = jnp.exp(ig_c - m_c)
        inter_o[...] = inter_c
        col_s[0] = inter_c
        col_s[1] = jnp.exp(-m_c)
        for h in range(ML_HEADS):
            sl = slice(h * ML_HD, (h + 1) * ML_HD)
            q = jnp.dot(mcb[:, sl], wq_ref[h], preferred_element_type=F32) * (ML_HD ** -0.5)
            k = jnp.dot(mcb[:, sl], wk_ref[h], preferred_element_type=F32)
            v = jnp.dot(mxb[:, sl], wv_ref[h], preferred_element_type=F32)
            q_s[h] = q
            dk_o[h] = dd_c[:, h:h + 1] * k
            v_o[h] = v
            col_s[2, :, h:h + 1] = jnp.sum(q * k, axis=-1, keepdims=True) * dd_c[:, h:h + 1]
            n0 = n_ref[h]
            col_s[3, :, h:h + 1] = jnp.sum(q * n0, axis=-1, keepdims=True)
            n_o[h] = inter_c[:, h:h + 1] * n0 + dd_c[:, h:h + 1] * k

    blk = pl.ds(pl.multiple_of(i * SBLK, SBLK), SBLK)
    for h in range(ML_HEADS):
        z = jnp.concatenate([q_s[h, blk, :], jnp.zeros((ML_HD - SBLK, ML_HD), F32)], axis=0).T
        qc_s[h, blk, :] = jnp.concatenate(
            [jnp.sum(z[:, j:j + 1] * c_ref[0, j, h], axis=0, keepdims=True) for j in range(SBLK)], axis=0)

    @pl.when(i == pl.num_programs(0) - 1)
    def _finish():
        for h in range(ML_HEADS):
            sl = slice(h * ML_HD, (h + 1) * ML_HD)
            inter = col_s[0, :, h:h + 1]
            s = col_s[2, :, h:h + 1]
            num = s * v_o[h] + inter * qc_s[h]
            den = s + inter * col_s[3, :, h:h + 1]
            hout = num / jnp.maximum(jnp.abs(den), col_s[1, :, h:h + 1])
            hm = _sigmoid(pbm_ref[:, sl]) * hout
            hm = hm * lax.rsqrt(jnp.mean(hm * hm, axis=-1, keepdims=True) + EPS)
            y_ref[0, :, D_RG + h * ML_HD:D_RG + (h + 1) * ML_HD] = (hm * mng_ref[:, sl]).astype(BF16)


def _smix(ptm, pbm, gi, gf, st, lw, l, *, name):
    ns = pbm.shape[0]
    rgh, rgc, mcv, mc_all, mn, mcol, mrow = st
    in_arrays = [ptm, pbm, gi, gf, rgh, rgc, mcv, mc_all, mn, mcol, mrow,
                 lw["rcw"], lw["rcb"], lw["wa"], lw["wx"], lw["rba"], lw["rbx"], lw["lam"],
                 lw["mcw"], lw["mcb"], lw["wq"], lw["wk"], lw["wv"], lw["bi_s"], lw["bf_s"],
                 lw["bi_c"], lw["bf_c"], lw["mng"]]
    in_specs = [_const_spec(a.shape) for a in in_arrays[:4]] + [_layer_spec(a, l) for a in in_arrays[4:]]
    in_specs[7] = pl.BlockSpec((1, SBLK, ML_HEADS, ML_HD, ML_HD), lambda i: (l, i, 0, 0, 0))
    out_shape = [
        jax.ShapeDtypeStruct((1, ns, D_MODEL), BF16),
        jax.ShapeDtypeStruct((ns, D_RG), F32),
        jax.ShapeDtypeStruct((CONV_W - 1, ns, D_RG), F32),
        jax.ShapeDtypeStruct((CONV_W - 1, ns, D_ML), F32),
        jax.ShapeDtypeStruct((ML_HEADS, ns, ML_HD), F32),
        jax.ShapeDtypeStruct((GATE_ROWS, ns), F32),
        jax.ShapeDtypeStruct((ns, GATE_ROWS), F32),
        jax.ShapeDtypeStruct((ML_HEADS, ns, ML_HD), F32),
        jax.ShapeDtypeStruct((ML_HEADS, ns, ML_HD), F32),
    ]
    out_specs = [pl.BlockSpec(s.shape, lambda i, n=len(s.shape): (0,) * n) for s in out_shape]
    scratch = [
        pltpu.VMEM((ML_HEADS, ns, ML_HD), F32),
        pltpu.VMEM((ML_HEADS, ns, ML_HD), F32),
        pltpu.VMEM((4, ns, GATE_ROWS), F32),
    ]
    return pl.pallas_call(
        _smix_body,
        grid=(ns // SBLK,),
        in_specs=in_specs,
        out_specs=out_specs,
        out_shape=out_shape,
        scratch_shapes=scratch,
        compiler_params=pltpu.CompilerParams(dimension_semantics=("arbitrary",), vmem_limit_bytes=VMEM_LIMIT),
        name=name,
    )(*in_arrays)


CBLK = 2 * SUBLANES


def _cupdate_body(c_ref, inter_ref, dk_ref, v_ref, o_ref):
    for h in range(ML_HEADS):
        z = jnp.concatenate([dk_ref[0, h], jnp.zeros((ML_HD - CBLK, ML_HD), F32)], axis=0).T
        inter_rows = jnp.broadcast_to(inter_ref[0, :, h:h + 1], (CBLK, LANES))
        vblk = v_ref[0, h]
        for j in range(CBLK):
            o_ref[0, j, h] = inter_rows[j:j + 1, :] * c_ref[0, j, h] + z[:, j:j + 1] * vblk[j:j + 1, :]


def _cupdate(mc_all, inter_all, dk_all, v_all):
    depth, ns = mc_all.shape[:2]
    cspec = pl.BlockSpec((1, CBLK, ML_HEADS, ML_HD, ML_HD), lambda l, i: (l, i, 0, 0, 0))
    kvspec = pl.BlockSpec((1, ML_HEADS, CBLK, ML_HD), lambda l, i: (l, 0, i, 0))
    return pl.pallas_call(
        _cupdate_body,
        grid=(depth, ns // CBLK),
        in_specs=[cspec, pl.BlockSpec((1, CBLK, GATE_ROWS), lambda l, i: (l, i, 0)), kvspec, kvspec],
        out_specs=cspec,
        out_shape=jax.ShapeDtypeStruct(mc_all.shape, F32),
        compiler_params=pltpu.CompilerParams(dimension_semantics=("arbitrary", "arbitrary"),
                                             vmem_limit_bytes=VMEM_LIMIT),
        name="cupdate_s",
    )(mc_all, inter_all, dk_all, v_all)


def _block_diag_pairs(w):
    depth = w.shape[0]
    w = w.reshape(depth, N_GROUPS, 2, RG_BLOCK, RG_BLOCK)
    zero = jnp.zeros((depth, N_GROUPS, RG_BLOCK, RG_BLOCK), w.dtype)
    top = jnp.concatenate([w[:, :, 0], zero], axis=-1)
    bottom = jnp.concatenate([zero, w[:, :, 1]], axis=-1)
    return jnp.concatenate([top, bottom], axis=-2)


def _gate_rows(w_cols):
    depth = w_cols.shape[0]
    return jnp.concatenate([jnp.swapaxes(w_cols, 1, 2),
                            jnp.zeros((depth, GATE_ROWS - ML_HEADS, D_MODEL), w_cols.dtype)], axis=1)


def _stacked_weights(ln1_g, w_in, rg_conv_w, rg_conv_b, rg_w_a, rg_w_x, rg_b_a, rg_b_x, rg_lambda,
                     m_conv_w, m_conv_b, m_w_q, m_w_k, m_w_v, m_b_i, m_b_f, m_norm_g, w_out, ln2_g,
                     w_ff1, w_ff2, nb_prompt, ns):
    depth = ln1_g.shape[0]
    row = lambda a: a.reshape(depth, 1, -1)
    pad8 = lambda a: jnp.concatenate([a, jnp.zeros((depth, GATE_ROWS - ML_HEADS), F32)], axis=1)
    bias8, bfor8 = pad8(m_b_i), pad8(m_b_f)
    lanes = lambda a: jnp.concatenate([jnp.tile(a, (1, nb_prompt)),
                                       jnp.zeros((depth, LANES - nb_prompt * GATE_ROWS), F32)], axis=1)[:, None, :]
    return dict(
        ln1=row(ln1_g),
        w_in=w_in.astype(BF16),
        wgi=_gate_rows(w_in[:, :, D_MAIN:D_MAIN + ML_HEADS]).astype(BF16),
        wgf=_gate_rows(w_in[:, :, D_MAIN + ML_HEADS:]).astype(BF16),
        rcw=rg_conv_w, rcb=row(rg_conv_b),
        wa=_block_diag_pairs(rg_w_a).astype(BF16), wx=_block_diag_pairs(rg_w_x).astype(BF16),
        rba=row(rg_b_a), rbx=row(rg_b_x), lam=row(rg_lambda),
        mcw=m_conv_w, mcb=row(m_conv_b),
        wq=m_w_q.astype(BF16), wk=m_w_k.astype(BF16), wkt=jnp.swapaxes(m_w_k, 2, 3).astype(BF16),
        wv=m_w_v.astype(BF16),
        bi=lanes(bias8), bf=lanes(bfor8),
        bi_s=jnp.broadcast_to(bias8[:, :, None], (depth, GATE_ROWS, ns)),
        bf_s=jnp.broadcast_to(bfor8[:, :, None], (depth, GATE_ROWS, ns)),
        bi_c=bias8.reshape(depth, 1, GATE_ROWS), bf_c=bfor8.reshape(depth, 1, GATE_ROWS),
        mng=row(m_norm_g),
        w_out=w_out.astype(BF16), ln2=row(ln2_g),
        w_ff1=w_ff1.astype(BF16), w_ff2=w_ff2.astype(BF16),
    )


def kernel(x_prompt, x_sample, state_rg_h, state_rg_conv, state_m_conv, state_m_C, state_m_n, state_m_m,
           meta_tokens, ln1_g, w_in, rg_conv_w, rg_conv_b, rg_w_a, rg_w_x, rg_b_a, rg_b_x, rg_lambda,
           m_conv_w, m_conv_b, m_w_q, m_w_k, m_w_v, m_b_i, m_b_f, m_norm_g, w_out, ln2_g,
           w_ff1, w_ff2, ln_f_g):
    nb, seq, d = x_prompt.shape
    ns = x_sample.shape[0]
    depth = ln1_g.shape[0]
    t_real = N_META + seq
    nc = -(-t_real // CHUNK)
    pad = nc * CHUNK - t_real
    assert (pad + N_META) % CHUNK == 0 and x_sample.shape[1] == 1 and ns == LANES and depth == DEPTH
    assert CHUNK == LANES and nb * GATE_ROWS <= LANES

    lw = _stacked_weights(ln1_g, w_in, rg_conv_w, rg_conv_b, rg_w_a, rg_w_x, rg_b_a, rg_b_x, rg_lambda,
                          m_conv_w, m_conv_b, m_w_q, m_w_k, m_w_v, m_b_i, m_b_f, m_norm_g, w_out, ln2_g,
                          w_ff1, w_ff2, nb, ns)
    lnf = ln_f_g.reshape(1, d)
    xp = jnp.concatenate([jnp.zeros((nb, pad, d), F32),
                          jnp.broadcast_to(meta_tokens.astype(F32)[None], (nb, N_META, d)),
                          x_prompt], axis=1)
    xs = x_sample.reshape(1, ns, d)
    m_cols = jnp.concatenate([state_m_m, jnp.zeros((depth, ns, GATE_ROWS - ML_HEADS), F32)], axis=2)
    st_in = (state_rg_h, jnp.swapaxes(state_rg_conv, 1, 2), jnp.swapaxes(state_m_conv, 1, 2), state_m_C,
             jnp.swapaxes(state_m_n, 1, 2), m_cols, jnp.swapaxes(m_cols, 1, 2))

    p_states = [[] for _ in range(6)]
    s_states = [[] for _ in range(6)]
    for l in range(depth):
        last = l == depth - 1

        ptm, pbm, gi, gf = _inproj(xp, lw, l, lt=CHUNK, name=f"inproj_p{l}")
        y, rgh, rgc, mcv, st, mrow = _pmix(ptm, pbm, gi, gf, lw, l, nb=nb, nc=nc, pad=pad, name=f"mixer_p{l}")
        lt_out = CHUNK // 2
        skip = (pad + N_META) // lt_out if last else 0
        xp = _outmlp(xp, y, lw, l, lnf, lt=lt_out, final=last, skip=skip,
                     nzero=0 if last else pad // lt_out, name=f"outmlp_p{l}")
        st = st.reshape(nb, ML_HEADS, ML_HD, S_COLS)
        p_states[0].append(rgh)
        p_states[1].append(jnp.swapaxes(rgc.reshape(CONV_W - 1, nb, D_RG), 0, 1))
        p_states[2].append(jnp.swapaxes(mcv.reshape(CONV_W - 1, nb, D_ML), 0, 1))
        p_states[3].append(st[..., :ML_HD])
        p_states[4].append(st[..., ML_HD])
        p_states[5].append(mrow[0, :nb * GATE_ROWS].reshape(nb, GATE_ROWS)[:, :ML_HEADS])

        stm, sbm, gis, gfs = _inproj(xs, lw, l, lt=ns, name=f"inproj_s{l}")
        ys, srgh, srgc, smcv, smn, smrow, sinter, sdk, sv = _smix(stm, sbm, gis, gfs, st_in, lw, l,
                                                                  name=f"mixer_s{l}")
        xs = _outmlp(xs, ys, lw, l, lnf, lt=ns, final=last, skip=0, nzero=0, name=f"outmlp_s{l}")
        s_states[0].append(srgh)
        s_states[1].append(jnp.swapaxes(srgc, 0, 1))
        s_states[2].append(jnp.swapaxes(smcv, 0, 1))
        s_states[3].append((sinter, sdk, sv))
        s_states[4].append(jnp.swapaxes(smn, 0, 1))
        s_states[5].append(smrow[:ML_HEADS].T)

    y_prompt = xp
    y_sample = xs.reshape(ns, 1, d)
    ps_out = [jnp.stack(s) for s in p_states]
    c_new = _cupdate(state_m_C, *(jnp.stack([t[j] for t in s_states[3]]) for j in range(3)))
    ss_out = [c_new if j == 3 else jnp.stack(s) for j, s in enumerate(s_states)]
    return (y_prompt, y_sample, *ps_out, *ss_out)
```

```python
import functools

import jax
import jax.numpy as jnp
from jax import lax
from jax.experimental import pallas as pl
from jax.experimental.pallas import tpu as pltpu

F32 = jnp.float32
BF16 = jnp.bfloat16

D_MODEL = 1024
N_META = 16
D_RG = 512
RG_BLOCK = 64
RG_C = 8.0
D_ML = 512
ML_HEADS = 4
ML_HD = 128
CONV_W = 4
D_FF = 4096
EPS = 1e-6
DEPTH = 4

LANES = 128
SUBLANES = 8
CHUNK = 128
PITCH = CHUNK + SUBLANES
N_GROUPS = D_RG // LANES
D_MAIN = 2 * D_RG + 2 * D_ML
GATE_ROWS = SUBLANES
S_COLS = 2 * ML_HD
N_SLABS = 3 * N_GROUPS
FF_TILE = 1024
VMEM_LIMIT = 56 * 1024 * 1024

_NT = (((1,), (1,)), ((), ()))


def _const_spec(shape):
    zeros = (0,) * len(shape)
    return pl.BlockSpec(shape, lambda *_: zeros, pipeline_mode=pl.Buffered(1))


def _layer_spec(arr, l):
    tail = (0,) * (arr.ndim - 1)
    return pl.BlockSpec((None,) + arr.shape[1:], lambda *_: (l,) + tail, pipeline_mode=pl.Buffered(1))


def _rms(x, g):
    return x * lax.rsqrt(jnp.mean(x * x, axis=-1, keepdims=True) + EPS) * g


def _sigmoid(x):
    return 0.5 * jnp.tanh(0.5 * x) + 0.5


def _sqrt_nonneg(s):
    return jnp.where(s > 0.0, s * lax.rsqrt(s), 0.0)


def _project(u, w_ref, n, ptm_ref, pbm_ref, nb, lt):
    pr = jnp.dot(u, w_ref[:, n * 512:(n + 1) * 512], preferred_element_type=F32)
    if n == D_MAIN // 512 - 1:
        pbm_ref[...] = pr
        return
    for g in range(N_GROUPS):
        cols = slice(g * LANES, (g + 1) * LANES)
        if nb == 1:
            ptm_ref[n * N_GROUPS + g] = pr[:, cols]
            continue
        for b in range(nb):
            ptm_ref[n * N_GROUPS + g, pl.ds(b, lt, stride=nb), :] = pr[b * lt:(b + 1) * lt, cols]


def _gate_preacts(u, wg_ref, nb, lt):
    return jnp.concatenate([lax.dot_general(wg_ref[...], u[b * lt:(b + 1) * lt], _NT, preferred_element_type=F32)
                            for b in range(nb)], axis=0)


def _inproj_body(x_ref, g_ref, w_ref, wgi_ref, wgf_ref, ptm_ref, pbm_ref, gi_ref, gf_ref):
    nb, lt, d = x_ref.shape
    x = x_ref[...].reshape(nb * lt, d)
    u = _rms(x, g_ref[...]).astype(BF16)
    for n in range(D_MAIN // 512):
        _project(u, w_ref, n, ptm_ref, pbm_ref, nb, lt)
    gi_ref[...] = _gate_preacts(u, wgi_ref, nb, lt)
    gf_ref[...] = _gate_preacts(u, wgf_ref, nb, lt)


def _inproj(x, lw, l, *, lt, name):
    nb, t, d = x.shape
    nc = t // lt
    rows = nb * lt
    return pl.pallas_call(
        _inproj_body,
        grid=(nc,),
        in_specs=[
            pl.BlockSpec((nb, lt, d), lambda c: (0, c, 0)),
            _layer_spec(lw["ln1"], l),
            pl.BlockSpec((None, d, D_MAIN), lambda c: (l, 0, 0), pipeline_mode=pl.Buffered(1)),
            _layer_spec(lw["wgi"], l),
            _layer_spec(lw["wgf"], l),
        ],
        out_specs=[
            pl.BlockSpec((N_SLABS, rows, LANES), lambda c: (0, c, 0)),
            pl.BlockSpec((rows, D_ML), lambda c: (c, 0)),
            pl.BlockSpec((nb * GATE_ROWS, lt), lambda c: (c, 0)),
            pl.BlockSpec((nb * GATE_ROWS, lt), lambda c: (c, 0)),
        ],
        out_shape=[
            jax.ShapeDtypeStruct((N_SLABS, nc * rows, LANES), F32),
            jax.ShapeDtypeStruct((nc * rows, D_ML), F32),
            jax.ShapeDtypeStruct((nc * nb * GATE_ROWS, lt), F32),
            jax.ShapeDtypeStruct((nc * nb * GATE_ROWS, lt), F32),
        ],
        compiler_params=pltpu.CompilerParams(dimension_semantics=("arbitrary",), vmem_limit_bytes=VMEM_LIMIT),
        name=name,
    )(x, lw["ln1"], lw["w_in"], lw["wgi"], lw["wgf"])


def _outmlp_body(x_ref, y_ref, wo_ref, g2_ref, w1_ref, w2_ref, gf_ref, o_ref, *, final, nzero):
    nb, lt, d = x_ref.shape
    rows = nb * lt

    def compute():
        x = x_ref[...].reshape(rows, d)
        y = y_ref[...].reshape(rows, d)
        x1 = x + jnp.dot(y, wo_ref[...], preferred_element_type=F32)
        u2 = _rms(x1, g2_ref[...]).astype(BF16)
        acc = x1
        for c in range(D_FF // FF_TILE):
            cols = slice(c * FF_TILE, (c + 1) * FF_TILE)
            h = jnp.dot(u2, w1_ref[:, cols], preferred_element_type=F32)
            h = jnp.square(jnp.maximum(h, 0.0)).astype(BF16)
            acc = acc + jnp.dot(h, w2_ref[cols, :], preferred_element_type=F32)
        if final:
            acc = _rms(acc, gf_ref[...])
        o_ref[...] = acc.reshape(nb, lt, d)

    def zero():
        o_ref[...] = jnp.zeros(o_ref.shape, F32)

    if nzero:
        i = pl.program_id(0)
        pl.when(i < nzero)(zero)
        pl.when(i >= nzero)(compute)
    else:
        compute()


def _outmlp(x, y, lw, l, gf, *, lt, final, skip, nzero, name):
    nb, t, d = x.shape
    nsteps = t // lt - skip
    return pl.pallas_call(
        functools.partial(_outmlp_body, final=final, nzero=nzero),
        grid=(nsteps,),
        in_specs=[
            pl.BlockSpec((nb, lt, d), lambda i: (0, i + skip, 0)),
            pl.BlockSpec((nb, lt, d), lambda i: (0, i + skip, 0)),
            _layer_spec(lw["w_out"], l),
            _layer_spec(lw["ln2"], l),
            _layer_spec(lw["w_ff1"], l),
            _layer_spec(lw["w_ff2"], l),
            _const_spec((1, d)),
        ],
        out_specs=pl.BlockSpec((nb, lt, d), lambda i: (0, i, 0)),
        out_shape=jax.ShapeDtypeStruct((nb, nsteps * lt, d), F32),
        compiler_params=pltpu.CompilerParams(dimension_semantics=("arbitrary",), vmem_limit_bytes=VMEM_LIMIT),
        name=name,
    )(x, y, lw["w_out"], lw["ln2"], lw["w_ff1"], lw["w_ff2"], gf)


def _rg_gates(xc, wa_ref, wx_ref, rba_ref, rbx_ref, lam_ref, g):
    sl = slice(g * LANES, (g + 1) * LANES)
    xg = xc[:, sl]
    xb = xg.astype(BF16)
    r = _sigmoid(jnp.dot(xb, wa_ref[g], preferred_element_type=F32) + rba_ref[:, sl])
    i = _sigmoid(jnp.dot(xb, wx_ref[g], preferred_element_type=F32) + rbx_ref[:, sl])
    log_a = (RG_C * r) * jax.nn.log_sigmoid(lam_ref[:, sl])
    a = jnp.exp(log_a)
    gx = _sqrt_nonneg(1.0 - a * a) * (i * xg)
    return a, gx


def _row_scan(x, op, fill):
    row = lax.broadcasted_iota(jnp.int32, x.shape, 0)
    sh = 1
    while sh < x.shape[0]:
        x = op(x, jnp.where(row >= sh, pltpu.roll(x, sh, 0), fill))
        sh *= 2
    return x


def _slabs(ptm_ref, first):
    return jnp.concatenate([ptm_ref[first + g] for g in range(N_GROUPS)], axis=1)


def _pmix_body(x_ref, g1_ref, w_ref, wgi_ref, wgf_ref,
               rcw_ref, rcb_ref, wa_ref, wx_ref, rba_ref, rbx_ref, lam_ref,
               mcw_ref, mcb_ref, wq_ref, wkt_ref, wv_ref, bi_ref, bf_ref, mng_ref,
               y_ref, rgh_ref, rgc_ref, mcv_ref, st_ref, m_ref,
               ptm_ref, pbm_ref, a_s, g_s, bm_s, q_s, kt_s, v_s, col_s, row_s, *, pad):
    nb = y_ref.shape[0]
    lt = CHUNK
    rows = nb * lt
    ntail = (CONV_W - 1) * nb
    c = pl.program_id(0)

    @pl.when(c == 0)
    def _init():
        rgh_ref[...] = jnp.zeros_like(rgh_ref)
        rgc_ref[...] = jnp.zeros_like(rgc_ref)
        mcv_ref[...] = jnp.zeros_like(mcv_ref)
        st_ref[...] = jnp.zeros_like(st_ref)
        m_ref[...] = jnp.zeros_like(m_ref)
        ones_col = (lax.broadcasted_iota(jnp.int32, (rows, ML_HD), 1) == 0).astype(BF16)
        for h in range(ML_HEADS):
            v_s[h, :, ML_HD:S_COLS] = ones_col

    def conv(x, w_ref, b_ref, tail_ref):
        tail = tail_ref[...]
        acc = b_ref[...] + w_ref[CONV_W - 1:CONV_W, :] * x
        for k in range(1, CONV_W):
            shifted = jnp.concatenate([tail[ntail - k * nb:], x[:rows - k * nb]], axis=0)
            acc = acc + w_ref[CONV_W - 1 - k:CONV_W - k, :] * shifted
        tail_ref[...] = x[rows - ntail:]
        return acc

    u = _rms(x_ref[...].reshape(rows, x_ref.shape[2]), g1_ref[...]).astype(BF16)
    _project(u, w_ref, 2, ptm_ref, pbm_ref, nb, lt)
    gi = _gate_preacts(u, wgi_ref, nb, lt)
    gf = _gate_preacts(u, wgf_ref, nb, lt)

    mx = _slabs(ptm_ref, 2 * N_GROUPS)
    mconv = conv(mx, mcw_ref, mcb_ref, mcv_ref)
    mc = mconv * _sigmoid(mconv)
    for n in (0, 1, 3):
        _project(u, w_ref, n, ptm_ref, pbm_ref, nb, lt)
    for t in range(lt):
        rs = slice(t * nb, (t + 1) * nb)
        idx = pl.ds(t, nb, stride=PITCH)
        for g in range(N_GROUPS):
            cols = slice(g * LANES, (g + 1) * LANES)
            bm_s[g, idx, :] = mc[rs, cols]
            bm_s[N_GROUPS + g, idx, :] = mx[rs, cols]

    def seq_major(first):
        return jnp.concatenate(
            [jnp.concatenate([bm_s[first + g, b * PITCH:b * PITCH + lt, :] for b in range(nb)], axis=0)
             for g in range(N_GROUPS)], axis=1).astype(BF16)

    mcb = seq_major(0)
    mxb = seq_major(N_GROUPS)
    for h in range(ML_HEADS):
        sl = slice(h * ML_HD, (h + 1) * ML_HD)
        q_s[h] = jnp.dot(mcb[:, sl], wq_ref[h], preferred_element_type=F32) * (ML_HD ** -0.5)
        kt = lax.dot_general(wkt_ref[h], mcb[:, sl], _NT, preferred_element_type=F32)
        for b in range(nb):
            kt_s[h, b] = kt[:, b * lt:(b + 1) * lt]
        v_s[h, :, 0:ML_HD] = jnp.dot(mxb[:, sl], wv_ref[h], preferred_element_type=F32).astype(BF16)

    def to_cols(r):
        return jnp.concatenate([r, jnp.zeros((LANES - r.shape[0], lt), F32)], axis=0).T

    ig = to_cols(gi) + bi_ref[...]
    lf = jax.nn.log_sigmoid(to_cols(gf) + bf_ref[...])
    if pad:
        trow = lax.broadcasted_iota(jnp.int32, ig.shape, 0)
        is_pad = trow < jnp.where(c == 0, pad, 0)
        ig = jnp.where(is_pad, -1e30, ig)
        lf = jnp.where(is_pad, 0.0, lf)
    bcs = _row_scan(lf, jnp.add, 0.0)
    gg = ig - bcs
    cm = _row_scan(gg, jnp.maximum, -jnp.inf)
    m0 = m_ref[0:1, :]
    mm = jnp.maximum(m0, cm)
    mt = bcs + mm
    b_last = bcs[lt - 1:lt, :]
    m_last = mt[lt - 1:lt, :]
    col_s[0] = mm
    col_s[1] = jnp.exp(m0 - mm)
    col_s[2] = jnp.exp(-mt)
    row_s[0] = gg.T
    row_s[1] = jnp.exp(b_last - m_last + gg).T
    row_s[2] = jnp.broadcast_to(jnp.exp(b_last + m0 - m_last), (lt, LANES)).T
    m_ref[...] = jnp.broadcast_to(m_last, m_ref.shape)

    rg = {}

    def rg_conv():
        rg["xc"] = conv(_slabs(ptm_ref, 0), rcw_ref, rcb_ref, rgc_ref)
        if pad:
            rowi = lax.broadcasted_iota(jnp.int32, (rows, 1), 0)
            rg["keep"] = rowi >= jnp.where(c == 0, pad * nb, 0)

    def rg_gates(g):
        a, gx = _rg_gates(rg["xc"], wa_ref, wx_ref, rba_ref, rbx_ref, lam_ref, g)
        if pad:
            gx = jnp.where(rg["keep"], gx, 0.0)
        a_s[g] = a
        g_s[g] = gx

    def rg_scan(g):
        sl = slice(g * LANES, (g + 1) * LANES)
        gate = jax.nn.gelu(ptm_ref[N_GROUPS + g])
        h = rgh_ref[:, sl]
        for t in range(lt):
            rs = slice(t * nb, (t + 1) * nb)
            h = a_s[g, rs, :] * h + g_s[g, rs, :]
            bm_s[g, pl.ds(t, nb, stride=PITCH), :] = h * gate[rs]
        rgh_ref[:, sl] = h

    def rg_out(g):
        for b in range(nb):
            y_ref[b, :, g * LANES:(g + 1) * LANES] = bm_s[g, b * PITCH:b * PITCH + lt, :].astype(BF16)

    rg_pieces = [[] for _ in range(nb)]
    rg_pieces[0].append(rg_conv)
    for g in range(N_GROUPS):
        rg_pieces[g].append(functools.partial(rg_gates, g))
        rg_pieces[g + 1].append(functools.partial(rg_scan, g))
        rg_pieces[g + 2].append(functools.partial(rg_out, g))

    t_i = lax.broadcasted_iota(jnp.int32, (lt, lt), 0)
    s_i = lax.broadcasted_iota(jnp.int32, (lt, lt), 1)
    causal = s_i <= t_i

    for b in range(nb):
        for piece in rg_pieces[b]:
            piece()
        trows = slice(b * lt, (b + 1) * lt)
        for h in range(ML_HEADS):
            sl = slice(h * ML_HD, (h + 1) * ML_HD)
            r = b * GATE_ROWS + h
            qb = q_s[h, trows, :].astype(BF16)
            kt = kt_s[h, b]
            vext = v_s[h, trows, :]
            sidx = b * ML_HEADS + h
            s0 = st_ref[sidx]
            mm_c = col_s[0, :, r:r + 1]
            inter_c = col_s[1, :, r:r + 1]
            emm_c = col_s[2, :, r:r + 1]
            sc = jnp.dot(qb, kt.astype(BF16), preferred_element_type=F32)
            d = jnp.exp(jnp.where(causal, row_s[0, r:r + 1, :] - mm_c, -jnp.inf))
            pm = (sc * d).astype(BF16)
            numx = (jnp.dot(pm, vext, preferred_element_type=F32)
                    + inter_c * jnp.dot(qb, s0.astype(BF16), preferred_element_type=F32))
            den = numx[:, ML_HD:ML_HD + 1]
            hout = numx[:, 0:ML_HD] / jnp.maximum(jnp.abs(den), emm_c)
            hm = _sigmoid(pbm_ref[trows, sl]) * hout
            hm = hm * lax.rsqrt(jnp.mean(hm * hm, axis=-1, keepdims=True) + EPS)
            y_ref[b, :, D_RG + h * ML_HD:D_RG + (h + 1) * ML_HD] = (hm * mng_ref[:, sl]).astype(BF16)
            wkt = (kt * row_s[1, r:r + 1, :]).astype(BF16)
            sc_row = jnp.concatenate([row_s[2, r:r + 1, :], row_s[2, r:r + 1, :]], axis=1)
            st_ref[sidx] = sc_row * s0 + jnp.dot(wkt, vext, preferred_element_type=F32)

    if pad:
        @pl.when(c == 0)
        def _zero_pad_rows():
            y_ref[:, 0:pad, :] = jnp.zeros((nb, pad, y_ref.shape[2]), BF16)


def _pmix(x, lw, l, *, pad, name):
    nb, t, d = x.shape
    nc = t // CHUNK
    rows = nb * CHUNK
    ntail = (CONV_W - 1) * nb
    in_arrays = [x, lw["ln1"], lw["w_in"], lw["wgi"], lw["wgf"],
                 lw["rcw"], lw["rcb"], lw["wa"], lw["wx"], lw["rba"], lw["rbx"], lw["lam"],
                 lw["mcw"], lw["mcb"], lw["wq"], lw["wkt"], lw["wv"], lw["bi"], lw["bf"], lw["mng"]]
    in_specs = [pl.BlockSpec((nb, CHUNK, d), lambda c: (0, c, 0))] + [_layer_spec(a, l) for a in in_arrays[1:]]
    in_specs[2] = pl.BlockSpec((None, d, D_MAIN), lambda c: (l, 0, 0), pipeline_mode=pl.Buffered(1))
    out_shape = [
        jax.ShapeDtypeStruct((nb, nc * CHUNK, D_MODEL), BF16),
        jax.ShapeDtypeStruct((nb, D_RG), F32),
        jax.ShapeDtypeStruct((ntail, D_RG), F32),
        jax.ShapeDtypeStruct((ntail, D_ML), F32),
        jax.ShapeDtypeStruct((nb * ML_HEADS, ML_HD, S_COLS), F32),
        jax.ShapeDtypeStruct((SUBLANES, LANES), F32),
    ]
    out_specs = [pl.BlockSpec((nb, CHUNK, D_MODEL), lambda c: (0, c, 0))] + [
        pl.BlockSpec(s.shape, lambda c, n=len(s.shape): (0,) * n) for s in out_shape[1:]]
    scratch = [
        pltpu.VMEM((N_SLABS, rows, LANES), F32),
        pltpu.VMEM((rows, D_ML), F32),
        pltpu.VMEM((N_GROUPS, rows, LANES), F32),
        pltpu.VMEM((N_GROUPS, rows, LANES), F32),
        pltpu.VMEM((2 * N_GROUPS, nb * PITCH, LANES), F32),
        pltpu.VMEM((ML_HEADS, rows, ML_HD), F32),
        pltpu.VMEM((ML_HEADS, nb, ML_HD, CHUNK), F32),
        pltpu.VMEM((ML_HEADS, rows, S_COLS), BF16),
        pltpu.VMEM((3, CHUNK, LANES), F32),
        pltpu.VMEM((3, LANES, CHUNK), F32),
    ]
    return pl.pallas_call(
        functools.partial(_pmix_body, pad=pad),
        grid=(nc,),
        in_specs=in_specs,
        out_specs=out_specs,
        out_shape=out_shape,
        scratch_shapes=scratch,
        compiler_params=pltpu.CompilerParams(dimension_semantics=("arbitrary",), vmem_limit_bytes=VMEM_LIMIT),
        name=name,
    )(*in_arrays)


SBLK = SUBLANES


def _smix_body(ptm_ref, pbm_ref, gi_ref, gf_ref, rgh_ref, rgc_ref, mcv_ref, c_ref, n_ref, mcol_ref, mrow_ref,
               rcw_ref, rcb_ref, wa_ref, wx_ref, rba_ref, rbx_ref, lam_ref,
               mcw_ref, mcb_ref, wq_ref, wk_ref, wv_ref, bi_ref, bf_ref, bic_ref, bfc_ref, mng_ref,
               y_ref, rgh_o, rgc_o, mcv_o, n_o, m_o, inter_o, dk_o, v_o,
               q_s, qc_s, col_s):
    i = pl.program_id(0)
    ns = pbm_ref.shape[0]

    def conv(first, w_ref, b_ref, tail_ref, tail_o):
        x = _slabs(ptm_ref, first)
        acc = b_ref[...] + w_ref[CONV_W - 1:CONV_W, :] * x
        for j in range(CONV_W - 1):
            acc = acc + w_ref[j:j + 1, :] * tail_ref[j]
        for j in range(CONV_W - 2):
            tail_o[j] = tail_ref[j + 1]
        tail_o[CONV_W - 2] = x
        return acc, x

    @pl.when(i == 0)
    def _rowwise():
        xc, _ = conv(0, rcw_ref, rcb_ref, rgc_ref, rgc_o)
        for g in range(N_GROUPS):
            sl = slice(g * LANES, (g + 1) * LANES)
            a, gx = _rg_gates(xc, wa_ref, wx_ref, rba_ref, rbx_ref, lam_ref, g)
            hn = a * rgh_ref[:, sl] + gx
            rgh_o[:, sl] = hn
            y_ref[0, :, sl] = (hn * jax.nn.gelu(ptm_ref[N_GROUPS + g])).astype(BF16)

        mconv, mx = conv(2 * N_GROUPS, mcw_ref, mcb_ref, mcv_ref, mcv_o)
        mcb = (mconv * _sigmoid(mconv)).astype(BF16)
        mxb = mx.astype(BF16)

        ig_r = gi_ref[...] + bi_ref[...]
        lf_r = jax.nn.log_sigmoid(gf_ref[...] + bf_ref[...])
        m0_r = mrow_ref[...]
        m_o[...] = jnp.maximum(lf_r + m0_r, ig_r)
        z = jnp.concatenate([gi_ref[...], gf_ref[...], jnp.zeros((ns - 2 * GATE_ROWS, ns), F32)], axis=0).T
        ig_c = z[:, 0:GATE_ROWS] + bic_ref[...]
        lf_c = jax.nn.log_sigmoid(z[:, GATE_ROWS:2 * GATE_ROWS] + bfc_ref[...])
        m0_c = mcol_ref[...]
        m_c = jnp.maximum(lf_c + m0_c, ig_c)
        inter_c = jnp.exp(lf_c + m0_c - m_c)
        dd_c = jnp.exp(ig_c - m_c)
        inter_o[...] = inter_c
        col_s[0] = inter_c
        col_s[1] = jnp.exp(-m_c)
        for h in range(ML_HEADS):
            sl = slice(h * ML_HD, (h + 1) * ML_HD)
            q = jnp.dot(mcb[:, sl], wq_ref[h], preferred_element_type=F32) * (ML_HD ** -0.5)
            k = jnp.dot(mcb[:, sl], wk_ref[h], preferred_element_type=F32)
            v = jnp.dot(mxb[:, sl], wv_ref[h], preferred_element_type=F32)
            q_s[h] = q
            dk_o[h] = dd_c[:, h:h + 1] * k
            v_o[h] = v
            col_s[2, :, h:h + 1] = jnp.sum(q * k, axis=-1, keepdims=True) * dd_c[:, h:h + 1]
            n0 = n_ref[h]
            col_s[3, :, h:h + 1] = jnp.sum(q * n0, axis=-1, keepdims=True)
            n_o[h] = inter_c[:, h:h + 1] * n0 + dd_c[:, h:h + 1] * k

    blk = pl.ds(pl.multiple_of(i * SBLK, SBLK), SBLK)
    for h in range(ML_HEADS):
        z = jnp.concatenate([q_s[h, blk, :], jnp.zeros((ML_HD - SBLK, ML_HD), F32)], axis=0).T
        qc_s[h, blk, :] = jnp.concatenate(
            [jnp.sum(z[:, j:j + 1] * c_ref[0, j, h], axis=0, keepdims=True) for j in range(SBLK)], axis=0)

    @pl.when(i == pl.num_programs(0) - 1)
    def _finish():
        for h in range(ML_HEADS):
            sl = slice(h * ML_HD, (h + 1) * ML_HD)
            inter = col_s[0, :, h:h + 1]
            s = col_s[2, :, h:h + 1]
            num = s * v_o[h] + inter * qc_s[h]
            den = s + inter * col_s[3, :, h:h + 1]
            hout = num / jnp.maximum(jnp.abs(den), col_s[1, :, h:h + 1])
            hm = _sigmoid(pbm_ref[:, sl]) * hout
            hm = hm * lax.rsqrt(jnp.mean(hm * hm, axis=-1, keepdims=True) + EPS)
            y_ref[0, :, D_RG + h * ML_HD:D_RG + (h + 1) * ML_HD] = (hm * mng_ref[:, sl]).astype(BF16)


def _smix(ptm, pbm, gi, gf, st, lw, l, *, name):
    ns = pbm.shape[0]
    rgh, rgc, mcv, mc_all, mn, mcol, mrow = st
    in_arrays = [ptm, pbm, gi, gf, rgh, rgc, mcv, mc_all, mn, mcol, mrow,
                 lw["rcw"], lw["rcb"], lw["wa"], lw["wx"], lw["rba"], lw["rbx"], lw["lam"],
                 lw["mcw"], lw["mcb"], lw["wq"], lw["wk"], lw["wv"], lw["bi_s"], lw["bf_s"],
                 lw["bi_c"], lw["bf_c"], lw["mng"]]
    in_specs = [_const_spec(a.shape) for a in in_arrays[:4]] + [_layer_spec(a, l) for a in in_arrays[4:]]
    in_specs[7] = pl.BlockSpec((1, SBLK, ML_HEADS, ML_HD, ML_HD), lambda i: (l, i, 0, 0, 0))
    out_shape = [
        jax.ShapeDtypeStruct((1, ns, D_MODEL), BF16),
        jax.ShapeDtypeStruct((ns, D_RG), F32),
        jax.ShapeDtypeStruct((CONV_W - 1, ns, D_RG), F32),
        jax.ShapeDtypeStruct((CONV_W - 1, ns, D_ML), F32),
        jax.ShapeDtypeStruct((ML_HEADS, ns, ML_HD), F32),
        jax.ShapeDtypeStruct((GATE_ROWS, ns), F32),
        jax.ShapeDtypeStruct((ns, GATE_ROWS), F32),
        jax.ShapeDtypeStruct((ML_HEADS, ns, ML_HD), F32),
        jax.ShapeDtypeStruct((ML_HEADS, ns, ML_HD), F32),
    ]
    out_specs = [pl.BlockSpec(s.shape, lambda i, n=len(s.shape): (0,) * n) for s in out_shape]
    scratch = [
        pltpu.VMEM((ML_HEADS, ns, ML_HD), F32),
        pltpu.VMEM((ML_HEADS, ns, ML_HD), F32),
        pltpu.VMEM((4, ns, GATE_ROWS), F32),
    ]
    return pl.pallas_call(
        _smix_body,
        grid=(ns // SBLK,),
        in_specs=in_specs,
        out_specs=out_specs,
        out_shape=out_shape,
        scratch_shapes=scratch,
        compiler_params=pltpu.CompilerParams(dimension_semantics=("arbitrary",), vmem_limit_bytes=VMEM_LIMIT),
        name=name,
    )(*in_arrays)


CBLK = 2 * SUBLANES


def _cupdate_body(c_ref, inter_ref, dk_ref, v_ref, o_ref):
    for h in range(ML_HEADS):
        z = jnp.concatenate([dk_ref[0, h], jnp.zeros((ML_HD - CBLK, ML_HD), F32)], axis=0).T
        inter_rows = jnp.broadcast_to(inter_ref[0, :, h:h + 1], (CBLK, LANES))
        vblk = v_ref[0, h]
        for j in range(CBLK):
            o_ref[0, j, h] = inter_rows[j:j + 1, :] * c_ref[0, j, h] + z[:, j:j + 1] * vblk[j:j + 1, :]


def _cupdate(mc_all, inter_all, dk_all, v_all):
    depth, ns = mc_all.shape[:2]
    cspec = pl.BlockSpec((1, CBLK, ML_HEADS, ML_HD, ML_HD), lambda l, i: (l, i, 0, 0, 0))
    kvspec = pl.BlockSpec((1, ML_HEADS, CBLK, ML_HD), lambda l, i: (l, 0, i, 0))
    return pl.pallas_call(
        _cupdate_body,
        grid=(depth, ns // CBLK),
        in_specs=[cspec, pl.BlockSpec((1, CBLK, GATE_ROWS), lambda l, i: (l, i, 0)), kvspec, kvspec],
        out_specs=cspec,
        out_shape=jax.ShapeDtypeStruct(mc_all.shape, F32),
        compiler_params=pltpu.CompilerParams(dimension_semantics=("arbitrary", "arbitrary"),
                                             vmem_limit_bytes=VMEM_LIMIT),
        name="cupdate_s",
    )(mc_all, inter_all, dk_all, v_all)


def _block_diag_pairs(w):
    depth = w.shape[0]
    w = w.reshape(depth, N_GROUPS, 2, RG_BLOCK, RG_BLOCK)
    zero = jnp.zeros((depth, N_GROUPS, RG_BLOCK, RG_BLOCK), w.dtype)
    top = jnp.concatenate([w[:, :, 0], zero], axis=-1)
    bottom = jnp.concatenate([zero, w[:, :, 1]], axis=-1)
    return jnp.concatenate([top, bottom], axis=-2)


def _gate_rows(w_cols):
    depth = w_cols.shape[0]
    return jnp.concatenate([jnp.swapaxes(w_cols, 1, 2),
                            jnp.zeros((depth, GATE_ROWS - ML_HEADS, D_MODEL), w_cols.dtype)], axis=1)


def _stacked_weights(ln1_g, w_in, rg_conv_w, rg_conv_b, rg_w_a, rg_w_x, rg_b_a, rg_b_x, rg_lambda,
                     m_conv_w, m_conv_b, m_w_q, m_w_k, m_w_v, m_b_i, m_b_f, m_norm_g, w_out, ln2_g,
                     w_ff1, w_ff2, nb_prompt, ns):
    depth = ln1_g.shape[0]
    row = lambda a: a.reshape(depth, 1, -1)
    pad8 = lambda a: jnp.concatenate([a, jnp.zeros((depth, GATE_ROWS - ML_HEADS), F32)], axis=1)
    bias8, bfor8 = pad8(m_b_i), pad8(m_b_f)
    lanes = lambda a: jnp.concatenate([jnp.tile(a, (1, nb_prompt)),
                                       jnp.zeros((depth, LANES - nb_prompt * GATE_ROWS), F32)], axis=1)[:, None, :]
    return dict(
        ln1=row(ln1_g),
        w_in=w_in.astype(BF16),
        wgi=_gate_rows(w_in[:, :, D_MAIN:D_MAIN + ML_HEADS]).astype(BF16),
        wgf=_gate_rows(w_in[:, :, D_MAIN + ML_HEADS:]).astype(BF16),
        rcw=rg_conv_w, rcb=row(rg_conv_b),
        wa=_block_diag_pairs(rg_w_a).astype(BF16), wx=_block_diag_pairs(rg_w_x).astype(BF16),
        rba=row(rg_b_a), rbx=row(rg_b_x), lam=row(rg_lambda),
        mcw=m_conv_w, mcb=row(m_conv_b),
        wq=m_w_q.astype(BF16), wk=m_w_k.astype(BF16), wkt=jnp.swapaxes(m_w_k, 2, 3).astype(BF16),
        wv=m_w_v.astype(BF16),
        bi=lanes(bias8), bf=lanes(bfor8),
        bi_s=jnp.broadcast_to(bias8[:, :, None], (depth, GATE_ROWS, ns)),
        bf_s=jnp.broadcast_to(bfor8[:, :, None], (depth, GATE_ROWS, ns)),
        bi_c=bias8.reshape(depth, 1, GATE_ROWS), bf_c=bfor8.reshape(depth, 1, GATE_ROWS),
        mng=row(m_norm_g),
        w_out=w_out.astype(BF16), ln2=row(ln2_g),
        w_ff1=w_ff1.astype(BF16), w_ff2=w_ff2.astype(BF16),
    )


def kernel(x_prompt, x_sample, state_rg_h, state_rg_conv, state_m_conv, state_m_C, state_m_n, state_m_m,
           meta_tokens, ln1_g, w_in, rg_conv_w, rg_conv_b, rg_w_a, rg_w_x, rg_b_a, rg_b_x, rg_lambda,
           m_conv_w, m_conv_b, m_w_q, m_w_k, m_w_v, m_b_i, m_b_f, m_norm_g, w_out, ln2_g,
           w_ff1, w_ff2, ln_f_g):
    nb, seq, d = x_prompt.shape
    ns = x_sample.shape[0]
    depth = ln1_g.shape[0]
    t_real = N_META + seq
    nc = -(-t_real // CHUNK)
    pad = nc * CHUNK - t_real
    assert (pad + N_META) % CHUNK == 0 and x_sample.shape[1] == 1 and ns == LANES and depth == DEPTH
    assert CHUNK == LANES and nb * GATE_ROWS <= LANES

    lw = _stacked_weights(ln1_g, w_in, rg_conv_w, rg_conv_b, rg_w_a, rg_w_x, rg_b_a, rg_b_x, rg_lambda,
                          m_conv_w, m_conv_b, m_w_q, m_w_k, m_w_v, m_b_i, m_b_f, m_norm_g, w_out, ln2_g,
                          w_ff1, w_ff2, nb, ns)
    lnf = ln_f_g.reshape(1, d)
    xp = jnp.concatenate([jnp.zeros((nb, pad, d), F32),
                          jnp.broadcast_to(meta_tokens.astype(F32)[None], (nb, N_META, d)),
                          x_prompt], axis=1)
    xs = x_sample.reshape(1, ns, d)
    m_cols = jnp.concatenate([state_m_m, jnp.zeros((depth, ns, GATE_ROWS - ML_HEADS), F32)], axis=2)
    st_in = (state_rg_h, jnp.swapaxes(state_rg_conv, 1, 2), jnp.swapaxes(state_m_conv, 1, 2), state_m_C,
             jnp.swapaxes(state_m_n, 1, 2), m_cols, jnp.swapaxes(m_cols, 1, 2))

    p_states = [[] for _ in range(6)]
    s_states = [[] for _ in range(6)]
    for l in range(depth):
        last = l == depth - 1

        y, rgh, rgc, mcv, st, mrow = _pmix(xp, lw, l, pad=pad, name=f"mixer_p{l}")
        lt_out = CHUNK // 2
        skip = (pad + N_META) // lt_out if last else 0
        xp = _outmlp(xp, y, lw, l, lnf, lt=lt_out, final=last, skip=skip,
                     nzero=0 if last else pad // lt_out, name=f"outmlp_p{l}")
        st = st.reshape(nb, ML_HEADS, ML_HD, S_COLS)
        p_states[0].append(rgh)
        p_states[1].append(jnp.swapaxes(rgc.reshape(CONV_W - 1, nb, D_RG), 0, 1))
        p_states[2].append(jnp.swapaxes(mcv.reshape(CONV_W - 1, nb, D_ML), 0, 1))
        p_states[3].append(st[..., :ML_HD])
        p_states[4].append(st[..., ML_HD])
        p_states[5].append(mrow[0, :nb * GATE_ROWS].reshape(nb, GATE_ROWS)[:, :ML_HEADS])

        stm, sbm, gis, gfs = _inproj(xs, lw, l, lt=ns, name=f"inproj_s{l}")
        ys, srgh, srgc, smcv, smn, smrow, sinter, sdk, sv = _smix(stm, sbm, gis, gfs, st_in, lw, l,
                                                                  name=f"mixer_s{l}")
        xs = _outmlp(xs, ys, lw, l, lnf, lt=ns, final=last, skip=0, nzero=0, name=f"outmlp_s{l}")
        s_states[0].append(srgh)
        s_states[1].append(jnp.swapaxes(srgc, 0, 1))
        s_states[2].append(jnp.swapaxes(smcv, 0, 1))
        s_states[3].append((sinter, sdk, sv))
        s_states[4].append(jnp.swapaxes(smn, 0, 1))
        s_states[5].append(smrow[:ML_HEADS].T)

    y_prompt = xp
    y_sample = xs.reshape(ns, 1, d)
    ps_out = [jnp.stack(s) for s in p_states]
    c_new = _cupdate(state_m_C, *(jnp.stack([t[j] for t in s_states[3]]) for j in range(3)))
    ss_out = [c_new if j == 3 else jnp.stack(s) for j, s in enumerate(s_states)]
    return (y_prompt, y_sample, *ps_out, *ss_out)
```

```python
import functools

import jax
import jax.numpy as jnp
from jax import lax
from jax.experimental import pallas as pl
from jax.experimental.pallas import tpu as pltpu

F32 = jnp.float32
BF16 = jnp.bfloat16

D_MODEL = 1024
N_META = 16
D_RG = 512
RG_BLOCK = 64
RG_C = 8.0
D_ML = 512
ML_HEADS = 4
ML_HD = 128
CONV_W = 4
D_FF = 4096
EPS = 1e-6
DEPTH = 4

LANES = 128
SUBLANES = 8
CHUNK = 128
PITCH = CHUNK + SUBLANES
N_GROUPS = D_RG // LANES
D_MAIN = 2 * D_RG + 2 * D_ML
GATE_ROWS = SUBLANES
S_COLS = 2 * ML_HD
N_SLABS = 3 * N_GROUPS
FF_TILE = 1024
VMEM_LIMIT = 56 * 1024 * 1024

_NT = (((1,), (1,)), ((), ()))


def _const_spec(shape):
    zeros = (0,) * len(shape)
    return pl.BlockSpec(shape, lambda *_: zeros, pipeline_mode=pl.Buffered(1))


def _layer_spec(arr, l):
    tail = (0,) * (arr.ndim - 1)
    return pl.BlockSpec((None,) + arr.shape[1:], lambda *_: (l,) + tail, pipeline_mode=pl.Buffered(1))


def _rms(x, g):
    return x * lax.rsqrt(jnp.mean(x * x, axis=-1, keepdims=True) + EPS) * g


def _sigmoid(x):
    return 0.5 * jnp.tanh(0.5 * x) + 0.5


def _sqrt_nonneg(s):
    return jnp.where(s > 0.0, s * lax.rsqrt(s), 0.0)


def _project(u, w_ref, n, dst_ref, base, nb, lt):
    pr = jnp.dot(u, w_ref[:, n * 512:(n + 1) * 512], preferred_element_type=F32)
    if n == D_MAIN // 512 - 1:
        dst_ref[...] = pr
        return
    for g in range(N_GROUPS):
        cols = slice(g * LANES, (g + 1) * LANES)
        if nb == 1:
            dst_ref[base + g] = pr[:, cols]
            continue
        for b in range(nb):
            dst_ref[base + g, pl.ds(b, lt, stride=nb), :] = pr[b * lt:(b + 1) * lt, cols]


def _gate_preacts(u, wg_ref, nb, lt):
    return jnp.concatenate([lax.dot_general(wg_ref[...], u[b * lt:(b + 1) * lt], _NT, preferred_element_type=F32)
                            for b in range(nb)], axis=0)


def _inproj_body(x_ref, g_ref, w_ref, wgi_ref, wgf_ref, ptm_ref, pbm_ref, gi_ref, gf_ref):
    nb, lt, d = x_ref.shape
    x = x_ref[...].reshape(nb * lt, d)
    u = _rms(x, g_ref[...]).astype(BF16)
    for n in range(D_MAIN // 512 - 1):
        _project(u, w_ref, n, ptm_ref, n * N_GROUPS, nb, lt)
    _project(u, w_ref, D_MAIN // 512 - 1, pbm_ref, 0, nb, lt)
    gi_ref[...] = _gate_preacts(u, wgi_ref, nb, lt)
    gf_ref[...] = _gate_preacts(u, wgf_ref, nb, lt)


def _inproj(x, lw, l, *, lt, name):
    nb, t, d = x.shape
    nc = t // lt
    rows = nb * lt
    return pl.pallas_call(
        _inproj_body,
        grid=(nc,),
        in_specs=[
            pl.BlockSpec((nb, lt, d), lambda c: (0, c, 0)),
            _layer_spec(lw["ln1"], l),
            pl.BlockSpec((None, d, D_MAIN), lambda c: (l, 0, 0), pipeline_mode=pl.Buffered(1)),
            _layer_spec(lw["wgi"], l),
            _layer_spec(lw["wgf"], l),
        ],
        out_specs=[
            pl.BlockSpec((N_SLABS, rows, LANES), lambda c: (0, c, 0)),
            pl.BlockSpec((rows, D_ML), lambda c: (c, 0)),
            pl.BlockSpec((nb * GATE_ROWS, lt), lambda c: (c, 0)),
            pl.BlockSpec((nb * GATE_ROWS, lt), lambda c: (c, 0)),
        ],
        out_shape=[
            jax.ShapeDtypeStruct((N_SLABS, nc * rows, LANES), F32),
            jax.ShapeDtypeStruct((nc * rows, D_ML), F32),
            jax.ShapeDtypeStruct((nc * nb * GATE_ROWS, lt), F32),
            jax.ShapeDtypeStruct((nc * nb * GATE_ROWS, lt), F32),
        ],
        compiler_params=pltpu.CompilerParams(dimension_semantics=("arbitrary",), vmem_limit_bytes=VMEM_LIMIT),
        name=name,
    )(x, lw["ln1"], lw["w_in"], lw["wgi"], lw["wgf"])


def _outmlp_body(x_ref, y_ref, wo_ref, g2_ref, w1_ref, w2_ref, gf_ref, o_ref, *, final, nzero):
    nb, lt, d = x_ref.shape
    rows = nb * lt

    def compute():
        x = x_ref[...].reshape(rows, d)
        y = y_ref[...].reshape(rows, d)
        x1 = x + jnp.dot(y, wo_ref[...], preferred_element_type=F32)
        u2 = _rms(x1, g2_ref[...]).astype(BF16)
        acc = x1
        for c in range(D_FF // FF_TILE):
            cols = slice(c * FF_TILE, (c + 1) * FF_TILE)
            h = jnp.dot(u2, w1_ref[:, cols], preferred_element_type=F32)
            h = jnp.square(jnp.maximum(h, 0.0)).astype(BF16)
            acc = acc + jnp.dot(h, w2_ref[cols, :], preferred_element_type=F32)
        if final:
            acc = _rms(acc, gf_ref[...])
        o_ref[...] = acc.reshape(nb, lt, d)

    def zero():
        o_ref[...] = jnp.zeros(o_ref.shape, F32)

    if nzero:
        i = pl.program_id(0)
        pl.when(i < nzero)(zero)
        pl.when(i >= nzero)(compute)
    else:
        compute()


def _outmlp(x, y, lw, l, gf, *, lt, final, skip, nzero, name):
    nb, t, d = x.shape
    nsteps = t // lt - skip
    return pl.pallas_call(
        functools.partial(_outmlp_body, final=final, nzero=nzero),
        grid=(nsteps,),
        in_specs=[
            pl.BlockSpec((nb, lt, d), lambda i: (0, i + skip, 0)),
            pl.BlockSpec((nb, lt, d), lambda i: (0, i + skip, 0)),
            _layer_spec(lw["w_out"], l),
            _layer_spec(lw["ln2"], l),
            _layer_spec(lw["w_ff1"], l),
            _layer_spec(lw["w_ff2"], l),
            _const_spec((1, d)),
        ],
        out_specs=pl.BlockSpec((nb, lt, d), lambda i: (0, i, 0)),
        out_shape=jax.ShapeDtypeStruct((nb, nsteps * lt, d), F32),
        compiler_params=pltpu.CompilerParams(dimension_semantics=("arbitrary",), vmem_limit_bytes=VMEM_LIMIT),
        name=name,
    )(x, y, lw["w_out"], lw["ln2"], lw["w_ff1"], lw["w_ff2"], gf)


def _rg_gates(xc, wa_ref, wx_ref, rba_ref, rbx_ref, lam_ref, g):
    sl = slice(g * LANES, (g + 1) * LANES)
    xg = xc[:, sl]
    xb = xg.astype(BF16)
    r = _sigmoid(jnp.dot(xb, wa_ref[g], preferred_element_type=F32) + rba_ref[:, sl])
    i = _sigmoid(jnp.dot(xb, wx_ref[g], preferred_element_type=F32) + rbx_ref[:, sl])
    log_a = (RG_C * r) * jax.nn.log_sigmoid(lam_ref[:, sl])
    a = jnp.exp(log_a)
    gx = _sqrt_nonneg(1.0 - a * a) * (i * xg)
    return a, gx


def _row_scan(x, op, fill):
    row = lax.broadcasted_iota(jnp.int32, x.shape, 0)
    sh = 1
    while sh < x.shape[0]:
        x = op(x, jnp.where(row >= sh, pltpu.roll(x, sh, 0), fill))
        sh *= 2
    return x


def _slabs(ptm_ref, first):
    return jnp.concatenate([ptm_ref[first + g] for g in range(N_GROUPS)], axis=1)


def _pmix_body(x_ref, g1_ref, w_ref, wgi_ref, wgf_ref,
               rcw_ref, rcb_ref, wa_ref, wx_ref, rba_ref, rbx_ref, lam_ref,
               mcw_ref, mcb_ref, wq_ref, wkt_ref, wv_ref, bi_ref, bf_ref, mng_ref,
               y_ref, rgh_ref, rgc_ref, mcv_ref, st_ref, m_ref,
               px_s, pg_s, pm_s, po_s, a_s, g_s, bmc_s, bmx_s, yb_s, q_s, kt_s, v_s, ho_s, col_s, row_s, *, pad):
    nb = y_ref.shape[0]
    lt = CHUNK
    rows = nb * lt
    ntail = (CONV_W - 1) * nb
    c = pl.program_id(0)

    @pl.when(c == 0)
    def _init():
        rgh_ref[...] = jnp.zeros_like(rgh_ref)
        rgc_ref[...] = jnp.zeros_like(rgc_ref)
        mcv_ref[...] = jnp.zeros_like(mcv_ref)
        st_ref[...] = jnp.zeros_like(st_ref)
        m_ref[...] = jnp.zeros_like(m_ref)
        ones_col = (lax.broadcasted_iota(jnp.int32, (rows, ML_HD), 1) == 0).astype(BF16)
        for h in range(ML_HEADS):
            v_s[h, :, ML_HD:S_COLS] = ones_col

    def conv(x, w_ref, b_ref, tail_ref):
        tail = tail_ref[...]
        acc = b_ref[...] + w_ref[CONV_W - 1:CONV_W, :] * x
        for k in range(1, CONV_W):
            shifted = jnp.concatenate([tail[ntail - k * nb:], x[:rows - k * nb]], axis=0)
            acc = acc + w_ref[CONV_W - 1 - k:CONV_W - k, :] * shifted
        tail_ref[...] = x[rows - ntail:]
        return acc

    u = _rms(x_ref[...].reshape(rows, x_ref.shape[2]), g1_ref[...]).astype(BF16)
    _project(u, w_ref, 2, pm_s, 0, nb, lt)
    gi = _gate_preacts(u, wgi_ref, nb, lt)
    gf = _gate_preacts(u, wgf_ref, nb, lt)

    mx = _slabs(pm_s, 0)
    mconv = conv(mx, mcw_ref, mcb_ref, mcv_ref)
    mc = mconv * _sigmoid(mconv)
    _project(u, w_ref, 0, px_s, 0, nb, lt)
    for t in range(lt):
        rs = slice(t * nb, (t + 1) * nb)
        idx = pl.ds(t, nb, stride=PITCH)
        for g in range(N_GROUPS):
            cols = slice(g * LANES, (g + 1) * LANES)
            bmc_s[g, idx, :] = mc[rs, cols]
            bmx_s[g, idx, :] = mx[rs, cols]

    def seq_major(src):
        return jnp.concatenate(
            [jnp.concatenate([src[g, b * PITCH:b * PITCH + lt, :] for b in range(nb)], axis=0)
             for g in range(N_GROUPS)], axis=1).astype(BF16)

    mcb = seq_major(bmc_s)
    mxb = seq_major(bmx_s)
    for h in range(ML_HEADS):
        sl = slice(h * ML_HD, (h + 1) * ML_HD)
        q_s[h] = jnp.dot(mcb[:, sl], wq_ref[h], preferred_element_type=F32) * (ML_HD ** -0.5)
        kt = lax.dot_general(wkt_ref[h], mcb[:, sl], _NT, preferred_element_type=F32)
        for b in range(nb):
            kt_s[h, b] = kt[:, b * lt:(b + 1) * lt]
        v_s[h, :, 0:ML_HD] = jnp.dot(mxb[:, sl], wv_ref[h], preferred_element_type=F32).astype(BF16)

    def to_cols(r):
        return jnp.concatenate([r, jnp.zeros((LANES - r.shape[0], lt), F32)], axis=0).T

    ig = to_cols(gi) + bi_ref[...]
    lf = jax.nn.log_sigmoid(to_cols(gf) + bf_ref[...])
    if pad:
        trow = lax.broadcasted_iota(jnp.int32, ig.shape, 0)
        is_pad = trow < jnp.where(c == 0, pad, 0)
        ig = jnp.where(is_pad, -1e30, ig)
        lf = jnp.where(is_pad, 0.0, lf)
    bcs = _row_scan(lf, jnp.add, 0.0)
    gg = ig - bcs
    cm = _row_scan(gg, jnp.maximum, -jnp.inf)
    m0 = m_ref[0:1, :]
    mm = jnp.maximum(m0, cm)
    mt = bcs + mm
    b_last = bcs[lt - 1:lt, :]
    m_last = mt[lt - 1:lt, :]
    col_s[0] = mm
    col_s[1] = jnp.exp(m0 - mm)
    col_s[2] = jnp.exp(-mt)
    row_s[0] = gg.T
    row_s[1] = jnp.exp(b_last - m_last + gg).T
    row_s[2] = jnp.broadcast_to(jnp.exp(b_last + m0 - m_last), (lt, LANES)).T
    m_ref[...] = jnp.broadcast_to(m_last, m_ref.shape)

    rg = {}

    def rg_conv():
        rg["xc"] = conv(_slabs(px_s, 0), rcw_ref, rcb_ref, rgc_ref)
        if pad:
            rowi = lax.broadcasted_iota(jnp.int32, (rows, 1), 0)
            rg["keep"] = rowi >= jnp.where(c == 0, pad * nb, 0)

    def rg_gates(g):
        a, gx = _rg_gates(rg["xc"], wa_ref, wx_ref, rba_ref, rbx_ref, lam_ref, g)
        if pad:
            gx = jnp.where(rg["keep"], gx, 0.0)
        a_s[g] = a
        g_s[g] = gx

    def rg_scan(g):
        sl = slice(g * LANES, (g + 1) * LANES)
        gate = jax.nn.gelu(pg_s[g])
        h = rgh_ref[:, sl]
        for t in range(lt):
            rs = slice(t * nb, (t + 1) * nb)
            h = a_s[g, rs, :] * h + g_s[g, rs, :]
            yb_s[g, pl.ds(t, nb, stride=PITCH), :] = h * gate[rs]
        rgh_ref[:, sl] = h

    def rg_out(g):
        for b in range(nb):
            y_ref[b, :, g * LANES:(g + 1) * LANES] = yb_s[g, b * PITCH:b * PITCH + lt, :].astype(BF16)

    def gate_out(b):
        trows = slice(b * lt, (b + 1) * lt)
        for h in range(ML_HEADS):
            sl = slice(h * ML_HD, (h + 1) * ML_HD)
            hm = _sigmoid(po_s[trows, sl]) * ho_s[b * ML_HEADS + h]
            hm = hm * lax.rsqrt(jnp.mean(hm * hm, axis=-1, keepdims=True) + EPS)
            y_ref[b, :, D_RG + h * ML_HD:D_RG + (h + 1) * ML_HD] = (hm * mng_ref[:, sl]).astype(BF16)

    pieces = [[] for _ in range(nb + 1)]
    pieces[0].append(rg_conv)
    for g in range(N_GROUPS):
        pieces[g].append(functools.partial(rg_gates, g))
        pieces[g + 1].append(functools.partial(rg_scan, g))
        pieces[g + 2].append(functools.partial(rg_out, g))
    pieces[0].append(functools.partial(_project, u, w_ref, 1, pg_s, 0, nb, lt))
    pieces[1].append(functools.partial(_project, u, w_ref, 3, po_s, 0, nb, lt))
    for b in range(nb):
        pieces[max(b + 1, 2)].append(functools.partial(gate_out, b))

    t_i = lax.broadcasted_iota(jnp.int32, (lt, lt), 0)
    s_i = lax.broadcasted_iota(jnp.int32, (lt, lt), 1)
    causal = s_i <= t_i

    for b in range(nb):
        for piece in pieces[b]:
            piece()
        trows = slice(b * lt, (b + 1) * lt)
        heads = range(ML_HEADS)
        qb = [q_s[h, trows, :].astype(BF16) for h in heads]
        kt = [kt_s[h, b] for h in heads]
        s0 = [st_ref[b * ML_HEADS + h] for h in heads]
        sc = [jnp.dot(qb[h], kt[h].astype(BF16), preferred_element_type=F32) for h in heads]
        qs = [jnp.dot(qb[h], s0[h].astype(BF16), preferred_element_type=F32) for h in heads]
        pm = []
        for h in heads:
            r = b * GATE_ROWS + h
            d = jnp.exp(jnp.where(causal, row_s[0, r:r + 1, :] - col_s[0, :, r:r + 1], -jnp.inf))
            pm.append((sc[h] * d).astype(BF16))
        pv = [jnp.dot(pm[h], v_s[h, trows, :], preferred_element_type=F32) for h in heads]
        for h in heads:
            r = b * GATE_ROWS + h
            numx = pv[h] + col_s[1, :, r:r + 1] * qs[h]
            den = numx[:, ML_HD:ML_HD + 1]
            ho_s[b * ML_HEADS + h] = numx[:, 0:ML_HD] / jnp.maximum(jnp.abs(den), col_s[2, :, r:r + 1])
        for h in heads:
            r = b * GATE_ROWS + h
            wkt = (kt[h] * row_s[1, r:r + 1, :]).astype(BF16)
            sc_row = jnp.concatenate([row_s[2, r:r + 1, :], row_s[2, r:r + 1, :]], axis=1)
            st_ref[b * ML_HEADS + h] = sc_row * s0[h] + jnp.dot(wkt, v_s[h, trows, :], preferred_element_type=F32)
    for piece in pieces[nb]:
        piece()

    if pad:
        @pl.when(c == 0)
        def _zero_pad_rows():
            y_ref[:, 0:pad, :] = jnp.zeros((nb, pad, y_ref.shape[2]), BF16)


def _pmix(x, lw, l, *, pad, name):
    nb, t, d = x.shape
    nc = t // CHUNK
    rows = nb * CHUNK
    ntail = (CONV_W - 1) * nb
    in_arrays = [x, lw["ln1"], lw["w_in"], lw["wgi"], lw["wgf"],
                 lw["rcw"], lw["rcb"], lw["wa"], lw["wx"], lw["rba"], lw["rbx"], lw["lam"],
                 lw["mcw"], lw["mcb"], lw["wq"], lw["wkt"], lw["wv"], lw["bi"], lw["bf"], lw["mng"]]
    in_specs = [pl.BlockSpec((nb, CHUNK, d), lambda c: (0, c, 0))] + [_layer_spec(a, l) for a in in_arrays[1:]]
    in_specs[2] = pl.BlockSpec((None, d, D_MAIN), lambda c: (l, 0, 0), pipeline_mode=pl.Buffered(1))
    out_shape = [
        jax.ShapeDtypeStruct((nb, nc * CHUNK, D_MODEL), BF16),
        jax.ShapeDtypeStruct((nb, D_RG), F32),
        jax.ShapeDtypeStruct((ntail, D_RG), F32),
        jax.ShapeDtypeStruct((ntail, D_ML), F32),
        jax.ShapeDtypeStruct((nb * ML_HEADS, ML_HD, S_COLS), F32),
        jax.ShapeDtypeStruct((SUBLANES, LANES), F32),
    ]
    out_specs = [pl.BlockSpec((nb, CHUNK, D_MODEL), lambda c: (0, c, 0))] + [
        pl.BlockSpec(s.shape, lambda c, n=len(s.shape): (0,) * n) for s in out_shape[1:]]
    scratch = [
        pltpu.VMEM((N_GROUPS, rows, LANES), F32),
        pltpu.VMEM((N_GROUPS, rows, LANES), F32),
        pltpu.VMEM((N_GROUPS, rows, LANES), F32),
        pltpu.VMEM((rows, D_ML), F32),
        pltpu.VMEM((N_GROUPS, rows, LANES), F32),
        pltpu.VMEM((N_GROUPS, rows, LANES), F32),
        pltpu.VMEM((N_GROUPS, nb * PITCH, LANES), F32),
        pltpu.VMEM((N_GROUPS, nb * PITCH, LANES), F32),
        pltpu.VMEM((N_GROUPS, nb * PITCH, LANES), F32),
        pltpu.VMEM((ML_HEADS, rows, ML_HD), F32),
        pltpu.VMEM((ML_HEADS, nb, ML_HD, CHUNK), F32),
        pltpu.VMEM((ML_HEADS, rows, S_COLS), BF16),
        pltpu.VMEM((nb * ML_HEADS, CHUNK, ML_HD), F32),
        pltpu.VMEM((3, CHUNK, LANES), F32),
        pltpu.VMEM((3, LANES, CHUNK), F32),
    ]
    return pl.pallas_call(
        functools.partial(_pmix_body, pad=pad),
        grid=(nc,),
        in_specs=in_specs,
        out_specs=out_specs,
        out_shape=out_shape,
        scratch_shapes=scratch,
        compiler_params=pltpu.CompilerParams(dimension_semantics=("arbitrary",), vmem_limit_bytes=VMEM_LIMIT),
        name=name,
    )(*in_arrays)


SBLK = SUBLANES


def _smix_body(ptm_ref, pbm_ref, gi_ref, gf_ref, rgh_ref, rgc_ref, mcv_ref, c_ref, n_ref, mcol_ref, mrow_ref,
               rcw_ref, rcb_ref, wa_ref, wx_ref, rba_ref, rbx_ref, lam_ref,
               mcw_ref, mcb_ref, wq_ref, wk_ref, wv_ref, bi_ref, bf_ref, bic_ref, bfc_ref, mng_ref,
               y_ref, rgh_o, rgc_o, mcv_o, n_o, m_o, inter_o, dk_o, v_o,
               q_s, qc_s, col_s):
    i = pl.program_id(0)
    ns = pbm_ref.shape[0]

    def conv(first, w_ref, b_ref, tail_ref, tail_o):
        x = _slabs(ptm_ref, first)
        acc = b_ref[...] + w_ref[CONV_W - 1:CONV_W, :] * x
        for j in range(CONV_W - 1):
            acc = acc + w_ref[j:j + 1, :] * tail_ref[j]
        for j in range(CONV_W - 2):
            tail_o[j] = tail_ref[j + 1]
        tail_o[CONV_W - 2] = x
        return acc, x

    @pl.when(i == 0)
    def _rowwise():
        xc, _ = conv(0, rcw_ref, rcb_ref, rgc_ref, rgc_o)
        for g in range(N_GROUPS):
            sl = slice(g * LANES, (g + 1) * LANES)
            a, gx = _rg_gates(xc, wa_ref, wx_ref, rba_ref, rbx_ref, lam_ref, g)
            hn = a * rgh_ref[:, sl] + gx
            rgh_o[:, sl] = hn
            y_ref[0, :, sl] = (hn * jax.nn.gelu(ptm_ref[N_GROUPS + g])).astype(BF16)

        mconv, mx = conv(2 * N_GROUPS, mcw_ref, mcb_ref, mcv_ref, mcv_o)
        mcb = (mconv * _sigmoid(mconv)).astype(BF16)
        mxb = mx.astype(BF16)

        ig_r = gi_ref[...] + bi_ref[...]
        lf_r = jax.nn.log_sigmoid(gf_ref[...] + bf_ref[...])
        m0_r = mrow_ref[...]
        m_o[...] = jnp.maximum(lf_r + m0_r, ig_r)
        z = jnp.concatenate([gi_ref[...], gf_ref[...], jnp.zeros((ns - 2 * GATE_ROWS, ns), F32)], axis=0).T
        ig_c = z[:, 0:GATE_ROWS] + bic_ref[...]
        lf_c = jax.nn.log_sigmoid(z[:, GATE_ROWS:2 * GATE_ROWS] + bfc_ref[...])
        m0_c = mcol_ref[...]
        m_c = jnp.maximum(lf_c + m0_c, ig_c)
        inter_c = jnp.exp(lf_c + m0_c - m_c)
        dd_c = jnp.exp(ig_c - m_c)
        inter_o[...] = inter_c
        col_s[0] = inter_c
        col_s[1] = jnp.exp(-m_c)
        for h in range(ML_HEADS):
            sl = slice(h * ML_HD, (h + 1) * ML_HD)
            q = jnp.dot(mcb[:, sl], wq_ref[h], preferred_element_type=F32) * (ML_HD ** -0.5)
            k = jnp.dot(mcb[:, sl], wk_ref[h], preferred_element_type=F32)
            v = jnp.dot(mxb[:, sl], wv_ref[h], preferred_element_type=F32)
            q_s[h] = q
            dk_o[h] = dd_c[:, h:h + 1] * k
            v_o[h] = v
            col_s[2, :, h:h + 1] = jnp.sum(q * k, axis=-1, keepdims=True) * dd_c[:, h:h + 1]
            n0 = n_ref[h]
            col_s[3, :, h:h + 1] = jnp.sum(q * n0, axis=-1, keepdims=True)
            n_o[h] = inter_c[:, h:h + 1] * n0 + dd_c[:, h:h + 1] * k

    blk = pl.ds(pl.multiple_of(i * SBLK, SBLK), SBLK)
    for h in range(ML_HEADS):
        z = jnp.concatenate([q_s[h, blk, :], jnp.zeros((ML_HD - SBLK, ML_HD), F32)], axis=0).T
        qc_s[h, blk, :] = jnp.concatenate(
            [jnp.sum(z[:, j:j + 1] * c_ref[0, j, h], axis=0, keepdims=True) for j in range(SBLK)], axis=0)

    @pl.when(i == pl.num_programs(0) - 1)
    def _finish():
        for h in range(ML_HEADS):
            sl = slice(h * ML_HD, (h + 1) * ML_HD)
            inter = col_s[0, :, h:h + 1]
            s = col_s[2, :, h:h + 1]
            num = s * v_o[h] + inter * qc_s[h]
            den = s + inter * col_s[3, :, h:h + 1]
            hout = num / jnp.maximum(jnp.abs(den), col_s[1, :, h:h + 1])
            hm = _sigmoid(pbm_ref[:, sl]) * hout
            hm = hm * lax.rsqrt(jnp.mean(hm * hm, axis=-1, keepdims=True) + EPS)
            y_ref[0, :, D_RG + h * ML_HD:D_RG + (h + 1) * ML_HD] = (hm * mng_ref[:, sl]).astype(BF16)


def _smix(ptm, pbm, gi, gf, st, lw, l, *, name):
    ns = pbm.shape[0]
    rgh, rgc, mcv, mc_all, mn, mcol, mrow = st
    in_arrays = [ptm, pbm, gi, gf, rgh, rgc, mcv, mc_all, mn, mcol, mrow,
                 lw["rcw"], lw["rcb"], lw["wa"], lw["wx"], lw["rba"], lw["rbx"], lw["lam"],
                 lw["mcw"], lw["mcb"], lw["wq"], lw["wk"], lw["wv"], lw["bi_s"], lw["bf_s"],
                 lw["bi_c"], lw["bf_c"], lw["mng"]]
    in_specs = [_const_spec(a.shape) for a in in_arrays[:4]] + [_layer_spec(a, l) for a in in_arrays[4:]]
    in_specs[7] = pl.BlockSpec((1, SBLK, ML_HEADS, ML_HD, ML_HD), lambda i: (l, i, 0, 0, 0))
    out_shape = [
        jax.ShapeDtypeStruct((1, ns, D_MODEL), BF16),
        jax.ShapeDtypeStruct((ns, D_RG), F32),
        jax.ShapeDtypeStruct((CONV_W - 1, ns, D_RG), F32),
        jax.ShapeDtypeStruct((CONV_W - 1, ns, D_ML), F32),
        jax.ShapeDtypeStruct((ML_HEADS, ns, ML_HD), F32),
        jax.ShapeDtypeStruct((GATE_ROWS, ns), F32),
        jax.ShapeDtypeStruct((ns, GATE_ROWS), F32),
        jax.ShapeDtypeStruct((ML_HEADS, ns, ML_HD), F32),
        jax.ShapeDtypeStruct((ML_HEADS, ns, ML_HD), F32),
    ]
    out_specs = [pl.BlockSpec(s.shape, lambda i, n=len(s.shape): (0,) * n) for s in out_shape]
    scratch = [
        pltpu.VMEM((ML_HEADS, ns, ML_HD), F32),
        pltpu.VMEM((ML_HEADS, ns, ML_HD), F32),
        pltpu.VMEM((4, ns, GATE_ROWS), F32),
    ]
    return pl.pallas_call(
        _smix_body,
        grid=(ns // SBLK,),
        in_specs=in_specs,
        out_specs=out_specs,
        out_shape=out_shape,
        scratch_shapes=scratch,
        compiler_params=pltpu.CompilerParams(dimension_semantics=("arbitrary",), vmem_limit_bytes=VMEM_LIMIT),
        name=name,
    )(*in_arrays)


CBLK = 2 * SUBLANES


def _cupdate_body(c_ref, inter_ref, dk_ref, v_ref, o_ref):
    for h in range(ML_HEADS):
        z = jnp.concatenate([dk_ref[0, h], jnp.zeros((ML_HD - CBLK, ML_HD), F32)], axis=0).T
        inter_rows = jnp.broadcast_to(inter_ref[0, :, h:h + 1], (CBLK, LANES))
        vblk = v_ref[0, h]
        for j in range(CBLK):
            o_ref[0, j, h] = inter_rows[j:j + 1, :] * c_ref[0, j, h] + z[:, j:j + 1] * vblk[j:j + 1, :]


def _cupdate(mc_all, inter_all, dk_all, v_all):
    depth, ns = mc_all.shape[:2]
    cspec = pl.BlockSpec((1, CBLK, ML_HEADS, ML_HD, ML_HD), lambda l, i: (l, i, 0, 0, 0))
    kvspec = pl.BlockSpec((1, ML_HEADS, CBLK, ML_HD), lambda l, i: (l, 0, i, 0))
    return pl.pallas_call(
        _cupdate_body,
        grid=(depth, ns // CBLK),
        in_specs=[cspec, pl.BlockSpec((1, CBLK, GATE_ROWS), lambda l, i: (l, i, 0)), kvspec, kvspec],
        out_specs=cspec,
        out_shape=jax.ShapeDtypeStruct(mc_all.shape, F32),
        compiler_params=pltpu.CompilerParams(dimension_semantics=("arbitrary", "arbitrary"),
                                             vmem_limit_bytes=VMEM_LIMIT),
        name="cupdate_s",
    )(mc_all, inter_all, dk_all, v_all)


def _block_diag_pairs(w):
    depth = w.shape[0]
    w = w.reshape(depth, N_GROUPS, 2, RG_BLOCK, RG_BLOCK)
    zero = jnp.zeros((depth, N_GROUPS, RG_BLOCK, RG_BLOCK), w.dtype)
    top = jnp.concatenate([w[:, :, 0], zero], axis=-1)
    bottom = jnp.concatenate([zero, w[:, :, 1]], axis=-1)
    return jnp.concatenate([top, bottom], axis=-2)


def _gate_rows(w_cols):
    depth = w_cols.shape[0]
    return jnp.concatenate([jnp.swapaxes(w_cols, 1, 2),
                            jnp.zeros((depth, GATE_ROWS - ML_HEADS, D_MODEL), w_cols.dtype)], axis=1)


def _stacked_weights(ln1_g, w_in, rg_conv_w, rg_conv_b, rg_w_a, rg_w_x, rg_b_a, rg_b_x, rg_lambda,
                     m_conv_w, m_conv_b, m_w_q, m_w_k, m_w_v, m_b_i, m_b_f, m_norm_g, w_out, ln2_g,
                     w_ff1, w_ff2, nb_prompt, ns):
    depth = ln1_g.shape[0]
    row = lambda a: a.reshape(depth, 1, -1)
    pad8 = lambda a: jnp.concatenate([a, jnp.zeros((depth, GATE_ROWS - ML_HEADS), F32)], axis=1)
    bias8, bfor8 = pad8(m_b_i), pad8(m_b_f)
    lanes = lambda a: jnp.concatenate([jnp.tile(a, (1, nb_prompt)),
                                       jnp.zeros((depth, LANES - nb_prompt * GATE_ROWS), F32)], axis=1)[:, None, :]
    return dict(
        ln1=row(ln1_g),
        w_in=w_in.astype(BF16),
        wgi=_gate_rows(w_in[:, :, D_MAIN:D_MAIN + ML_HEADS]).astype(BF16),
        wgf=_gate_rows(w_in[:, :, D_MAIN + ML_HEADS:]).astype(BF16),
        rcw=rg_conv_w, rcb=row(rg_conv_b),
        wa=_block_diag_pairs(rg_w_a).astype(BF16), wx=_block_diag_pairs(rg_w_x).astype(BF16),
        rba=row(rg_b_a), rbx=row(rg_b_x), lam=row(rg_lambda),
        mcw=m_conv_w, mcb=row(m_conv_b),
        wq=m_w_q.astype(BF16), wk=m_w_k.astype(BF16), wkt=jnp.swapaxes(m_w_k, 2, 3).astype(BF16),
        wv=m_w_v.astype(BF16),
        bi=lanes(bias8), bf=lanes(bfor8),
        bi_s=jnp.broadcast_to(bias8[:, :, None], (depth, GATE_ROWS, ns)),
        bf_s=jnp.broadcast_to(bfor8[:, :, None], (depth, GATE_ROWS, ns)),
        bi_c=bias8.reshape(depth, 1, GATE_ROWS), bf_c=bfor8.reshape(depth, 1, GATE_ROWS),
        mng=row(m_norm_g),
        w_out=w_out.astype(BF16), ln2=row(ln2_g),
        w_ff1=w_ff1.astype(BF16), w_ff2=w_ff2.astype(BF16),
    )


def kernel(x_prompt, x_sample, state_rg_h, state_rg_conv, state_m_conv, state_m_C, state_m_n, state_m_m,
           meta_tokens, ln1_g, w_in, rg_conv_w, rg_conv_b, rg_w_a, rg_w_x, rg_b_a, rg_b_x, rg_lambda,
           m_conv_w, m_conv_b, m_w_q, m_w_k, m_w_v, m_b_i, m_b_f, m_norm_g, w_out, ln2_g,
           w_ff1, w_ff2, ln_f_g):
    nb, seq, d = x_prompt.shape
    ns = x_sample.shape[0]
    depth = ln1_g.shape[0]
    t_real = N_META + seq
    nc = -(-t_real // CHUNK)
    pad = nc * CHUNK - t_real
    assert (pad + N_META) % CHUNK == 0 and x_sample.shape[1] == 1 and ns == LANES and depth == DEPTH
    assert CHUNK == LANES and nb * GATE_ROWS <= LANES

    lw = _stacked_weights(ln1_g, w_in, rg_conv_w, rg_conv_b, rg_w_a, rg_w_x, rg_b_a, rg_b_x, rg_lambda,
                          m_conv_w, m_conv_b, m_w_q, m_w_k, m_w_v, m_b_i, m_b_f, m_norm_g, w_out, ln2_g,
                          w_ff1, w_ff2, nb, ns)
    lnf = ln_f_g.reshape(1, d)
    xp = jnp.concatenate([jnp.zeros((nb, pad, d), F32),
                          jnp.broadcast_to(meta_tokens.astype(F32)[None], (nb, N_META, d)),
                          x_prompt], axis=1)
    xs = x_sample.reshape(1, ns, d)
    m_cols = jnp.concatenate([state_m_m, jnp.zeros((depth, ns, GATE_ROWS - ML_HEADS), F32)], axis=2)
    st_in = (state_rg_h, jnp.swapaxes(state_rg_conv, 1, 2), jnp.swapaxes(state_m_conv, 1, 2), state_m_C,
             jnp.swapaxes(state_m_n, 1, 2), m_cols, jnp.swapaxes(m_cols, 1, 2))

    p_states = [[] for _ in range(6)]
    s_states = [[] for _ in range(6)]
    for l in range(depth):
        last = l == depth - 1

        y, rgh, rgc, mcv, st, mrow = _pmix(xp, lw, l, pad=pad, name=f"mixer_p{l}")
        lt_out = CHUNK // 2
        skip = (pad + N_META) // lt_out if last else 0
        xp = _outmlp(xp, y, lw, l, lnf, lt=lt_out, final=last, skip=skip,
                     nzero=0 if last else pad // lt_out, name=f"outmlp_p{l}")
        st = st.reshape(nb, ML_HEADS, ML_HD, S_COLS)
        p_states[0].append(rgh)
        p_states[1].append(jnp.swapaxes(rgc.reshape(CONV_W - 1, nb, D_RG), 0, 1))
        p_states[2].append(jnp.swapaxes(mcv.reshape(CONV_W - 1, nb, D_ML), 0, 1))
        p_states[3].append(st[..., :ML_HD])
        p_states[4].append(st[..., ML_HD])
        p_states[5].append(mrow[0, :nb * GATE_ROWS].reshape(nb, GATE_ROWS)[:, :ML_HEADS])

        stm, sbm, gis, gfs = _inproj(xs, lw, l, lt=ns, name=f"inproj_s{l}")
        ys, srgh, srgc, smcv, smn, smrow, sinter, sdk, sv = _smix(stm, sbm, gis, gfs, st_in, lw, l,
                                                                  name=f"mixer_s{l}")
        xs = _outmlp(xs, ys, lw, l, lnf, lt=ns, final=last, skip=0, nzero=0, name=f"outmlp_s{l}")
        s_states[0].append(srgh)
        s_states[1].append(jnp.swapaxes(srgc, 0, 1))
        s_states[2].append(jnp.swapaxes(smcv, 0, 1))
        s_states[3].append((sinter, sdk, sv))
        s_states[4].append(jnp.swapaxes(smn, 0, 1))
        s_states[5].append(smrow[:ML_HEADS].T)

    y_prompt = xp
    y_sample = xs.reshape(ns, 1, d)
    ps_out = [jnp.stack(s) for s in p_states]
    c_new = _cupdate(state_m_C, *(jnp.stack([t[j] for t in s_states[3]]) for j in range(3)))
    ss_out = [c_new if j == 3 else jnp.stack(s) for j, s in enumerate(s_states)]
    return (y_prompt, y_sample, *ps_out, *ss_out)
```

```python
import functools

import jax
import jax.numpy as jnp
from jax import lax
from jax.experimental import pallas as pl
from jax.experimental.pallas import tpu as pltpu

F32 = jnp.float32
BF16 = jnp.bfloat16

D_MODEL = 1024
N_META = 16
D_RG = 512
RG_BLOCK = 64
RG_C = 8.0
D_ML = 512
ML_HEADS = 4
ML_HD = 128
CONV_W = 4
D_FF = 4096
EPS = 1e-6
DEPTH = 4

LANES = 128
SUBLANES = 8
CHUNK = 128
PITCH = CHUNK + SUBLANES
N_GROUPS = D_RG // LANES
D_MAIN = 2 * D_RG + 2 * D_ML
GATE_ROWS = SUBLANES
S_COLS = 2 * ML_HD
N_SLABS = 3 * N_GROUPS
FF_TILE = 1024
VMEM_LIMIT = 56 * 1024 * 1024

_NT = (((1,), (1,)), ((), ()))


def _const_spec(shape):
    zeros = (0,) * len(shape)
    return pl.BlockSpec(shape, lambda *_: zeros, pipeline_mode=pl.Buffered(1))


def _layer_spec(arr, l):
    tail = (0,) * (arr.ndim - 1)
    return pl.BlockSpec((None,) + arr.shape[1:], lambda *_: (l,) + tail, pipeline_mode=pl.Buffered(1))


def _rms(x, g):
    return x * lax.rsqrt(jnp.mean(x * x, axis=-1, keepdims=True) + EPS) * g


def _sigmoid(x):
    return 0.5 * jnp.tanh(0.5 * x) + 0.5


def _sqrt_nonneg(s):
    return jnp.where(s > 0.0, s * lax.rsqrt(s), 0.0)


def _project(u, w_ref, n, dst_ref, base, nb, lt):
    pr = jnp.dot(u, w_ref[:, n * 512:(n + 1) * 512], preferred_element_type=F32)
    if n == D_MAIN // 512 - 1:
        dst_ref[...] = pr
        return
    for g in range(N_GROUPS):
        cols = slice(g * LANES, (g + 1) * LANES)
        if nb == 1:
            dst_ref[base + g] = pr[:, cols]
            continue
        for b in range(nb):
            dst_ref[base + g, pl.ds(b, lt, stride=nb), :] = pr[b * lt:(b + 1) * lt, cols]


def _gate_preacts(u, wg_ref, nb, lt):
    return jnp.concatenate([lax.dot_general(wg_ref[...], u[b * lt:(b + 1) * lt], _NT, preferred_element_type=F32)
                            for b in range(nb)], axis=0)


def _inproj_body(x_ref, g_ref, w_ref, wgi_ref, wgf_ref, ptm_ref, pbm_ref, gi_ref, gf_ref):
    nb, lt, d = x_ref.shape
    x = x_ref[...].reshape(nb * lt, d)
    u = _rms(x, g_ref[...]).astype(BF16)
    for n in range(D_MAIN // 512 - 1):
        _project(u, w_ref, n, ptm_ref, n * N_GROUPS, nb, lt)
    _project(u, w_ref, D_MAIN // 512 - 1, pbm_ref, 0, nb, lt)
    gi_ref[...] = _gate_preacts(u, wgi_ref, nb, lt)
    gf_ref[...] = _gate_preacts(u, wgf_ref, nb, lt)


def _inproj(x, lw, l, *, lt, name):
    nb, t, d = x.shape
    nc = t // lt
    rows = nb * lt
    return pl.pallas_call(
        _inproj_body,
        grid=(nc,),
        in_specs=[
            pl.BlockSpec((nb, lt, d), lambda c: (0, c, 0)),
            _layer_spec(lw["ln1"], l),
            pl.BlockSpec((None, d, D_MAIN), lambda c: (l, 0, 0), pipeline_mode=pl.Buffered(1)),
            _layer_spec(lw["wgi"], l),
            _layer_spec(lw["wgf"], l),
        ],
        out_specs=[
            pl.BlockSpec((N_SLABS, rows, LANES), lambda c: (0, c, 0)),
            pl.BlockSpec((rows, D_ML), lambda c: (c, 0)),
            pl.BlockSpec((nb * GATE_ROWS, lt), lambda c: (c, 0)),
            pl.BlockSpec((nb * GATE_ROWS, lt), lambda c: (c, 0)),
        ],
        out_shape=[
            jax.ShapeDtypeStruct((N_SLABS, nc * rows, LANES), F32),
            jax.ShapeDtypeStruct((nc * rows, D_ML), F32),
            jax.ShapeDtypeStruct((nc * nb * GATE_ROWS, lt), F32),
            jax.ShapeDtypeStruct((nc * nb * GATE_ROWS, lt), F32),
        ],
        compiler_params=pltpu.CompilerParams(dimension_semantics=("arbitrary",), vmem_limit_bytes=VMEM_LIMIT),
        name=name,
    )(x, lw["ln1"], lw["w_in"], lw["wgi"], lw["wgf"])


def _lead_block(meta_ref, nb, lt):
    meta = meta_ref[...]
    blk = jnp.concatenate([jnp.zeros((lt - meta.shape[0], meta.shape[1]), F32), meta], axis=0)
    return jnp.broadcast_to(blk[None], (nb, lt, meta.shape[1]))


def _outmlp_body(x_ref, y_ref, wo_ref, g2_ref, w1_ref, w2_ref, gf_ref, *rest, final, nzero, lead):
    o_ref = rest[-1]
    nb, lt, d = y_ref.shape
    rows = nb * lt

    def compute():
        x = x_ref[...]
        if lead:
            x = jnp.where(pl.program_id(0) == nzero, _lead_block(rest[0], nb, lt), x)
        x = x.reshape(rows, d)
        y = y_ref[...].reshape(rows, d)
        x1 = x + jnp.dot(y, wo_ref[...], preferred_element_type=F32)
        u2 = _rms(x1, g2_ref[...]).astype(BF16)
        acc = x1
        for c in range(D_FF // FF_TILE):
            cols = slice(c * FF_TILE, (c + 1) * FF_TILE)
            h = jnp.dot(u2, w1_ref[:, cols], preferred_element_type=F32)
            h = jnp.square(jnp.maximum(h, 0.0)).astype(BF16)
            acc = acc + jnp.dot(h, w2_ref[cols, :], preferred_element_type=F32)
        if final:
            acc = _rms(acc, gf_ref[...])
        o_ref[...] = acc.reshape(nb, lt, d)

    def zero():
        o_ref[...] = jnp.zeros(o_ref.shape, F32)

    if nzero:
        i = pl.program_id(0)
        pl.when(i < nzero)(zero)
        pl.when(i >= nzero)(compute)
    else:
        compute()


def _outmlp(x, y, lw, l, gf, *, lt, final, skip, nzero, name, meta=None):
    nb, t, d = y.shape
    nsteps = t // lt - skip
    lead = meta is not None
    x_map = (lambda i: (0, jnp.maximum(i - nzero - 1, 0), 0)) if lead else (lambda i: (0, i + skip, 0))
    return pl.pallas_call(
        functools.partial(_outmlp_body, final=final, nzero=nzero, lead=lead),
        grid=(nsteps,),
        in_specs=[
            pl.BlockSpec((nb, lt, d), x_map),
            pl.BlockSpec((nb, lt, d), lambda i: (0, i + skip, 0)),
            _layer_spec(lw["w_out"], l),
            _layer_spec(lw["ln2"], l),
            _layer_spec(lw["w_ff1"], l),
            _layer_spec(lw["w_ff2"], l),
            _const_spec((1, d)),
        ] + ([_const_spec(meta.shape)] if lead else []),
        out_specs=pl.BlockSpec((nb, lt, d), lambda i: (0, i, 0)),
        out_shape=jax.ShapeDtypeStruct((nb, nsteps * lt, d), F32),
        compiler_params=pltpu.CompilerParams(dimension_semantics=("arbitrary",), vmem_limit_bytes=VMEM_LIMIT),
        name=name,
    )(x, y, lw["w_out"], lw["ln2"], lw["w_ff1"], lw["w_ff2"], gf, *([meta] if lead else []))


def _rg_gates(xc, wa_ref, wx_ref, rba_ref, rbx_ref, lam_ref, g):
    sl = slice(g * LANES, (g + 1) * LANES)
    xg = xc[:, sl]
    xb = xg.astype(BF16)
    r = _sigmoid(jnp.dot(xb, wa_ref[g], preferred_element_type=F32) + rba_ref[:, sl])
    i = _sigmoid(jnp.dot(xb, wx_ref[g], preferred_element_type=F32) + rbx_ref[:, sl])
    log_a = (RG_C * r) * jax.nn.log_sigmoid(lam_ref[:, sl])
    a = jnp.exp(log_a)
    gx = _sqrt_nonneg(1.0 - a * a) * (i * xg)
    return a, gx


def _row_scan(x, op, fill):
    row = lax.broadcasted_iota(jnp.int32, x.shape, 0)
    sh = 1
    while sh < x.shape[0]:
        x = op(x, jnp.where(row >= sh, pltpu.roll(x, sh, 0), fill))
        sh *= 2
    return x


def _slabs(ptm_ref, first):
    return jnp.concatenate([ptm_ref[first + g] for g in range(N_GROUPS)], axis=1)


def _pmix_body(x_ref, meta_ref, g1_ref, w_ref, wgi_ref, wgf_ref,
               rcw_ref, rcb_ref, wa_ref, wx_ref, rba_ref, rbx_ref, lam_ref,
               mcw_ref, mcb_ref, wq_ref, wkt_ref, wv_ref, bi_ref, bf_ref, mng_ref,
               y_ref, rgh_ref, rgc_ref, mcv_ref, st_ref, m_ref,
               px_s, pg_s, pm_s, po_s, a_s, g_s, bmc_s, bmx_s, yb_s, q_s, kt_s, v_s, ho_s, col_s, row_s, *, pad, lead):
    nb = y_ref.shape[0]
    lt = CHUNK
    rows = nb * lt
    ntail = (CONV_W - 1) * nb
    c = pl.program_id(0)

    @pl.when(c == 0)
    def _init():
        rgh_ref[...] = jnp.zeros_like(rgh_ref)
        rgc_ref[...] = jnp.zeros_like(rgc_ref)
        mcv_ref[...] = jnp.zeros_like(mcv_ref)
        st_ref[...] = jnp.zeros_like(st_ref)
        m_ref[...] = jnp.zeros_like(m_ref)
        ones_col = (lax.broadcasted_iota(jnp.int32, (rows, ML_HD), 1) == 0).astype(BF16)
        for h in range(ML_HEADS):
            v_s[h, :, ML_HD:S_COLS] = ones_col

    def conv(x, w_ref, b_ref, tail_ref):
        tail = tail_ref[...]
        acc = b_ref[...] + w_ref[CONV_W - 1:CONV_W, :] * x
        for k in range(1, CONV_W):
            shifted = jnp.concatenate([tail[ntail - k * nb:], x[:rows - k * nb]], axis=0)
            acc = acc + w_ref[CONV_W - 1 - k:CONV_W - k, :] * shifted
        tail_ref[...] = x[rows - ntail:]
        return acc

    x = x_ref[...]
    if lead:
        x = jnp.where(c == 0, _lead_block(meta_ref, nb, lt), x)
    u = _rms(x.reshape(rows, x.shape[2]), g1_ref[...]).astype(BF16)
    _project(u, w_ref, 2, pm_s, 0, nb, lt)
    gi = _gate_preacts(u, wgi_ref, nb, lt)
    gf = _gate_preacts(u, wgf_ref, nb, lt)

    mx = _slabs(pm_s, 0)
    mconv = conv(mx, mcw_ref, mcb_ref, mcv_ref)
    mc = mconv * _sigmoid(mconv)
    _project(u, w_ref, 0, px_s, 0, nb, lt)
    for t in range(lt):
        rs = slice(t * nb, (t + 1) * nb)
        idx = pl.ds(t, nb, stride=PITCH)
        for g in range(N_GROUPS):
            cols = slice(g * LANES, (g + 1) * LANES)
            bmc_s[g, idx, :] = mc[rs, cols]
            bmx_s[g, idx, :] = mx[rs, cols]

    def seq_major(src):
        return jnp.concatenate(
            [jnp.concatenate([src[g, b * PITCH:b * PITCH + lt, :] for b in range(nb)], axis=0)
             for g in range(N_GROUPS)], axis=1).astype(BF16)

    mcb = seq_major(bmc_s)
    mxb = seq_major(bmx_s)
    for h in range(ML_HEADS):
        sl = slice(h * ML_HD, (h + 1) * ML_HD)
        q_s[h] = jnp.dot(mcb[:, sl], wq_ref[h], preferred_element_type=F32) * (ML_HD ** -0.5)
        kt = lax.dot_general(wkt_ref[h], mcb[:, sl], _NT, preferred_element_type=F32)
        for b in range(nb):
            kt_s[h, b] = kt[:, b * lt:(b + 1) * lt]
        v_s[h, :, 0:ML_HD] = jnp.dot(mxb[:, sl], wv_ref[h], preferred_element_type=F32).astype(BF16)

    def to_cols(r):
        return jnp.concatenate([r, jnp.zeros((LANES - r.shape[0], lt), F32)], axis=0).T

    ig = to_cols(gi) + bi_ref[...]
    lf = jax.nn.log_sigmoid(to_cols(gf) + bf_ref[...])
    if pad:
        trow = lax.broadcasted_iota(jnp.int32, ig.shape, 0)
        is_pad = trow < jnp.where(c == 0, pad, 0)
        ig = jnp.where(is_pad, -1e30, ig)
        lf = jnp.where(is_pad, 0.0, lf)
    bcs = _row_scan(lf, jnp.add, 0.0)
    gg = ig - bcs
    cm = _row_scan(gg, jnp.maximum, -jnp.inf)
    m0 = m_ref[0:1, :]
    mm = jnp.maximum(m0, cm)
    mt = bcs + mm
    b_last = bcs[lt - 1:lt, :]
    m_last = mt[lt - 1:lt, :]
    col_s[0] = mm
    col_s[1] = jnp.exp(m0 - mm)
    col_s[2] = jnp.exp(-mt)
    row_s[0] = gg.T
    row_s[1] = jnp.exp(b_last - m_last + gg).T
    row_s[2] = jnp.broadcast_to(jnp.exp(b_last + m0 - m_last), (lt, LANES)).T
    m_ref[...] = jnp.broadcast_to(m_last, m_ref.shape)

    rg = {}

    def rg_conv():
        rg["xc"] = conv(_slabs(px_s, 0), rcw_ref, rcb_ref, rgc_ref)
        if pad:
            rowi = lax.broadcasted_iota(jnp.int32, (rows, 1), 0)
            rg["keep"] = rowi >= jnp.where(c == 0, pad * nb, 0)

    def rg_gates(g):
        a, gx = _rg_gates(rg["xc"], wa_ref, wx_ref, rba_ref, rbx_ref, lam_ref, g)
        if pad:
            gx = jnp.where(rg["keep"], gx, 0.0)
        a_s[g] = a
        g_s[g] = gx

    def rg_scan(g):
        sl = slice(g * LANES, (g + 1) * LANES)
        gate = jax.nn.gelu(pg_s[g])
        h = rgh_ref[:, sl]
        for t in range(lt):
            rs = slice(t * nb, (t + 1) * nb)
            h = a_s[g, rs, :] * h + g_s[g, rs, :]
            yb_s[g, pl.ds(t, nb, stride=PITCH), :] = h * gate[rs]
        rgh_ref[:, sl] = h

    def rg_out(g):
        for b in range(nb):
            y_ref[b, :, g * LANES:(g + 1) * LANES] = yb_s[g, b * PITCH:b * PITCH + lt, :].astype(BF16)

    def gate_out(b):
        trows = slice(b * lt, (b + 1) * lt)
        for h in range(ML_HEADS):
            sl = slice(h * ML_HD, (h + 1) * ML_HD)
            hm = _sigmoid(po_s[trows, sl]) * ho_s[b * ML_HEADS + h]
            hm = hm * lax.rsqrt(jnp.mean(hm * hm, axis=-1, keepdims=True) + EPS)
            y_ref[b, :, D_RG + h * ML_HD:D_RG + (h + 1) * ML_HD] = (hm * mng_ref[:, sl]).astype(BF16)

    pieces = [[] for _ in range(nb + 1)]
    pieces[0].append(rg_conv)
    for g in range(N_GROUPS):
        pieces[g].append(functools.partial(rg_gates, g))
        pieces[g + 1].append(functools.partial(rg_scan, g))
        pieces[g + 2].append(functools.partial(rg_out, g))
    pieces[0].append(functools.partial(_project, u, w_ref, 1, pg_s, 0, nb, lt))
    pieces[1].append(functools.partial(_project, u, w_ref, 3, po_s, 0, nb, lt))
    for b in range(nb):
        pieces[max(b + 1, 2)].append(functools.partial(gate_out, b))

    t_i = lax.broadcasted_iota(jnp.int32, (lt, lt), 0)
    s_i = lax.broadcasted_iota(jnp.int32, (lt, lt), 1)
    causal = s_i <= t_i

    for b in range(nb):
        for piece in pieces[b]:
            piece()
        trows = slice(b * lt, (b + 1) * lt)
        heads = range(ML_HEADS)
        qb = [q_s[h, trows, :].astype(BF16) for h in heads]
        kt = [kt_s[h, b] for h in heads]
        s0 = [st_ref[b * ML_HEADS + h] for h in heads]
        sc = [jnp.dot(qb[h], kt[h].astype(BF16), preferred_element_type=F32) for h in heads]
        qs = [jnp.dot(qb[h], s0[h].astype(BF16), preferred_element_type=F32) for h in heads]
        pm = []
        for h in heads:
            r = b * GATE_ROWS + h
            d = jnp.exp(jnp.where(causal, row_s[0, r:r + 1, :] - col_s[0, :, r:r + 1], -jnp.inf))
            pm.append((sc[h] * d).astype(BF16))
        pv = [jnp.dot(pm[h], v_s[h, trows, :], preferred_element_type=F32) for h in heads]
        for h in heads:
            r = b * GATE_ROWS + h
            numx = pv[h] + col_s[1, :, r:r + 1] * qs[h]
            den = numx[:, ML_HD:ML_HD + 1]
            ho_s[b * ML_HEADS + h] = numx[:, 0:ML_HD] / jnp.maximum(jnp.abs(den), col_s[2, :, r:r + 1])
        for h in heads:
            r = b * GATE_ROWS + h
            wkt = (kt[h] * row_s[1, r:r + 1, :]).astype(BF16)
            sc_row = jnp.concatenate([row_s[2, r:r + 1, :], row_s[2, r:r + 1, :]], axis=1)
            st_ref[b * ML_HEADS + h] = sc_row * s0[h] + jnp.dot(wkt, v_s[h, trows, :], preferred_element_type=F32)
    for piece in pieces[nb]:
        piece()

    if pad:
        @pl.when(c == 0)
        def _zero_pad_rows():
            y_ref[:, 0:pad, :] = jnp.zeros((nb, pad, y_ref.shape[2]), BF16)


def _pmix(x, meta, lw, l, *, pad, lead, name):
    nb, t, d = x.shape
    nc = t // CHUNK + (1 if lead else 0)
    rows = nb * CHUNK
    ntail = (CONV_W - 1) * nb
    in_arrays = [x, meta, lw["ln1"], lw["w_in"], lw["wgi"], lw["wgf"],
                 lw["rcw"], lw["rcb"], lw["wa"], lw["wx"], lw["rba"], lw["rbx"], lw["lam"],
                 lw["mcw"], lw["mcb"], lw["wq"], lw["wkt"], lw["wv"], lw["bi"], lw["bf"], lw["mng"]]
    x_map = (lambda c: (0, jnp.maximum(c - 1, 0), 0)) if lead else (lambda c: (0, c, 0))
    in_specs = [pl.BlockSpec((nb, CHUNK, d), x_map), _const_spec(meta.shape)] + [
        _layer_spec(a, l) for a in in_arrays[2:]]
    in_specs[3] = pl.BlockSpec((None, d, D_MAIN), lambda c: (l, 0, 0), pipeline_mode=pl.Buffered(1))
    out_shape = [
        jax.ShapeDtypeStruct((nb, nc * CHUNK, D_MODEL), BF16),
        jax.ShapeDtypeStruct((nb, D_RG), F32),
        jax.ShapeDtypeStruct((ntail, D_RG), F32),
        jax.ShapeDtypeStruct((ntail, D_ML), F32),
        jax.ShapeDtypeStruct((nb * ML_HEADS, ML_HD, S_COLS), F32),
        jax.ShapeDtypeStruct((SUBLANES, LANES), F32),
    ]
    out_specs = [pl.BlockSpec((nb, CHUNK, D_MODEL), lambda c: (0, c, 0))] + [
        pl.BlockSpec(s.shape, lambda c, n=len(s.shape): (0,) * n) for s in out_shape[1:]]
    scratch = [
        pltpu.VMEM((N_GROUPS, rows, LANES), F32),
        pltpu.VMEM((N_GROUPS, rows, LANES), F32),
        pltpu.VMEM((N_GROUPS, rows, LANES), F32),
        pltpu.VMEM((rows, D_ML), F32),
        pltpu.VMEM((N_GROUPS, rows, LANES), F32),
        pltpu.VMEM((N_GROUPS, rows, LANES), F32),
        pltpu.VMEM((N_GROUPS, nb * PITCH, LANES), F32),
        pltpu.VMEM((N_GROUPS, nb * PITCH, LANES), F32),
        pltpu.VMEM((N_GROUPS, nb * PITCH, LANES), F32),
        pltpu.VMEM((ML_HEADS, rows, ML_HD), F32),
        pltpu.VMEM((ML_HEADS, nb, ML_HD, CHUNK), F32),
        pltpu.VMEM((ML_HEADS, rows, S_COLS), BF16),
        pltpu.VMEM((nb * ML_HEADS, CHUNK, ML_HD), F32),
        pltpu.VMEM((3, CHUNK, LANES), F32),
        pltpu.VMEM((3, LANES, CHUNK), F32),
    ]
    return pl.pallas_call(
        functools.partial(_pmix_body, pad=pad, lead=lead),
        grid=(nc,),
        in_specs=in_specs,
        out_specs=out_specs,
        out_shape=out_shape,
        scratch_shapes=scratch,
        compiler_params=pltpu.CompilerParams(dimension_semantics=("arbitrary",), vmem_limit_bytes=VMEM_LIMIT),
        name=name,
    )(*in_arrays)


SBLK = SUBLANES


def _smix_body(ptm_ref, pbm_ref, gi_ref, gf_ref, rgh_ref, rgc_ref, mcv_ref, c_ref, n_ref, mcol_ref, mrow_ref,
               rcw_ref, rcb_ref, wa_ref, wx_ref, rba_ref, rbx_ref, lam_ref,
               mcw_ref, mcb_ref, wq_ref, wk_ref, wv_ref, bi_ref, bf_ref, bic_ref, bfc_ref, mng_ref,
               y_ref, rgh_o, rgc_o, mcv_o, n_o, m_o, inter_o, dk_o, v_o,
               q_s, qc_s, col_s):
    i = pl.program_id(0)
    ns = pbm_ref.shape[0]

    def conv(first, w_ref, b_ref, tail_ref, tail_o):
        x = _slabs(ptm_ref, first)
        acc = b_ref[...] + w_ref[CONV_W - 1:CONV_W, :] * x
        for j in range(CONV_W - 1):
            acc = acc + w_ref[j:j + 1, :] * tail_ref[j]
        for j in range(CONV_W - 2):
            tail_o[j] = tail_ref[j + 1]
        tail_o[CONV_W - 2] = x
        return acc, x

    @pl.when(i == 0)
    def _rowwise():
        xc, _ = conv(0, rcw_ref, rcb_ref, rgc_ref, rgc_o)
        for g in range(N_GROUPS):
            sl = slice(g * LANES, (g + 1) * LANES)
            a, gx = _rg_gates(xc, wa_ref, wx_ref, rba_ref, rbx_ref, lam_ref, g)
            hn = a * rgh_ref[:, sl] + gx
            rgh_o[:, sl] = hn
            y_ref[0, :, sl] = (hn * jax.nn.gelu(ptm_ref[N_GROUPS + g])).astype(BF16)

        mconv, mx = conv(2 * N_GROUPS, mcw_ref, mcb_ref, mcv_ref, mcv_o)
        mcb = (mconv * _sigmoid(mconv)).astype(BF16)
        mxb = mx.astype(BF16)

        ig_r = gi_ref[...] + bi_ref[...]
        lf_r = jax.nn.log_sigmoid(gf_ref[...] + bf_ref[...])
        m0_r = mrow_ref[...]
        m_o[...] = jnp.maximum(lf_r + m0_r, ig_r)
        z = jnp.concatenate([gi_ref[...], gf_ref[...], jnp.zeros((ns - 2 * GATE_ROWS, ns), F32)], axis=0).T
        ig_c = z[:, 0:GATE_ROWS] + bic_ref[...]
        lf_c = jax.nn.log_sigmoid(z[:, GATE_ROWS:2 * GATE_ROWS] + bfc_ref[...])
        m0_c = mcol_ref[...]
        m_c = jnp.maximum(lf_c + m0_c, ig_c)
        inter_c = jnp.exp(lf_c + m0_c - m_c)
        dd_c = jnp.exp(ig_c - m_c)
        inter_o[...] = inter_c
        col_s[0] = inter_c
        col_s[1] = jnp.exp(-m_c)
        for h in range(ML_HEADS):
            sl = slice(h * ML_HD, (h + 1) * ML_HD)
            q = jnp.dot(mcb[:, sl], wq_ref[h], preferred_element_type=F32) * (ML_HD ** -0.5)
            k = jnp.dot(mcb[:, sl], wk_ref[h], preferred_element_type=F32)
            v = jnp.dot(mxb[:, sl], wv_ref[h], preferred_element_type=F32)
            q_s[h] = q
            dk_o[h] = dd_c[:, h:h + 1] * k
            v_o[h] = v
            col_s[2, :, h:h + 1] = jnp.sum(q * k, axis=-1, keepdims=True) * dd_c[:, h:h + 1]
            n0 = n_ref[h]
            col_s[3, :, h:h + 1] = jnp.sum(q * n0, axis=-1, keepdims=True)
            n_o[h] = inter_c[:, h:h + 1] * n0 + dd_c[:, h:h + 1] * k

    blk = pl.ds(pl.multiple_of(i * SBLK, SBLK), SBLK)
    for h in range(ML_HEADS):
        z = jnp.concatenate([q_s[h, blk, :], jnp.zeros((ML_HD - SBLK, ML_HD), F32)], axis=0).T
        qc_s[h, blk, :] = jnp.concatenate(
            [jnp.sum(z[:, j:j + 1] * c_ref[0, j, h], axis=0, keepdims=True) for j in range(SBLK)], axis=0)

    @pl.when(i == pl.num_programs(0) - 1)
    def _finish():
        for h in range(ML_HEADS):
            sl = slice(h * ML_HD, (h + 1) * ML_HD)
            inter = col_s[0, :, h:h + 1]
            s = col_s[2, :, h:h + 1]
            num = s * v_o[h] + inter * qc_s[h]
            den = s + inter * col_s[3, :, h:h + 1]
            hout = num / jnp.maximum(jnp.abs(den), col_s[1, :, h:h + 1])
            hm = _sigmoid(pbm_ref[:, sl]) * hout
            hm = hm * lax.rsqrt(jnp.mean(hm * hm, axis=-1, keepdims=True) + EPS)
            y_ref[0, :, D_RG + h * ML_HD:D_RG + (h + 1) * ML_HD] = (hm * mng_ref[:, sl]).astype(BF16)


def _smix(ptm, pbm, gi, gf, st, lw, l, *, name):
    ns = pbm.shape[0]
    rgh, rgc, mcv, mc_all, mn, mcol, mrow = st
    in_arrays = [ptm, pbm, gi, gf, rgh, rgc, mcv, mc_all, mn, mcol, mrow,
                 lw["rcw"], lw["rcb"], lw["wa"], lw["wx"], lw["rba"], lw["rbx"], lw["lam"],
                 lw["mcw"], lw["mcb"], lw["wq"], lw["wk"], lw["wv"], lw["bi_s"], lw["bf_s"],
                 lw["bi_c"], lw["bf_c"], lw["mng"]]
    in_specs = [_const_spec(a.shape) for a in in_arrays[:4]] + [_layer_spec(a, l) for a in in_arrays[4:]]
    in_specs[7] = pl.BlockSpec((1, SBLK, ML_HEADS, ML_HD, ML_HD), lambda i: (l, i, 0, 0, 0))
    out_shape = [
        jax.ShapeDtypeStruct((1, ns, D_MODEL), BF16),
        jax.ShapeDtypeStruct((ns, D_RG), F32),
        jax.ShapeDtypeStruct((CONV_W - 1, ns, D_RG), F32),
        jax.ShapeDtypeStruct((CONV_W - 1, ns, D_ML), F32),
        jax.ShapeDtypeStruct((ML_HEADS, ns, ML_HD), F32),
        jax.ShapeDtypeStruct((GATE_ROWS, ns), F32),
        jax.ShapeDtypeStruct((ns, GATE_ROWS), F32),
        jax.ShapeDtypeStruct((ML_HEADS, ns, ML_HD), F32),
        jax.ShapeDtypeStruct((ML_HEADS, ns, ML_HD), F32),
    ]
    out_specs = [pl.BlockSpec(s.shape, lambda i, n=len(s.shape): (0,) * n) for s in out_shape]
    scratch = [
        pltpu.VMEM((ML_HEADS, ns, ML_HD), F32),
        pltpu.VMEM((ML_HEADS, ns, ML_HD), F32),
        pltpu.VMEM((4, ns, GATE_ROWS), F32),
    ]
    return pl.pallas_call(
        _smix_body,
        grid=(ns // SBLK,),
        in_specs=in_specs,
        out_specs=out_specs,
        out_shape=out_shape,
        scratch_shapes=scratch,
        compiler_params=pltpu.CompilerParams(dimension_semantics=("arbitrary",), vmem_limit_bytes=VMEM_LIMIT),
        name=name,
    )(*in_arrays)


CBLK = 2 * SUBLANES


def _cupdate_body(c_ref, inter_ref, dk_ref, v_ref, o_ref):
    for h in range(ML_HEADS):
        z = jnp.concatenate([dk_ref[0, h], jnp.zeros((ML_HD - CBLK, ML_HD), F32)], axis=0).T
        inter_rows = jnp.broadcast_to(inter_ref[0, :, h:h + 1], (CBLK, LANES))
        vblk = v_ref[0, h]
        for j in range(CBLK):
            o_ref[0, j, h] = inter_rows[j:j + 1, :] * c_ref[0, j, h] + z[:, j:j + 1] * vblk[j:j + 1, :]


def _cupdate(mc_all, inter_all, dk_all, v_all):
    depth, ns = mc_all.shape[:2]
    cspec = pl.BlockSpec((1, CBLK, ML_HEADS, ML_HD, ML_HD), lambda l, i: (l, i, 0, 0, 0))
    kvspec = pl.BlockSpec((1, ML_HEADS, CBLK, ML_HD), lambda l, i: (l, 0, i, 0))
    return pl.pallas_call(
        _cupdate_body,
        grid=(depth, ns // CBLK),
        in_specs=[cspec, pl.BlockSpec((1, CBLK, GATE_ROWS), lambda l, i: (l, i, 0)), kvspec, kvspec],
        out_specs=cspec,
        out_shape=jax.ShapeDtypeStruct(mc_all.shape, F32),
        compiler_params=pltpu.CompilerParams(dimension_semantics=("arbitrary", "arbitrary"),
                                             vmem_limit_bytes=VMEM_LIMIT),
        name="cupdate_s",
    )(mc_all, inter_all, dk_all, v_all)


def _cast_body(w_ref, o_ref):
    o_ref[...] = w_ref[...].astype(BF16)


def _main_cols_bf16(w_in):
    depth, d, _ = w_in.shape
    spec = pl.BlockSpec((1, d // 2, D_MAIN), lambda l, i: (l, i, 0))
    return pl.pallas_call(
        _cast_body,
        grid=(depth, 2),
        in_specs=[spec],
        out_specs=spec,
        out_shape=jax.ShapeDtypeStruct((depth, d, D_MAIN), BF16),
        compiler_params=pltpu.CompilerParams(dimension_semantics=("arbitrary", "arbitrary")),
        name="cast_w_in",
    )(w_in)


def _block_diag_pairs(w):
    depth = w.shape[0]
    w = w.reshape(depth, N_GROUPS, 2, RG_BLOCK, RG_BLOCK)
    zero = jnp.zeros((depth, N_GROUPS, RG_BLOCK, RG_BLOCK), w.dtype)
    top = jnp.concatenate([w[:, :, 0], zero], axis=-1)
    bottom = jnp.concatenate([zero, w[:, :, 1]], axis=-1)
    return jnp.concatenate([top, bottom], axis=-2)


def _gate_rows(w_cols):
    depth = w_cols.shape[0]
    return jnp.concatenate([jnp.swapaxes(w_cols, 1, 2),
                            jnp.zeros((depth, GATE_ROWS - ML_HEADS, D_MODEL), w_cols.dtype)], axis=1)


def _stacked_weights(ln1_g, w_in, rg_conv_w, rg_conv_b, rg_w_a, rg_w_x, rg_b_a, rg_b_x, rg_lambda,
                     m_conv_w, m_conv_b, m_w_q, m_w_k, m_w_v, m_b_i, m_b_f, m_norm_g, w_out, ln2_g,
                     w_ff1, w_ff2, nb_prompt, ns):
    depth = ln1_g.shape[0]
    row = lambda a: a.reshape(depth, 1, -1)
    pad8 = lambda a: jnp.concatenate([a, jnp.zeros((depth, GATE_ROWS - ML_HEADS), F32)], axis=1)
    bias8, bfor8 = pad8(m_b_i), pad8(m_b_f)
    lanes = lambda a: jnp.concatenate([jnp.tile(a, (1, nb_prompt)),
                                       jnp.zeros((depth, LANES - nb_prompt * GATE_ROWS), F32)], axis=1)[:, None, :]
    return dict(
        ln1=row(ln1_g),
        w_in=_main_cols_bf16(w_in),
        wgi=_gate_rows(w_in[:, :, D_MAIN:D_MAIN + ML_HEADS]).astype(BF16),
        wgf=_gate_rows(w_in[:, :, D_MAIN + ML_HEADS:]).astype(BF16),
        rcw=rg_conv_w, rcb=row(rg_conv_b),
        wa=_block_diag_pairs(rg_w_a).astype(BF16), wx=_block_diag_pairs(rg_w_x).astype(BF16),
        rba=row(rg_b_a), rbx=row(rg_b_x), lam=row(rg_lambda),
        mcw=m_conv_w, mcb=row(m_conv_b),
        wq=m_w_q.astype(BF16), wk=m_w_k.astype(BF16), wkt=jnp.swapaxes(m_w_k, 2, 3).astype(BF16),
        wv=m_w_v.astype(BF16),
        bi=lanes(bias8), bf=lanes(bfor8),
        bi_s=jnp.broadcast_to(bias8[:, :, None], (depth, GATE_ROWS, ns)),
        bf_s=jnp.broadcast_to(bfor8[:, :, None], (depth, GATE_ROWS, ns)),
        bi_c=bias8.reshape(depth, 1, GATE_ROWS), bf_c=bfor8.reshape(depth, 1, GATE_ROWS),
        mng=row(m_norm_g),
        w_out=w_out.astype(BF16), ln2=row(ln2_g),
        w_ff1=w_ff1.astype(BF16), w_ff2=w_ff2.astype(BF16),
    )


def kernel(x_prompt, x_sample, state_rg_h, state_rg_conv, state_m_conv, state_m_C, state_m_n, state_m_m,
           meta_tokens, ln1_g, w_in, rg_conv_w, rg_conv_b, rg_w_a, rg_w_x, rg_b_a, rg_b_x, rg_lambda,
           m_conv_w, m_conv_b, m_w_q, m_w_k, m_w_v, m_b_i, m_b_f, m_norm_g, w_out, ln2_g,
           w_ff1, w_ff2, ln_f_g):
    nb, seq, d = x_prompt.shape
    ns = x_sample.shape[0]
    depth = ln1_g.shape[0]
    t_real = N_META + seq
    nc = -(-t_real // CHUNK)
    pad = nc * CHUNK - t_real
    assert (pad + N_META) % CHUNK == 0 and x_sample.shape[1] == 1 and ns == LANES and depth == DEPTH
    assert CHUNK == LANES and nb * GATE_ROWS <= LANES

    lw = _stacked_weights(ln1_g, w_in, rg_conv_w, rg_conv_b, rg_w_a, rg_w_x, rg_b_a, rg_b_x, rg_lambda,
                          m_conv_w, m_conv_b, m_w_q, m_w_k, m_w_v, m_b_i, m_b_f, m_norm_g, w_out, ln2_g,
                          w_ff1, w_ff2, nb, ns)
    lnf = ln_f_g.reshape(1, d)
    meta = meta_tokens.astype(F32)
    xp = x_prompt
    xs = x_sample.reshape(1, ns, d)
    m_cols = jnp.concatenate([state_m_m, jnp.zeros((depth, ns, GATE_ROWS - ML_HEADS), F32)], axis=2)
    st_in = (state_rg_h, jnp.swapaxes(state_rg_conv, 1, 2), jnp.swapaxes(state_m_conv, 1, 2), state_m_C,
             jnp.swapaxes(state_m_n, 1, 2), m_cols, jnp.swapaxes(m_cols, 1, 2))

    p_states = [[] for _ in range(6)]
    s_states = [[] for _ in range(6)]
    for l in range(depth):
        last = l == depth - 1

        y, rgh, rgc, mcv, st, mrow = _pmix(xp, meta, lw, l, pad=pad, lead=l == 0, name=f"mixer_p{l}")
        lt_out = CHUNK // 2
        skip = (pad + N_META) // lt_out if last else 0
        xp = _outmlp(xp, y, lw, l, lnf, lt=lt_out, final=last, skip=skip,
                     nzero=0 if last else pad // lt_out, name=f"outmlp_p{l}", meta=meta if l == 0 else None)
        st = st.reshape(nb, ML_HEADS, ML_HD, S_COLS)
        p_states[0].append(rgh)
        p_states[1].append(jnp.swapaxes(rgc.reshape(CONV_W - 1, nb, D_RG), 0, 1))
        p_states[2].append(jnp.swapaxes(mcv.reshape(CONV_W - 1, nb, D_ML), 0, 1))
        p_states[3].append(st[..., :ML_HD])
        p_states[4].append(st[..., ML_HD])
        p_states[5].append(mrow[0, :nb * GATE_ROWS].reshape(nb, GATE_ROWS)[:, :ML_HEADS])

        stm, sbm, gis, gfs = _inproj(xs, lw, l, lt=ns, name=f"inproj_s{l}")
        ys, srgh, srgc, smcv, smn, smrow, sinter, sdk, sv = _smix(stm, sbm, gis, gfs, st_in, lw, l,
                                                                  name=f"mixer_s{l}")
        xs = _outmlp(xs, ys, lw, l, lnf, lt=ns, final=last, skip=0, nzero=0, name=f"outmlp_s{l}")
        s_states[0].append(srgh)
        s_states[1].append(jnp.swapaxes(srgc, 0, 1))
        s_states[2].append(jnp.swapaxes(smcv, 0, 1))
        s_states[3].append((sinter, sdk, sv))
        s_states[4].append(jnp.swapaxes(smn, 0, 1))
        s_states[5].append(smrow[:ML_HEADS].T)

    y_prompt = xp
    y_sample = xs.reshape(ns, 1, d)
    ps_out = [jnp.stack(s) for s in p_states]
    c_new = _cupdate(state_m_C, *(jnp.stack([t[j] for t in s_states[3]]) for j in range(3)))
    ss_out = [c_new if j == 3 else jnp.stack(s) for j, s in enumerate(s_states)]
    return (y_prompt, y_sample, *ps_out, *ss_out)
```

```python
import functools

import jax
import jax.numpy as jnp
from jax import lax
from jax.experimental import pallas as pl
from jax.experimental.pallas import tpu as pltpu

F32 = jnp.float32
BF16 = jnp.bfloat16

D_MODEL = 1024
N_META = 16
D_RG = 512
RG_BLOCK = 64
RG_C = 8.0
D_ML = 512
ML_HEADS = 4
ML_HD = 128
CONV_W = 4
D_FF = 4096
EPS = 1e-6
DEPTH = 4

LANES = 128
SUBLANES = 8
CHUNK = 128
PITCH = CHUNK + SUBLANES
N_GROUPS = D_RG // LANES
D_MAIN = 2 * D_RG + 2 * D_ML
GATE_ROWS = SUBLANES
S_COLS = 2 * ML_HD
N_SLABS = 3 * N_GROUPS
FF_TILE = 1024
VMEM_LIMIT = 56 * 1024 * 1024

_NT = (((1,), (1,)), ((), ()))


def _const_spec(shape):
    zeros = (0,) * len(shape)
    return pl.BlockSpec(shape, lambda *_: zeros, pipeline_mode=pl.Buffered(1))


def _layer_spec(arr, l):
    tail = (0,) * (arr.ndim - 1)
    return pl.BlockSpec((None,) + arr.shape[1:], lambda *_: (l,) + tail, pipeline_mode=pl.Buffered(1))


def _rms(x, g):
    return x * lax.rsqrt(jnp.mean(x * x, axis=-1, keepdims=True) + EPS) * g


def _sigmoid(x):
    return 0.5 * jnp.tanh(0.5 * x) + 0.5


_GELU_K0 = 0.7978845608028654
_GELU_K1 = _GELU_K0 * 0.044715


def _gelu(x):
    hx = 0.5 * x
    return hx * jnp.tanh(x * (_GELU_K0 + _GELU_K1 * (x * x))) + hx


def _sqrt_nonneg(s):
    return jnp.where(s > 0.0, s * lax.rsqrt(s), 0.0)


def _project(u, w_ref, n, dst_ref, base, nb, lt):
    pr = jnp.dot(u, w_ref[:, n * 512:(n + 1) * 512], preferred_element_type=F32)
    if n == D_MAIN // 512 - 1:
        dst_ref[...] = pr
        return
    for g in range(N_GROUPS):
        cols = slice(g * LANES, (g + 1) * LANES)
        if nb == 1:
            dst_ref[base + g] = pr[:, cols]
            continue
        for b in range(nb):
            dst_ref[base + g, pl.ds(b, lt, stride=nb), :] = pr[b * lt:(b + 1) * lt, cols]


def _gate_preacts(u, wg_ref, nb, lt):
    return jnp.concatenate([lax.dot_general(wg_ref[...], u[b * lt:(b + 1) * lt], _NT, preferred_element_type=F32)
                            for b in range(nb)], axis=0)


def _inproj_body(x_ref, g_ref, w_ref, wgi_ref, wgf_ref, ptm_ref, pbm_ref, gi_ref, gf_ref):
    nb, lt, d = x_ref.shape
    x = x_ref[...].reshape(nb * lt, d)
    u = _rms(x, g_ref[...]).astype(BF16)
    for n in range(D_MAIN // 512 - 1):
        _project(u, w_ref, n, ptm_ref, n * N_GROUPS, nb, lt)
    _project(u, w_ref, D_MAIN // 512 - 1, pbm_ref, 0, nb, lt)
    gi_ref[...] = _gate_preacts(u, wgi_ref, nb, lt)
    gf_ref[...] = _gate_preacts(u, wgf_ref, nb, lt)


def _inproj(x, lw, l, *, lt, name):
    nb, t, d = x.shape
    nc = t // lt
    rows = nb * lt
    return pl.pallas_call(
        _inproj_body,
        grid=(nc,),
        in_specs=[
            pl.BlockSpec((nb, lt, d), lambda c: (0, c, 0)),
            _layer_spec(lw["ln1"], l),
            pl.BlockSpec((None, d, D_MAIN), lambda c: (l, 0, 0), pipeline_mode=pl.Buffered(1)),
            _layer_spec(lw["wgi"], l),
            _layer_spec(lw["wgf"], l),
        ],
        out_specs=[
            pl.BlockSpec((N_SLABS, rows, LANES), lambda c: (0, c, 0)),
            pl.BlockSpec((rows, D_ML), lambda c: (c, 0)),
            pl.BlockSpec((nb * GATE_ROWS, lt), lambda c: (c, 0)),
            pl.BlockSpec((nb * GATE_ROWS, lt), lambda c: (c, 0)),
        ],
        out_shape=[
            jax.ShapeDtypeStruct((N_SLABS, nc * rows, LANES), F32),
            jax.ShapeDtypeStruct((nc * rows, D_ML), F32),
            jax.ShapeDtypeStruct((nc * nb * GATE_ROWS, lt), F32),
            jax.ShapeDtypeStruct((nc * nb * GATE_ROWS, lt), F32),
        ],
        compiler_params=pltpu.CompilerParams(dimension_semantics=("arbitrary",), vmem_limit_bytes=VMEM_LIMIT),
        name=name,
    )(x, lw["ln1"], lw["w_in"], lw["wgi"], lw["wgf"])


def _lead_block(meta_ref, nb, lt):
    meta = meta_ref[...]
    blk = jnp.concatenate([jnp.zeros((lt - meta.shape[0], meta.shape[1]), F32), meta], axis=0)
    return jnp.broadcast_to(blk[None], (nb, lt, meta.shape[1]))


def _outmlp_body(x_ref, y_ref, wo_ref, g2_ref, w1_ref, w2_ref, gf_ref, *rest, final, nzero, lead):
    o_ref = rest[-1]
    nb, lt, d = y_ref.shape
    rows = nb * lt

    def compute():
        x = x_ref[...]
        if lead:
            x = jnp.where(pl.program_id(0) == nzero, _lead_block(rest[0], nb, lt), x)
        x = x.reshape(rows, d)
        y = y_ref[...].reshape(rows, d)
        x1 = x + jnp.dot(y, wo_ref[...], preferred_element_type=F32)
        u2 = _rms(x1, g2_ref[...]).astype(BF16)
        acc = x1
        for c in range(D_FF // FF_TILE):
            cols = slice(c * FF_TILE, (c + 1) * FF_TILE)
            h = jnp.dot(u2, w1_ref[:, cols], preferred_element_type=F32)
            h = jnp.square(jnp.maximum(h, 0.0)).astype(BF16)
            acc = acc + jnp.dot(h, w2_ref[cols, :], preferred_element_type=F32)
        if final:
            acc = _rms(acc, gf_ref[...])
        o_ref[...] = acc.reshape(nb, lt, d)

    def zero():
        o_ref[...] = jnp.zeros(o_ref.shape, F32)

    if nzero:
        i = pl.program_id(0)
        pl.when(i < nzero)(zero)
        pl.when(i >= nzero)(compute)
    else:
        compute()


def _outmlp(x, y, lw, l, gf, *, lt, final, skip, nzero, name, meta=None):
    nb, t, d = y.shape
    nsteps = t // lt - skip
    lead = meta is not None
    x_map = (lambda i: (0, jnp.maximum(i - nzero - 1, 0), 0)) if lead else (lambda i: (0, i + skip, 0))
    return pl.pallas_call(
        functools.partial(_outmlp_body, final=final, nzero=nzero, lead=lead),
        grid=(nsteps,),
        in_specs=[
            pl.BlockSpec((nb, lt, d), x_map),
            pl.BlockSpec((nb, lt, d), lambda i: (0, i + skip, 0)),
            _layer_spec(lw["w_out"], l),
            _layer_spec(lw["ln2"], l),
            _layer_spec(lw["w_ff1"], l),
            _layer_spec(lw["w_ff2"], l),
            _const_spec((1, d)),
        ] + ([_const_spec(meta.shape)] if lead else []),
        out_specs=pl.BlockSpec((nb, lt, d), lambda i: (0, i, 0)),
        out_shape=jax.ShapeDtypeStruct((nb, nsteps * lt, d), F32),
        compiler_params=pltpu.CompilerParams(dimension_semantics=("arbitrary",), vmem_limit_bytes=VMEM_LIMIT),
        name=name,
    )(x, y, lw["w_out"], lw["ln2"], lw["w_ff1"], lw["w_ff2"], gf, *([meta] if lead else []))


def _rg_gates(xc, wa_ref, wx_ref, rba_ref, rbx_ref, lam_ref, g):
    sl = slice(g * LANES, (g + 1) * LANES)
    xg = xc[:, sl]
    xb = xg.astype(BF16)
    i = _sigmoid(jnp.dot(xb, wx_ref[g], preferred_element_type=F32) + rbx_ref[:, sl])
    c = (0.5 * RG_C) * jax.nn.log_sigmoid(lam_ref[:, sl])
    pre = jnp.dot(xb, wa_ref[g], preferred_element_type=F32) + rba_ref[:, sl]
    a = jnp.exp(c * jnp.tanh(0.5 * pre) + c)
    gx = _sqrt_nonneg(1.0 - a * a) * (i * xg)
    return a, gx


def _row_scan(x, op, fill):
    row = lax.broadcasted_iota(jnp.int32, x.shape, 0)
    sh = 1
    while sh < x.shape[0]:
        x = op(x, jnp.where(row >= sh, pltpu.roll(x, sh, 0), fill))
        sh *= 2
    return x


def _slabs(ptm_ref, first):
    return jnp.concatenate([ptm_ref[first + g] for g in range(N_GROUPS)], axis=1)


def _pmix_body(x_ref, meta_ref, g1_ref, w_ref, wgi_ref, wgf_ref,
               rcw_ref, rcb_ref, wa_ref, wx_ref, rba_ref, rbx_ref, lam_ref,
               mcw_ref, mcb_ref, wq_ref, wkt_ref, wv_ref, bi_ref, bf_ref, mng_ref,
               y_ref, rgh_ref, rgc_ref, mcv_ref, st_ref, m_ref,
               px_s, pg_s, pm_s, po_s, a_s, g_s, bmc_s, bmx_s, yb_s, q_s, kt_s, v_s, ho_s, col_s, row_s, *, pad, lead):
    nb = y_ref.shape[0]
    lt = CHUNK
    rows = nb * lt
    ntail = (CONV_W - 1) * nb
    c = pl.program_id(0)

    @pl.when(c == 0)
    def _init():
        rgh_ref[...] = jnp.zeros_like(rgh_ref)
        rgc_ref[...] = jnp.zeros_like(rgc_ref)
        mcv_ref[...] = jnp.zeros_like(mcv_ref)
        st_ref[...] = jnp.zeros_like(st_ref)
        m_ref[...] = jnp.zeros_like(m_ref)
        ones_col = (lax.broadcasted_iota(jnp.int32, (rows, ML_HD), 1) == 0).astype(BF16)
        for h in range(ML_HEADS):
            v_s[h, :, ML_HD:S_COLS] = ones_col

    def conv(x, w_ref, b_ref, tail_ref):
        tail = tail_ref[...]
        acc = b_ref[...] + w_ref[CONV_W - 1:CONV_W, :] * x
        for k in range(1, CONV_W):
            shifted = jnp.concatenate([tail[ntail - k * nb:], x[:rows - k * nb]], axis=0)
            acc = acc + w_ref[CONV_W - 1 - k:CONV_W - k, :] * shifted
        tail_ref[...] = x[rows - ntail:]
        return acc

    x = x_ref[...]
    if lead:
        x = jnp.where(c == 0, _lead_block(meta_ref, nb, lt), x)
    u = _rms(x.reshape(rows, x.shape[2]), g1_ref[...]).astype(BF16)
    _project(u, w_ref, 2, pm_s, 0, nb, lt)
    gi = _gate_preacts(u, wgi_ref, nb, lt)
    gf = _gate_preacts(u, wgf_ref, nb, lt)

    mx = _slabs(pm_s, 0)
    mconv = conv(mx, mcw_ref, mcb_ref, mcv_ref)
    mc = mconv * _sigmoid(mconv)
    _project(u, w_ref, 0, px_s, 0, nb, lt)
    for t in range(lt):
        rs = slice(t * nb, (t + 1) * nb)
        idx = pl.ds(t, nb, stride=PITCH)
        for g in range(N_GROUPS):
            cols = slice(g * LANES, (g + 1) * LANES)
            bmc_s[g, idx, :] = mc[rs, cols]
            bmx_s[g, idx, :] = mx[rs, cols]

    def seq_major(src):
        return jnp.concatenate(
            [jnp.concatenate([src[g, b * PITCH:b * PITCH + lt, :] for b in range(nb)], axis=0)
             for g in range(N_GROUPS)], axis=1).astype(BF16)

    mcb = seq_major(bmc_s)
    mxb = seq_major(bmx_s)
    for h in range(ML_HEADS):
        sl = slice(h * ML_HD, (h + 1) * ML_HD)
        q_s[h] = jnp.dot(mcb[:, sl], wq_ref[h], preferred_element_type=F32) * (ML_HD ** -0.5)
        kt = lax.dot_general(wkt_ref[h], mcb[:, sl], _NT, preferred_element_type=F32)
        for b in range(nb):
            kt_s[h, b] = kt[:, b * lt:(b + 1) * lt]
        v_s[h, :, 0:ML_HD] = jnp.dot(mxb[:, sl], wv_ref[h], preferred_element_type=F32).astype(BF16)

    def to_cols(r):
        return jnp.concatenate([r, jnp.zeros((LANES - r.shape[0], lt), F32)], axis=0).T

    ig = to_cols(gi) + bi_ref[...]
    lf = jax.nn.log_sigmoid(to_cols(gf) + bf_ref[...])
    if pad:
        trow = lax.broadcasted_iota(jnp.int32, ig.shape, 0)
        is_pad = trow < jnp.where(c == 0, pad, 0)
        ig = jnp.where(is_pad, -1e30, ig)
        lf = jnp.where(is_pad, 0.0, lf)
    bcs = _row_scan(lf, jnp.add, 0.0)
    gg = ig - bcs
    cm = _row_scan(gg, jnp.maximum, -jnp.inf)
    m0 = m_ref[0:1, :]
    mm = jnp.maximum(m0, cm)
    mt = bcs + mm
    b_last = bcs[lt - 1:lt, :]
    m_last = mt[lt - 1:lt, :]
    col_s[0] = mm
    col_s[1] = jnp.exp(m0 - mm)
    col_s[2] = jnp.exp(-mt)
    row_s[0] = gg.T
    row_s[1] = jnp.exp(b_last - m_last + gg).T
    row_s[2] = jnp.broadcast_to(jnp.exp(b_last + m0 - m_last), (lt, LANES)).T
    m_ref[...] = jnp.broadcast_to(m_last, m_ref.shape)

    rg = {}

    def rg_conv():
        rg["xc"] = conv(_slabs(px_s, 0), rcw_ref, rcb_ref, rgc_ref)
        if pad:
            rowi = lax.broadcasted_iota(jnp.int32, (rows, 1), 0)
            rg["keep"] = rowi >= jnp.where(c == 0, pad * nb, 0)

    def rg_gates(g):
        a, gx = _rg_gates(rg["xc"], wa_ref, wx_ref, rba_ref, rbx_ref, lam_ref, g)
        if pad:
            gx = jnp.where(rg["keep"], gx, 0.0)
        a_s[g] = a
        g_s[g] = gx

    def rg_scan(g):
        sl = slice(g * LANES, (g + 1) * LANES)
        gate = _gelu(pg_s[g])
        h = rgh_ref[:, sl]
        for t in range(lt):
            rs = slice(t * nb, (t + 1) * nb)
            h = a_s[g, rs, :] * h + g_s[g, rs, :]
            yb_s[g, pl.ds(t, nb, stride=PITCH), :] = h * gate[rs]
        rgh_ref[:, sl] = h

    def rg_out(g):
        for b in range(nb):
            y_ref[b, :, g * LANES:(g + 1) * LANES] = yb_s[g, b * PITCH:b * PITCH + lt, :].astype(BF16)

    def gate_out(b):
        trows = slice(b * lt, (b + 1) * lt)
        for h in range(ML_HEADS):
            sl = slice(h * ML_HD, (h + 1) * ML_HD)
            hm = _sigmoid(po_s[trows, sl]) * ho_s[b * ML_HEADS + h]
            hm = hm * lax.rsqrt(jnp.mean(hm * hm, axis=-1, keepdims=True) + EPS)
            y_ref[b, :, D_RG + h * ML_HD:D_RG + (h + 1) * ML_HD] = (hm * mng_ref[:, sl]).astype(BF16)

    pieces = [[] for _ in range(nb + 1)]
    pieces[0].append(rg_conv)
    for g in range(N_GROUPS):
        pieces[g].append(functools.partial(rg_gates, g))
        pieces[g + 1].append(functools.partial(rg_scan, g))
        pieces[g + 2].append(functools.partial(rg_out, g))
    pieces[0].append(functools.partial(_project, u, w_ref, 1, pg_s, 0, nb, lt))
    pieces[1].append(functools.partial(_project, u, w_ref, 3, po_s, 0, nb, lt))
    for b in range(nb):
        pieces[max(b + 1, 2)].append(functools.partial(gate_out, b))

    t_i = lax.broadcasted_iota(jnp.int32, (lt, lt), 0)
    s_i = lax.broadcasted_iota(jnp.int32, (lt, lt), 1)
    causal = s_i <= t_i

    for b in range(nb):
        for piece in pieces[b]:
            piece()
        trows = slice(b * lt, (b + 1) * lt)
        heads = range(ML_HEADS)
        qb = [q_s[h, trows, :].astype(BF16) for h in heads]
        kt = [kt_s[h, b] for h in heads]
        s0 = [st_ref[b * ML_HEADS + h] for h in heads]
        sc = [jnp.dot(qb[h], kt[h].astype(BF16), preferred_element_type=F32) for h in heads]
        qs = [jnp.dot(qb[h], s0[h].astype(BF16), preferred_element_type=F32) for h in heads]
        pm = []
        for h in heads:
            r = b * GATE_ROWS + h
            d = jnp.exp(jnp.where(causal, row_s[0, r:r + 1, :] - col_s[0, :, r:r + 1], -jnp.inf))
            pm.append((sc[h] * d).astype(BF16))
        pv = [jnp.dot(pm[h], v_s[h, trows, :], preferred_element_type=F32) for h in heads]
        for h in heads:
            r = b * GATE_ROWS + h
            numx = pv[h] + col_s[1, :, r:r + 1] * qs[h]
            den = numx[:, ML_HD:ML_HD + 1]
            ho_s[b * ML_HEADS + h] = numx[:, 0:ML_HD] / jnp.maximum(jnp.abs(den), col_s[2, :, r:r + 1])
        for h in heads:
            r = b * GATE_ROWS + h
            wkt = (kt[h] * row_s[1, r:r + 1, :]).astype(BF16)
            sc_row = jnp.concatenate([row_s[2, r:r + 1, :], row_s[2, r:r + 1, :]], axis=1)
            st_ref[b * ML_HEADS + h] = sc_row * s0[h] + jnp.dot(wkt, v_s[h, trows, :], preferred_element_type=F32)
    for piece in pieces[nb]:
        piece()

    if pad:
        @pl.when(c == 0)
        def _zero_pad_rows():
            y_ref[:, 0:pad, :] = jnp.zeros((nb, pad, y_ref.shape[2]), BF16)


def _pmix(x, meta, lw, l, *, pad, lead, name):
    nb, t, d = x.shape
    nc = t // CHUNK + (1 if lead else 0)
    rows = nb * CHUNK
    ntail = (CONV_W - 1) * nb
    in_arrays = [x, meta, lw["ln1"], lw["w_in"], lw["wgi"], lw["wgf"],
                 lw["rcw"], lw["rcb"], lw["wa"], lw["wx"], lw["rba"], lw["rbx"], lw["lam"],
                 lw["mcw"], lw["mcb"], lw["wq"], lw["wkt"], lw["wv"], lw["bi"], lw["bf"], lw["mng"]]
    x_map = (lambda c: (0, jnp.maximum(c - 1, 0), 0)) if lead else (lambda c: (0, c, 0))
    in_specs = [pl.BlockSpec((nb, CHUNK, d), x_map), _const_spec(meta.shape)] + [
        _layer_spec(a, l) for a in in_arrays[2:]]
    in_specs[3] = pl.BlockSpec((None, d, D_MAIN), lambda c: (l, 0, 0), pipeline_mode=pl.Buffered(1))
    out_shape = [
        jax.ShapeDtypeStruct((nb, nc * CHUNK, D_MODEL), BF16),
        jax.ShapeDtypeStruct((nb, D_RG), F32),
        jax.ShapeDtypeStruct((ntail, D_RG), F32),
        jax.ShapeDtypeStruct((ntail, D_ML), F32),
        jax.ShapeDtypeStruct((nb * ML_HEADS, ML_HD, S_COLS), F32),
        jax.ShapeDtypeStruct((SUBLANES, LANES), F32),
    ]
    out_specs = [pl.BlockSpec((nb, CHUNK, D_MODEL), lambda c: (0, c, 0))] + [
        pl.BlockSpec(s.shape, lambda c, n=len(s.shape): (0,) * n) for s in out_shape[1:]]
    scratch = [
        pltpu.VMEM((N_GROUPS, rows, LANES), F32),
        pltpu.VMEM((N_GROUPS, rows, LANES), F32),
        pltpu.VMEM((N_GROUPS, rows, LANES), F32),
        pltpu.VMEM((rows, D_ML), F32),
        pltpu.VMEM((N_GROUPS, rows, LANES), F32),
        pltpu.VMEM((N_GROUPS, rows, LANES), F32),
        pltpu.VMEM((N_GROUPS, nb * PITCH, LANES), F32),
        pltpu.VMEM((N_GROUPS, nb * PITCH, LANES), F32),
        pltpu.VMEM((N_GROUPS, nb * PITCH, LANES), F32),
        pltpu.VMEM((ML_HEADS, rows, ML_HD), F32),
        pltpu.VMEM((ML_HEADS, nb, ML_HD, CHUNK), F32),
        pltpu.VMEM((ML_HEADS, rows, S_COLS), BF16),
        pltpu.VMEM((nb * ML_HEADS, CHUNK, ML_HD), F32),
        pltpu.VMEM((3, CHUNK, LANES), F32),
        pltpu.VMEM((3, LANES, CHUNK), F32),
    ]
    return pl.pallas_call(
        functools.partial(_pmix_body, pad=pad, lead=lead),
        grid=(nc,),
        in_specs=in_specs,
        out_specs=out_specs,
        out_shape=out_shape,
        scratch_shapes=scratch,
        compiler_params=pltpu.CompilerParams(dimension_semantics=("arbitrary",), vmem_limit_bytes=VMEM_LIMIT),
        name=name,
    )(*in_arrays)


SBLK = SUBLANES


def _smix_body(ptm_ref, pbm_ref, gi_ref, gf_ref, rgh_ref, rgc_ref, mcv_ref, c_ref, n_ref, mcol_ref, mrow_ref,
               rcw_ref, rcb_ref, wa_ref, wx_ref, rba_ref, rbx_ref, lam_ref,
               mcw_ref, mcb_ref, wq_ref, wk_ref, wv_ref, bi_ref, bf_ref, bic_ref, bfc_ref, mng_ref,
               y_ref, rgh_o, rgc_o, mcv_o, n_o, m_o, inter_o, dk_o, v_o,
               q_s, qc_s, col_s):
    i = pl.program_id(0)
    ns = pbm_ref.shape[0]

    def conv(first, w_ref, b_ref, tail_ref, tail_o):
        x = _slabs(ptm_ref, first)
        acc = b_ref[...] + w_ref[CONV_W - 1:CONV_W, :] * x
        for j in range(CONV_W - 1):
            acc = acc + w_ref[j:j + 1, :] * tail_ref[j]
        for j in range(CONV_W - 2):
            tail_o[j] = tail_ref[j + 1]
        tail_o[CONV_W - 2] = x
        return acc, x

    @pl.when(i == 0)
    def _rowwise():
        xc, _ = conv(0, rcw_ref, rcb_ref, rgc_ref, rgc_o)
        for g in range(N_GROUPS):
            sl = slice(g * LANES, (g + 1) * LANES)
            a, gx = _rg_gates(xc, wa_ref, wx_ref, rba_ref, rbx_ref, lam_ref, g)
            hn = a * rgh_ref[:, sl] + gx
            rgh_o[:, sl] = hn
            y_ref[0, :, sl] = (hn * _gelu(ptm_ref[N_GROUPS + g])).astype(BF16)

        mconv, mx = conv(2 * N_GROUPS, mcw_ref, mcb_ref, mcv_ref, mcv_o)
        mcb = (mconv * _sigmoid(mconv)).astype(BF16)
        mxb = mx.astype(BF16)

        ig_r = gi_ref[...] + bi_ref[...]
        lf_r = jax.nn.log_sigmoid(gf_ref[...] + bf_ref[...])
        m0_r = mrow_ref[...]
        m_o[...] = jnp.maximum(lf_r + m0_r, ig_r)
        z = jnp.concatenate([gi_ref[...], gf_ref[...], jnp.zeros((ns - 2 * GATE_ROWS, ns), F32)], axis=0).T
        ig_c = z[:, 0:GATE_ROWS] + bic_ref[...]
        lf_c = jax.nn.log_sigmoid(z[:, GATE_ROWS:2 * GATE_ROWS] + bfc_ref[...])
        m0_c = mcol_ref[...]
        m_c = jnp.maximum(lf_c + m0_c, ig_c)
        inter_c = jnp.exp(lf_c + m0_c - m_c)
        dd_c = jnp.exp(ig_c - m_c)
        inter_o[...] = inter_c
        col_s[0] = inter_c
        col_s[1] = jnp.exp(-m_c)
        for h in range(ML_HEADS):
            sl = slice(h * ML_HD, (h + 1) * ML_HD)
            q = jnp.dot(mcb[:, sl], wq_ref[h], preferred_element_type=F32) * (ML_HD ** -0.5)
            k = jnp.dot(mcb[:, sl], wk_ref[h], preferred_element_type=F32)
            v = jnp.dot(mxb[:, sl], wv_ref[h], preferred_element_type=F32)
            q_s[h] = q
            dk_o[h] = dd_c[:, h:h + 1] * k
            v_o[h] = v
            col_s[2, :, h:h + 1] = jnp.sum(q * k, axis=-1, keepdims=True) * dd_c[:, h:h + 1]
            n0 = n_ref[h]
            col_s[3, :, h:h + 1] = jnp.sum(q * n0, axis=-1, keepdims=True)
            n_o[h] = inter_c[:, h:h + 1] * n0 + dd_c[:, h:h + 1] * k

    blk = pl.ds(pl.multiple_of(i * SBLK, SBLK), SBLK)
    for h in range(ML_HEADS):
        z = jnp.concatenate([q_s[h, blk, :], jnp.zeros((ML_HD - SBLK, ML_HD), F32)], axis=0).T
        qc_s[h, blk, :] = jnp.concatenate(
            [jnp.sum(z[:, j:j + 1] * c_ref[0, j, h], axis=0, keepdims=True) for j in range(SBLK)], axis=0)

    @pl.when(i == pl.num_programs(0) - 1)
    def _finish():
        for h in range(ML_HEADS):
            sl = slice(h * ML_HD, (h + 1) * ML_HD)
            inter = col_s[0, :, h:h + 1]
            s = col_s[2, :, h:h + 1]
            num = s * v_o[h] + inter * qc_s[h]
            den = s + inter * col_s[3, :, h:h + 1]
            hout = num / jnp.maximum(jnp.abs(den), col_s[1, :, h:h + 1])
            hm = _sigmoid(pbm_ref[:, sl]) * hout
            hm = hm * lax.rsqrt(jnp.mean(hm * hm, axis=-1, keepdims=True) + EPS)
            y_ref[0, :, D_RG + h * ML_HD:D_RG + (h + 1) * ML_HD] = (hm * mng_ref[:, sl]).astype(BF16)


def _smix(ptm, pbm, gi, gf, st, lw, l, *, name):
    ns = pbm.shape[0]
    rgh, rgc, mcv, mc_all, mn, mcol, mrow = st
    in_arrays = [ptm, pbm, gi, gf, rgh, rgc, mcv, mc_all, mn, mcol, mrow,
                 lw["rcw"], lw["rcb"], lw["wa"], lw["wx"], lw["rba"], lw["rbx"], lw["lam"],
                 lw["mcw"], lw["mcb"], lw["wq"], lw["wk"], lw["wv"], lw["bi_s"], lw["bf_s"],
                 lw["bi_c"], lw["bf_c"], lw["mng"]]
    in_specs = [_const_spec(a.shape) for a in in_arrays[:4]] + [_layer_spec(a, l) for a in in_arrays[4:]]
    in_specs[7] = pl.BlockSpec((1, SBLK, ML_HEADS, ML_HD, ML_HD), lambda i: (l, i, 0, 0, 0))
    out_shape = [
        jax.ShapeDtypeStruct((1, ns, D_MODEL), BF16),
        jax.ShapeDtypeStruct((ns, D_RG), F32),
        jax.ShapeDtypeStruct((CONV_W - 1, ns, D_RG), F32),
        jax.ShapeDtypeStruct((CONV_W - 1, ns, D_ML), F32),
        jax.ShapeDtypeStruct((ML_HEADS, ns, ML_HD), F32),
        jax.ShapeDtypeStruct((GATE_ROWS, ns), F32),
        jax.ShapeDtypeStruct((ns, GATE_ROWS), F32),
        jax.ShapeDtypeStruct((ML_HEADS, ns, ML_HD), F32),
        jax.ShapeDtypeStruct((ML_HEADS, ns, ML_HD), F32),
    ]
    out_specs = [pl.BlockSpec(s.shape, lambda i, n=len(s.shape): (0,) * n) for s in out_shape]
    scratch = [
        pltpu.VMEM((ML_HEADS, ns, ML_HD), F32),
        pltpu.VMEM((ML_HEADS, ns, ML_HD), F32),
        pltpu.VMEM((4, ns, GATE_ROWS), F32),
    ]
    return pl.pallas_call(
        _smix_body,
        grid=(ns // SBLK,),
        in_specs=in_specs,
        out_specs=out_specs,
        out_shape=out_shape,
        scratch_shapes=scratch,
        compiler_params=pltpu.CompilerParams(dimension_semantics=("arbitrary",), vmem_limit_bytes=VMEM_LIMIT),
        name=name,
    )(*in_arrays)


CBLK = 2 * SUBLANES


def _cupdate_body(c_ref, inter_ref, dk_ref, v_ref, o_ref):
    for h in range(ML_HEADS):
        z = jnp.concatenate([dk_ref[0, h], jnp.zeros((ML_HD - CBLK, ML_HD), F32)], axis=0).T
        inter_rows = jnp.broadcast_to(inter_ref[0, :, h:h + 1], (CBLK, LANES))
        vblk = v_ref[0, h]
        for j in range(CBLK):
            o_ref[0, j, h] = inter_rows[j:j + 1, :] * c_ref[0, j, h] + z[:, j:j + 1] * vblk[j:j + 1, :]


def _cupdate(mc_all, inter_all, dk_all, v_all):
    depth, ns = mc_all.shape[:2]
    cspec = pl.BlockSpec((1, CBLK, ML_HEADS, ML_HD, ML_HD), lambda l, i: (l, i, 0, 0, 0))
    kvspec = pl.BlockSpec((1, ML_HEADS, CBLK, ML_HD), lambda l, i: (l, 0, i, 0))
    return pl.pallas_call(
        _cupdate_body,
        grid=(depth, ns // CBLK),
        in_specs=[cspec, pl.BlockSpec((1, CBLK, GATE_ROWS), lambda l, i: (l, i, 0)), kvspec, kvspec],
        out_specs=cspec,
        out_shape=jax.ShapeDtypeStruct(mc_all.shape, F32),
        compiler_params=pltpu.CompilerParams(dimension_semantics=("arbitrary", "arbitrary"),
                                             vmem_limit_bytes=VMEM_LIMIT),
        name="cupdate_s",
    )(mc_all, inter_all, dk_all, v_all)


def _block_diag_pairs(w):
    depth = w.shape[0]
    w = w.reshape(depth, N_GROUPS, 2, RG_BLOCK, RG_BLOCK)
    zero = jnp.zeros((depth, N_GROUPS, RG_BLOCK, RG_BLOCK), w.dtype)
    top = jnp.concatenate([w[:, :, 0], zero], axis=-1)
    bottom = jnp.concatenate([zero, w[:, :, 1]], axis=-1)
    return jnp.concatenate([top, bottom], axis=-2)


def _gate_rows(w_cols):
    depth = w_cols.shape[0]
    return jnp.concatenate([jnp.swapaxes(w_cols, 1, 2),
                            jnp.zeros((depth, GATE_ROWS - ML_HEADS, D_MODEL), w_cols.dtype)], axis=1)


def _stacked_weights(ln1_g, w_in, rg_conv_w, rg_conv_b, rg_w_a, rg_w_x, rg_b_a, rg_b_x, rg_lambda,
                     m_conv_w, m_conv_b, m_w_q, m_w_k, m_w_v, m_b_i, m_b_f, m_norm_g, w_out, ln2_g,
                     w_ff1, w_ff2, nb_prompt, ns):
    depth = ln1_g.shape[0]
    row = lambda a: a.reshape(depth, 1, -1)
    pad8 = lambda a: jnp.concatenate([a, jnp.zeros((depth, GATE_ROWS - ML_HEADS), F32)], axis=1)
    bias8, bfor8 = pad8(m_b_i), pad8(m_b_f)
    lanes = lambda a: jnp.concatenate([jnp.tile(a, (1, nb_prompt)),
                                       jnp.zeros((depth, LANES - nb_prompt * GATE_ROWS), F32)], axis=1)[:, None, :]
    return dict(
        ln1=row(ln1_g),
        w_in=jnp.pad(w_in, ((0, 0), (0, 0), (0, -w_in.shape[2] % LANES))).astype(BF16),
        wgi=_gate_rows(w_in[:, :, D_MAIN:D_MAIN + ML_HEADS]).astype(BF16),
        wgf=_gate_rows(w_in[:, :, D_MAIN + ML_HEADS:]).astype(BF16),
        rcw=rg_conv_w, rcb=row(rg_conv_b),
        wa=_block_diag_pairs(rg_w_a).astype(BF16), wx=_block_diag_pairs(rg_w_x).astype(BF16),
        rba=row(rg_b_a), rbx=row(rg_b_x), lam=row(rg_lambda),
        mcw=m_conv_w, mcb=row(m_conv_b),
        wq=m_w_q.astype(BF16), wk=m_w_k.astype(BF16), wkt=jnp.swapaxes(m_w_k, 2, 3).astype(BF16),
        wv=m_w_v.astype(BF16),
        bi=lanes(bias8), bf=lanes(bfor8),
        bi_s=jnp.broadcast_to(bias8[:, :, None], (depth, GATE_ROWS, ns)),
        bf_s=jnp.broadcast_to(bfor8[:, :, None], (depth, GATE_ROWS, ns)),
        bi_c=bias8.reshape(depth, 1, GATE_ROWS), bf_c=bfor8.reshape(depth, 1, GATE_ROWS),
        mng=row(m_norm_g),
        w_out=w_out.astype(BF16), ln2=row(ln2_g),
        w_ff1=w_ff1.astype(BF16), w_ff2=w_ff2.astype(BF16),
    )


def kernel(x_prompt, x_sample, state_rg_h, state_rg_conv, state_m_conv, state_m_C, state_m_n, state_m_m,
           meta_tokens, ln1_g, w_in, rg_conv_w, rg_conv_b, rg_w_a, rg_w_x, rg_b_a, rg_b_x, rg_lambda,
           m_conv_w, m_conv_b, m_w_q, m_w_k, m_w_v, m_b_i, m_b_f, m_norm_g, w_out, ln2_g,
           w_ff1, w_ff2, ln_f_g):
    nb, seq, d = x_prompt.shape
    ns = x_sample.shape[0]
    depth = ln1_g.shape[0]
    t_real = N_META + seq
    nc = -(-t_real // CHUNK)
    pad = nc * CHUNK - t_real
    assert (pad + N_META) % CHUNK == 0 and x_sample.shape[1] == 1 and ns == LANES and depth == DEPTH
    assert CHUNK == LANES and nb * GATE_ROWS <= LANES

    lw = _stacked_weights(ln1_g, w_in, rg_conv_w, rg_conv_b, rg_w_a, rg_w_x, rg_b_a, rg_b_x, rg_lambda,
                          m_conv_w, m_conv_b, m_w_q, m_w_k, m_w_v, m_b_i, m_b_f, m_norm_g, w_out, ln2_g,
                          w_ff1, w_ff2, nb, ns)
    lnf = ln_f_g.reshape(1, d)
    meta = meta_tokens.astype(F32)
    xp = x_prompt
    xs = x_sample.reshape(1, ns, d)
    m_cols = jnp.concatenate([state_m_m, jnp.zeros((depth, ns, GATE_ROWS - ML_HEADS), F32)], axis=2)
    st_in = (state_rg_h, jnp.swapaxes(state_rg_conv, 1, 2), jnp.swapaxes(state_m_conv, 1, 2), state_m_C,
             jnp.swapaxes(state_m_n, 1, 2), m_cols, jnp.swapaxes(m_cols, 1, 2))

    p_states = [[] for _ in range(6)]
    s_states = [[] for _ in range(6)]
    for l in range(depth):
        last = l == depth - 1

        y, rgh, rgc, mcv, st, mrow = _pmix(xp, meta, lw, l, pad=pad, lead=l == 0, name=f"mixer_p{l}")
        lt_out = CHUNK // 2
        skip = (pad + N_META) // lt_out if last else 0
        xp = _outmlp(xp, y, lw, l, lnf, lt=lt_out, final=last, skip=skip,
                     nzero=0 if last else pad // lt_out, name=f"outmlp_p{l}", meta=meta if l == 0 else None)
        st = st.reshape(nb, ML_HEADS, ML_HD, S_COLS)
        p_states[0].append(rgh)
        p_states[1].append(jnp.swapaxes(rgc.reshape(CONV_W - 1, nb, D_RG), 0, 1))
        p_states[2].append(jnp.swapaxes(mcv.reshape(CONV_W - 1, nb, D_ML), 0, 1))
        p_states[3].append(st[..., :ML_HD])
        p_states[4].append(st[..., ML_HD])
        p_states[5].append(mrow[0, :nb * GATE_ROWS].reshape(nb, GATE_ROWS)[:, :ML_HEADS])

        stm, sbm, gis, gfs = _inproj(xs, lw, l, lt=ns, name=f"inproj_s{l}")
        ys, srgh, srgc, smcv, smn, smrow, sinter, sdk, sv = _smix(stm, sbm, gis, gfs, st_in, lw, l,
                                                                  name=f"mixer_s{l}")
        xs = _outmlp(xs, ys, lw, l, lnf, lt=ns, final=last, skip=0, nzero=0, name=f"outmlp_s{l}")
        s_states[0].append(srgh)
        s_states[1].append(jnp.swapaxes(srgc, 0, 1))
        s_states[2].append(jnp.swapaxes(smcv, 0, 1))
        s_states[3].append((sinter, sdk, sv))
        s_states[4].append(jnp.swapaxes(smn, 0, 1))
        s_states[5].append(smrow[:ML_HEADS].T)

    y_prompt = xp
    y_sample = xs.reshape(ns, 1, d)
    ps_out = [jnp.stack(s) for s in p_states]
    c_new = _cupdate(state_m_C, *(jnp.stack([t[j] for t in s_states[3]]) for j in range(3)))
    ss_out = [c_new if j == 3 else jnp.stack(s) for j, s in enumerate(s_states)]
    return (y_prompt, y_sample, *ps_out, *ss_out)
```

```python
import functools

import jax
import jax.numpy as jnp
from jax import lax
from jax.experimental import pallas as pl
from jax.experimental.pallas import tpu as pltpu

F32 = jnp.float32
BF16 = jnp.bfloat16

D_MODEL = 1024
N_META = 16
D_RG = 512
RG_BLOCK = 64
RG_C = 8.0
D_ML = 512
ML_HEADS = 4
ML_HD = 128
CONV_W = 4
D_FF = 4096
EPS = 1e-6
DEPTH = 4

LANES = 128
SUBLANES = 8
CHUNK = 128
PITCH = CHUNK + SUBLANES
N_GROUPS = D_RG // LANES
D_MAIN = 2 * D_RG + 2 * D_ML
GATE_ROWS = SUBLANES
S_COLS = 2 * ML_HD
N_SLABS = 3 * N_GROUPS
FF_TILE = 1024
VMEM_LIMIT = 56 * 1024 * 1024

_NT = (((1,), (1,)), ((), ()))


def _const_spec(shape):
    zeros = (0,) * len(shape)
    return pl.BlockSpec(shape, lambda *_: zeros, pipeline_mode=pl.Buffered(1))


def _layer_spec(arr, l):
    tail = (0,) * (arr.ndim - 1)
    return pl.BlockSpec((None,) + arr.shape[1:], lambda *_: (l,) + tail, pipeline_mode=pl.Buffered(1))


def _rms(x, g):
    return x * lax.rsqrt(jnp.mean(x * x, axis=-1, keepdims=True) + EPS) * g


def _sigmoid(x):
    return 0.5 * jnp.tanh(0.5 * x) + 0.5


_GELU_K0 = 0.7978845608028654
_GELU_K1 = _GELU_K0 * 0.044715


def _gelu(x):
    hx = 0.5 * x
    return hx * jnp.tanh(x * (_GELU_K0 + _GELU_K1 * (x * x))) + hx


def _sqrt_nonneg(s):
    return jnp.where(s > 0.0, s * lax.rsqrt(s), 0.0)


def _project(u, w_ref, n, dst_ref, base, nb, lt):
    pr = jnp.dot(u, w_ref[:, n * 512:(n + 1) * 512], preferred_element_type=F32)
    if n == D_MAIN // 512 - 1:
        dst_ref[...] = pr
        return
    for g in range(N_GROUPS):
        cols = slice(g * LANES, (g + 1) * LANES)
        if nb == 1:
            dst_ref[base + g] = pr[:, cols]
            continue
        for b in range(nb):
            dst_ref[base + g, pl.ds(b, lt, stride=nb), :] = pr[b * lt:(b + 1) * lt, cols]


def _gate_preacts(u, wg_ref, nb, lt):
    return jnp.concatenate([lax.dot_general(wg_ref[...], u[b * lt:(b + 1) * lt], _NT, preferred_element_type=F32)
                            for b in range(nb)], axis=0)


def _inproj_body(x_ref, g_ref, w_ref, wgi_ref, wgf_ref, ptm_ref, pbm_ref, gi_ref, gf_ref):
    nb, lt, d = x_ref.shape
    x = x_ref[...].reshape(nb * lt, d)
    u = _rms(x, g_ref[...]).astype(BF16)
    for n in range(D_MAIN // 512 - 1):
        _project(u, w_ref, n, ptm_ref, n * N_GROUPS, nb, lt)
    _project(u, w_ref, D_MAIN // 512 - 1, pbm_ref, 0, nb, lt)
    gi_ref[...] = _gate_preacts(u, wgi_ref, nb, lt)
    gf_ref[...] = _gate_preacts(u, wgf_ref, nb, lt)


def _inproj(x, lw, l, *, lt, name):
    nb, t, d = x.shape
    nc = t // lt
    rows = nb * lt
    return pl.pallas_call(
        _inproj_body,
        grid=(nc,),
        in_specs=[
            pl.BlockSpec((nb, lt, d), lambda c: (0, c, 0)),
            _layer_spec(lw["ln1"], l),
            pl.BlockSpec((None, d, D_MAIN), lambda c: (l, 0, 0), pipeline_mode=pl.Buffered(1)),
            _layer_spec(lw["wgi"], l),
            _layer_spec(lw["wgf"], l),
        ],
        out_specs=[
            pl.BlockSpec((N_SLABS, rows, LANES), lambda c: (0, c, 0)),
            pl.BlockSpec((rows, D_ML), lambda c: (c, 0)),
            pl.BlockSpec((nb * GATE_ROWS, lt), lambda c: (c, 0)),
            pl.BlockSpec((nb * GATE_ROWS, lt), lambda c: (c, 0)),
        ],
        out_shape=[
            jax.ShapeDtypeStruct((N_SLABS, nc * rows, LANES), F32),
            jax.ShapeDtypeStruct((nc * rows, D_ML), F32),
            jax.ShapeDtypeStruct((nc * nb * GATE_ROWS, lt), F32),
            jax.ShapeDtypeStruct((nc * nb * GATE_ROWS, lt), F32),
        ],
        compiler_params=pltpu.CompilerParams(dimension_semantics=("arbitrary",), vmem_limit_bytes=VMEM_LIMIT),
        name=name,
    )(x, lw["ln1"], lw["w_in"], lw["wgi"], lw["wgf"])


def _lead_block(meta_ref, nb, lt):
    meta = meta_ref[...]
    blk = jnp.concatenate([jnp.zeros((lt - meta.shape[0], meta.shape[1]), F32), meta], axis=0)
    return jnp.broadcast_to(blk[None], (nb, lt, meta.shape[1]))


def _outmlp_body(x_ref, y_ref, wo_ref, g2_ref, w1_ref, w2_ref, gf_ref, *rest, final, nzero, lead):
    o_ref = rest[-1]
    nb, lt, d = y_ref.shape
    rows = nb * lt

    def compute():
        x = x_ref[...]
        if lead:
            x = jnp.where(pl.program_id(0) == nzero, _lead_block(rest[0], nb, lt), x)
        x = x.reshape(rows, d)
        y = y_ref[...].reshape(rows, d)
        x1 = x + jnp.dot(y, wo_ref[...], preferred_element_type=F32)
        u2 = _rms(x1, g2_ref[...]).astype(BF16)
        acc = x1
        for c in range(D_FF // FF_TILE):
            cols = slice(c * FF_TILE, (c + 1) * FF_TILE)
            h = jnp.dot(u2, w1_ref[:, cols], preferred_element_type=F32)
            h = jnp.square(jnp.maximum(h, 0.0)).astype(BF16)
            acc = acc + jnp.dot(h, w2_ref[cols, :], preferred_element_type=F32)
        if final:
            acc = _rms(acc, gf_ref[...])
        o_ref[...] = acc.reshape(nb, lt, d)

    def zero():
        o_ref[...] = jnp.zeros(o_ref.shape, F32)

    if nzero:
        i = pl.program_id(0)
        pl.when(i < nzero)(zero)
        pl.when(i >= nzero)(compute)
    else:
        compute()


def _outmlp(x, y, lw, l, gf, *, lt, final, skip, nzero, name, meta=None):
    nb, t, d = y.shape
    nsteps = t // lt - skip
    lead = meta is not None
    x_map = (lambda i: (0, jnp.maximum(i - nzero - 1, 0), 0)) if lead else (lambda i: (0, i + skip, 0))
    return pl.pallas_call(
        functools.partial(_outmlp_body, final=final, nzero=nzero, lead=lead),
        grid=(nsteps,),
        in_specs=[
            pl.BlockSpec((nb, lt, d), x_map),
            pl.BlockSpec((nb, lt, d), lambda i: (0, i + skip, 0)),
            _layer_spec(lw["w_out"], l),
            _layer_spec(lw["ln2"], l),
            _layer_spec(lw["w_ff1"], l),
            _layer_spec(lw["w_ff2"], l),
            _const_spec((1, d)),
        ] + ([_const_spec(meta.shape)] if lead else []),
        out_specs=pl.BlockSpec((nb, lt, d), lambda i: (0, i, 0)),
        out_shape=jax.ShapeDtypeStruct((nb, nsteps * lt, d), F32),
        compiler_params=pltpu.CompilerParams(dimension_semantics=("arbitrary",), vmem_limit_bytes=VMEM_LIMIT),
        name=name,
    )(x, y, lw["w_out"], lw["ln2"], lw["w_ff1"], lw["w_ff2"], gf, *([meta] if lead else []))


def _rg_gates(xc, wa_ref, wx_ref, rba_ref, rbx_ref, lam_ref, g):
    sl = slice(g * LANES, (g + 1) * LANES)
    xg = xc[:, sl]
    xb = xg.astype(BF16)
    i = _sigmoid(jnp.dot(xb, wx_ref[g], preferred_element_type=F32) + rbx_ref[:, sl])
    c = (0.5 * RG_C) * jax.nn.log_sigmoid(lam_ref[:, sl])
    pre = jnp.dot(xb, wa_ref[g], preferred_element_type=F32) + rba_ref[:, sl]
    a = jnp.exp(c * jnp.tanh(0.5 * pre) + c)
    gx = _sqrt_nonneg(1.0 - a * a) * (i * xg)
    return a, gx


def _row_scan(x, op, fill):
    row = lax.broadcasted_iota(jnp.int32, x.shape, 0)
    sh = 1
    while sh < x.shape[0]:
        x = op(x, jnp.where(row >= sh, pltpu.roll(x, sh, 0), fill))
        sh *= 2
    return x


def _slabs(ptm_ref, first):
    return jnp.concatenate([ptm_ref[first + g] for g in range(N_GROUPS)], axis=1)


def _pmix_body(x_ref, meta_ref, g1_ref, w_ref, wgi_ref, wgf_ref,
               rcw_ref, rcb_ref, wa_ref, wx_ref, rba_ref, rbx_ref, lam_ref,
               mcw_ref, mcb_ref, wq_ref, wkt_ref, wv_ref, bi_ref, bf_ref, mng_ref,
               y_ref, rgh_ref, rgc_ref, mcv_ref, st_ref, m_ref,
               px_s, pg_s, pm_s, po_s, a_s, g_s, bmc_s, bmx_s, yb_s, q_s, kt_s, v_s, ho_s, col_s, row_s, *, pad, lead):
    nb = y_ref.shape[0]
    lt = CHUNK
    rows = nb * lt
    ntail = (CONV_W - 1) * nb
    c = pl.program_id(0)

    @pl.when(c == 0)
    def _init():
        rgh_ref[...] = jnp.zeros_like(rgh_ref)
        rgc_ref[...] = jnp.zeros_like(rgc_ref)
        mcv_ref[...] = jnp.zeros_like(mcv_ref)
        st_ref[...] = jnp.zeros_like(st_ref)
        m_ref[...] = jnp.zeros_like(m_ref)
        ones_col = (lax.broadcasted_iota(jnp.int32, (rows, ML_HD), 1) == 0).astype(BF16)
        for h in range(ML_HEADS):
            v_s[h, :, ML_HD:S_COLS] = ones_col

    def conv(x, w_ref, b_ref, tail_ref):
        tail = tail_ref[...]
        acc = b_ref[...] + w_ref[CONV_W - 1:CONV_W, :] * x
        for k in range(1, CONV_W):
            shifted = jnp.concatenate([tail[ntail - k * nb:], x[:rows - k * nb]], axis=0)
            acc = acc + w_ref[CONV_W - 1 - k:CONV_W - k, :] * shifted
        tail_ref[...] = x[rows - ntail:]
        return acc

    x = x_ref[...]
    if lead:
        x = jnp.where(c == 0, _lead_block(meta_ref, nb, lt), x)
    u = _rms(x.reshape(rows, x.shape[2]), g1_ref[...]).astype(BF16)
    _project(u, w_ref, 2, pm_s, 0, nb, lt)
    gi = _gate_preacts(u, wgi_ref, nb, lt)
    gf = _gate_preacts(u, wgf_ref, nb, lt)

    mx = _slabs(pm_s, 0)
    mconv = conv(mx, mcw_ref, mcb_ref, mcv_ref)
    mc = mconv * _sigmoid(mconv)
    _project(u, w_ref, 0, px_s, 0, nb, lt)
    for t in range(lt):
        rs = slice(t * nb, (t + 1) * nb)
        idx = pl.ds(t, nb, stride=PITCH)
        for g in range(N_GROUPS):
            cols = slice(g * LANES, (g + 1) * LANES)
            bmc_s[g, idx, :] = mc[rs, cols]
            bmx_s[g, idx, :] = mx[rs, cols]

    def seq_major(src):
        return jnp.concatenate(
            [jnp.concatenate([src[g, b * PITCH:b * PITCH + lt, :] for b in range(nb)], axis=0)
             for g in range(N_GROUPS)], axis=1).astype(BF16)

    mcb = seq_major(bmc_s)
    mxb = seq_major(bmx_s)
    for h in range(ML_HEADS):
        sl = slice(h * ML_HD, (h + 1) * ML_HD)
        q_s[h] = jnp.dot(mcb[:, sl], wq_ref[h], preferred_element_type=F32) * (ML_HD ** -0.5)
        kt = lax.dot_general(wkt_ref[h], mcb[:, sl], _NT, preferred_element_type=F32)
        for b in range(nb):
            kt_s[h, b] = kt[:, b * lt:(b + 1) * lt]
        v_s[h, :, 0:ML_HD] = jnp.dot(mxb[:, sl], wv_ref[h], preferred_element_type=F32).astype(BF16)

    def to_cols(r):
        return jnp.concatenate([r, jnp.zeros((LANES - r.shape[0], lt), F32)], axis=0).T

    ig = to_cols(gi) + bi_ref[...]
    lf = jax.nn.log_sigmoid(to_cols(gf) + bf_ref[...])
    if pad:
        trow = lax.broadcasted_iota(jnp.int32, ig.shape, 0)
        is_pad = trow < jnp.where(c == 0, pad, 0)
        ig = jnp.where(is_pad, -1e30, ig)
        lf = jnp.where(is_pad, 0.0, lf)
    bcs = _row_scan(lf, jnp.add, 0.0)
    gg = ig - bcs
    cm = _row_scan(gg, jnp.maximum, -jnp.inf)
    m0 = m_ref[0:1, :]
    mm = jnp.maximum(m0, cm)
    mt = bcs + mm
    b_last = bcs[lt - 1:lt, :]
    m_last = mt[lt - 1:lt, :]
    col_s[0] = mm
    col_s[1] = jnp.exp(m0 - mm)
    col_s[2] = jnp.exp(-mt)
    row_s[0] = gg.T
    row_s[1] = jnp.exp(b_last - m_last + gg).T
    row_s[2] = jnp.broadcast_to(jnp.exp(b_last + m0 - m_last), (lt, LANES)).T
    m_ref[...] = jnp.broadcast_to(m_last, m_ref.shape)

    rg = {}

    def rg_conv():
        rg["xc"] = conv(_slabs(px_s, 0), rcw_ref, rcb_ref, rgc_ref)
        if pad:
            rowi = lax.broadcasted_iota(jnp.int32, (rows, 1), 0)
            rg["keep"] = rowi >= jnp.where(c == 0, pad * nb, 0)

    def rg_gates(g):
        a, gx = _rg_gates(rg["xc"], wa_ref, wx_ref, rba_ref, rbx_ref, lam_ref, g)
        if pad:
            gx = jnp.where(rg["keep"], gx, 0.0)
        a_s[g] = a
        g_s[g] = gx

    def rg_scan(g):
        sl = slice(g * LANES, (g + 1) * LANES)
        gate = _gelu(pg_s[g])
        h = rgh_ref[:, sl]
        for t in range(lt):
            rs = slice(t * nb, (t + 1) * nb)
            h = a_s[g, rs, :] * h + g_s[g, rs, :]
            yb_s[g, pl.ds(t, nb, stride=PITCH), :] = h * gate[rs]
        rgh_ref[:, sl] = h

    def rg_out(g):
        for b in range(nb):
            y_ref[b, :, g * LANES:(g + 1) * LANES] = yb_s[g, b * PITCH:b * PITCH + lt, :].astype(BF16)

    def gate_out(b):
        trows = slice(b * lt, (b + 1) * lt)
        for h in range(ML_HEADS):
            sl = slice(h * ML_HD, (h + 1) * ML_HD)
            hm = _sigmoid(po_s[trows, sl]) * ho_s[b * ML_HEADS + h]
            hm = hm * lax.rsqrt(jnp.mean(hm * hm, axis=-1, keepdims=True) + EPS)
            y_ref[b, :, D_RG + h * ML_HD:D_RG + (h + 1) * ML_HD] = (hm * mng_ref[:, sl]).astype(BF16)

    pieces = [[] for _ in range(nb + 1)]
    pieces[0].append(rg_conv)
    for g in range(N_GROUPS):
        pieces[g].append(functools.partial(rg_gates, g))
        pieces[g + 1].append(functools.partial(rg_scan, g))
        pieces[g + 2].append(functools.partial(rg_out, g))
    pieces[0].append(functools.partial(_project, u, w_ref, 1, pg_s, 0, nb, lt))
    pieces[1].append(functools.partial(_project, u, w_ref, 3, po_s, 0, nb, lt))
    for b in range(nb):
        pieces[max(b + 1, 2)].append(functools.partial(gate_out, b))

    t_i = lax.broadcasted_iota(jnp.int32, (lt, lt), 0)
    s_i = lax.broadcasted_iota(jnp.int32, (lt, lt), 1)
    causal = s_i <= t_i

    for b in range(nb):
        for piece in pieces[b]:
            piece()
        trows = slice(b * lt, (b + 1) * lt)
        heads = range(ML_HEADS)
        qb = [q_s[h, trows, :].astype(BF16) for h in heads]
        kt = [kt_s[h, b] for h in heads]
        s0 = [st_ref[b * ML_HEADS + h] for h in heads]
        sc = [jnp.dot(qb[h], kt[h].astype(BF16), preferred_element_type=F32) for h in heads]
        qs = [jnp.dot(qb[h], s0[h].astype(BF16), preferred_element_type=F32) for h in heads]
        pm = []
        for h in heads:
            r = b * GATE_ROWS + h
            d = jnp.exp(jnp.where(causal, row_s[0, r:r + 1, :] - col_s[0, :, r:r + 1], -jnp.inf))
            pm.append((sc[h] * d).astype(BF16))
        pv = [jnp.dot(pm[h], v_s[h, trows, :], preferred_element_type=F32) for h in heads]
        for h in heads:
            r = b * GATE_ROWS + h
            numx = pv[h] + col_s[1, :, r:r + 1] * qs[h]
            den = numx[:, ML_HD:ML_HD + 1]
            ho_s[b * ML_HEADS + h] = numx[:, 0:ML_HD] / jnp.maximum(jnp.abs(den), col_s[2, :, r:r + 1])
        for h in heads:
            r = b * GATE_ROWS + h
            wkt = (kt[h] * row_s[1, r:r + 1, :]).astype(BF16)
            sc_row = jnp.concatenate([row_s[2, r:r + 1, :], row_s[2, r:r + 1, :]], axis=1)
            st_ref[b * ML_HEADS + h] = sc_row * s0[h] + jnp.dot(wkt, v_s[h, trows, :], preferred_element_type=F32)
    for piece in pieces[nb]:
        piece()

    if pad:
        @pl.when(c == 0)
        def _zero_pad_rows():
            y_ref[:, 0:pad, :] = jnp.zeros((nb, pad, y_ref.shape[2]), BF16)


def _pmix(x, meta, lw, l, *, pad, lead, name):
    nb, t, d = x.shape
    nc = t // CHUNK + (1 if lead else 0)
    rows = nb * CHUNK
    ntail = (CONV_W - 1) * nb
    in_arrays = [x, meta, lw["ln1"], lw["w_in"], lw["wgi"], lw["wgf"],
                 lw["rcw"], lw["rcb"], lw["wa"], lw["wx"], lw["rba"], lw["rbx"], lw["lam"],
                 lw["mcw"], lw["mcb"], lw["wq"], lw["wkt"], lw["wv"], lw["bi"], lw["bf"], lw["mng"]]
    x_map = (lambda c: (0, jnp.maximum(c - 1, 0), 0)) if lead else (lambda c: (0, c, 0))
    in_specs = [pl.BlockSpec((nb, CHUNK, d), x_map), _const_spec(meta.shape)] + [
        _layer_spec(a, l) for a in in_arrays[2:]]
    in_specs[3] = pl.BlockSpec((None, d, D_MAIN), lambda c: (l, 0, 0), pipeline_mode=pl.Buffered(1))
    out_shape = [
        jax.ShapeDtypeStruct((nb, nc * CHUNK, D_MODEL), BF16),
        jax.ShapeDtypeStruct((nb, D_RG), F32),
        jax.ShapeDtypeStruct((ntail, D_RG), F32),
        jax.ShapeDtypeStruct((ntail, D_ML), F32),
        jax.ShapeDtypeStruct((nb * ML_HEADS, ML_HD, S_COLS), F32),
        jax.ShapeDtypeStruct((SUBLANES, LANES), F32),
    ]
    out_specs = [pl.BlockSpec((nb, CHUNK, D_MODEL), lambda c: (0, c, 0))] + [
        pl.BlockSpec(s.shape, lambda c, n=len(s.shape): (0,) * n) for s in out_shape[1:]]
    scratch = [
        pltpu.VMEM((N_GROUPS, rows, LANES), F32),
        pltpu.VMEM((N_GROUPS, rows, LANES), F32),
        pltpu.VMEM((N_GROUPS, rows, LANES), F32),
        pltpu.VMEM((rows, D_ML), F32),
        pltpu.VMEM((N_GROUPS, rows, LANES), F32),
        pltpu.VMEM((N_GROUPS, rows, LANES), F32),
        pltpu.VMEM((N_GROUPS, nb * PITCH, LANES), F32),
        pltpu.VMEM((N_GROUPS, nb * PITCH, LANES), F32),
        pltpu.VMEM((N_GROUPS, nb * PITCH, LANES), F32),
        pltpu.VMEM((ML_HEADS, rows, ML_HD), F32),
        pltpu.VMEM((ML_HEADS, nb, ML_HD, CHUNK), F32),
        pltpu.VMEM((ML_HEADS, rows, S_COLS), BF16),
        pltpu.VMEM((nb * ML_HEADS, CHUNK, ML_HD), F32),
        pltpu.VMEM((3, CHUNK, LANES), F32),
        pltpu.VMEM((3, LANES, CHUNK), F32),
    ]
    return pl.pallas_call(
        functools.partial(_pmix_body, pad=pad, lead=lead),
        grid=(nc,),
        in_specs=in_specs,
        out_specs=out_specs,
        out_shape=out_shape,
        scratch_shapes=scratch,
        compiler_params=pltpu.CompilerParams(dimension_semantics=("arbitrary",), vmem_limit_bytes=VMEM_LIMIT),
        name=name,
    )(*in_arrays)


SBLK = SUBLANES


def _smix_body(ptm_ref, pbm_ref, gi_ref, gf_ref, rgh_ref, rgc_ref, mcv_ref, c_ref, n_ref, mcol_ref, mrow_ref,
               rcw_ref, rcb_ref, wa_ref, wx_ref, rba_ref, rbx_ref, lam_ref,
               mcw_ref, mcb_ref, wq_ref, wk_ref, wv_ref, bi_ref, bf_ref, bic_ref, bfc_ref, mng_ref,
               y_ref, rgh_o, rgc_o, mcv_o, n_o, m_o, inter_o, dk_o, v_o,
               q_s, qc_s, col_s):
    i = pl.program_id(0)
    ns = pbm_ref.shape[0]

    def conv(first, w_ref, b_ref, tail_ref, tail_o):
        x = _slabs(ptm_ref, first)
        acc = b_ref[...] + w_ref[CONV_W - 1:CONV_W, :] * x
        for j in range(CONV_W - 1):
            acc = acc + w_ref[j:j + 1, :] * tail_ref[j]
        for j in range(CONV_W - 2):
            tail_o[j] = tail_ref[j + 1]
        tail_o[CONV_W - 2] = x
        return acc, x

    @pl.when(i == 0)
    def _rowwise():
        xc, _ = conv(0, rcw_ref, rcb_ref, rgc_ref, rgc_o)
        for g in range(N_GROUPS):
            sl = slice(g * LANES, (g + 1) * LANES)
            a, gx = _rg_gates(xc, wa_ref, wx_ref, rba_ref, rbx_ref, lam_ref, g)
            hn = a * rgh_ref[:, sl] + gx
            rgh_o[:, sl] = hn
            y_ref[0, :, sl] = (hn * _gelu(ptm_ref[N_GROUPS + g])).astype(BF16)

        mconv, mx = conv(2 * N_GROUPS, mcw_ref, mcb_ref, mcv_ref, mcv_o)
        mcb = (mconv * _sigmoid(mconv)).astype(BF16)
        mxb = mx.astype(BF16)

        ig_r = gi_ref[...] + bi_ref[...]
        lf_r = jax.nn.log_sigmoid(gf_ref[...] + bf_ref[...])
        m0_r = mrow_ref[...]
        m_o[...] = jnp.maximum(lf_r + m0_r, ig_r)
        z = jnp.concatenate([gi_ref[...], gf_ref[...], jnp.zeros((ns - 2 * GATE_ROWS, ns), F32)], axis=0).T
        ig_c = z[:, 0:GATE_ROWS] + bic_ref[...]
        lf_c = jax.nn.log_sigmoid(z[:, GATE_ROWS:2 * GATE_ROWS] + bfc_ref[...])
        m0_c = mcol_ref[...]
        m_c = jnp.maximum(lf_c + m0_c, ig_c)
        inter_c = jnp.exp(lf_c + m0_c - m_c)
        dd_c = jnp.exp(ig_c - m_c)
        inter_o[...] = inter_c
        col_s[0] = inter_c
        col_s[1] = jnp.exp(-m_c)
        for h in range(ML_HEADS):
            sl = slice(h * ML_HD, (h + 1) * ML_HD)
            q = jnp.dot(mcb[:, sl], wq_ref[h], preferred_element_type=F32) * (ML_HD ** -0.5)
            k = jnp.dot(mcb[:, sl], wk_ref[h], preferred_element_type=F32)
            v = jnp.dot(mxb[:, sl], wv_ref[h], preferred_element_type=F32)
            q_s[h] = q
            dk_o[h] = dd_c[:, h:h + 1] * k
            v_o[h] = v
            col_s[2, :, h:h + 1] = jnp.sum(q * k, axis=-1, keepdims=True) * dd_c[:, h:h + 1]
            n0 = n_ref[h]
            col_s[3, :, h:h + 1] = jnp.sum(q * n0, axis=-1, keepdims=True)
            n_o[h] = inter_c[:, h:h + 1] * n0 + dd_c[:, h:h + 1] * k

    blk = pl.ds(pl.multiple_of(i * SBLK, SBLK), SBLK)
    for h in range(ML_HEADS):
        z = jnp.concatenate([q_s[h, blk, :], jnp.zeros((ML_HD - SBLK, ML_HD), F32)], axis=0).T
        qc_s[h, blk, :] = jnp.concatenate(
            [jnp.sum(z[:, j:j + 1] * c_ref[0, j, h], axis=0, keepdims=True) for j in range(SBLK)], axis=0)

    @pl.when(i == pl.num_programs(0) - 1)
    def _finish():
        for h in range(ML_HEADS):
            sl = slice(h * ML_HD, (h + 1) * ML_HD)
            inter = col_s[0, :, h:h + 1]
            s = col_s[2, :, h:h + 1]
            num = s * v_o[h] + inter * qc_s[h]
            den = s + inter * col_s[3, :, h:h + 1]
            hout = num / jnp.maximum(jnp.abs(den), col_s[1, :, h:h + 1])
            hm = _sigmoid(pbm_ref[:, sl]) * hout
            hm = hm * lax.rsqrt(jnp.mean(hm * hm, axis=-1, keepdims=True) + EPS)
            y_ref[0, :, D_RG + h * ML_HD:D_RG + (h + 1) * ML_HD] = (hm * mng_ref[:, sl]).astype(BF16)


def _smix(ptm, pbm, gi, gf, st, lw, l, *, name):
    ns = pbm.shape[0]
    rgh, rgc, mcv, mc_all, mn, mcol, mrow = st
    in_arrays = [ptm, pbm, gi, gf, rgh, rgc, mcv, mc_all, mn, mcol, mrow,
                 lw["rcw"], lw["rcb"], lw["wa"], lw["wx"], lw["rba"], lw["rbx"], lw["lam"],
                 lw["mcw"], lw["mcb"], lw["wq"], lw["wk"], lw["wv"], lw["bi_s"], lw["bf_s"],
                 lw["bi_c"], lw["bf_c"], lw["mng"]]
    in_specs = [_const_spec(a.shape) for a in in_arrays[:4]] + [_layer_spec(a, l) for a in in_arrays[4:]]
    in_specs[7] = pl.BlockSpec((1, SBLK, ML_HEADS, ML_HD, ML_HD), lambda i: (l, i, 0, 0, 0))
    out_shape = [
        jax.ShapeDtypeStruct((1, ns, D_MODEL), BF16),
        jax.ShapeDtypeStruct((ns, D_RG), F32),
        jax.ShapeDtypeStruct((CONV_W - 1, ns, D_RG), F32),
        jax.ShapeDtypeStruct((CONV_W - 1, ns, D_ML), F32),
        jax.ShapeDtypeStruct((ML_HEADS, ns, ML_HD), F32),
        jax.ShapeDtypeStruct((GATE_ROWS, ns), F32),
        jax.ShapeDtypeStruct((ns, GATE_ROWS), F32),
        jax.ShapeDtypeStruct((ML_HEADS, ns, ML_HD), F32),
        jax.ShapeDtypeStruct((ML_HEADS, ns, ML_HD), F32),
    ]
    out_specs = [pl.BlockSpec(s.shape, lambda i, n=len(s.shape): (0,) * n) for s in out_shape]
    scratch = [
        pltpu.VMEM((ML_HEADS, ns, ML_HD), F32),
        pltpu.VMEM((ML_HEADS, ns, ML_HD), F32),
        pltpu.VMEM((4, ns, GATE_ROWS), F32),
    ]
    return pl.pallas_call(
        _smix_body,
        grid=(ns // SBLK,),
        in_specs=in_specs,
        out_specs=out_specs,
        out_shape=out_shape,
        scratch_shapes=scratch,
        compiler_params=pltpu.CompilerParams(dimension_semantics=("arbitrary",), vmem_limit_bytes=VMEM_LIMIT),
        name=name,
    )(*in_arrays)


CBLK = 2 * SUBLANES


def _cupdate_body(c_ref, inter_ref, dk_ref, v_ref, o_ref):
    for h in range(ML_HEADS):
        z = jnp.concatenate([dk_ref[0, h], jnp.zeros((ML_HD - CBLK, ML_HD), F32)], axis=0).T
        inter_rows = jnp.broadcast_to(inter_ref[0, :, h:h + 1], (CBLK, LANES))
        vblk = v_ref[0, h]
        for j in range(CBLK):
            o_ref[0, j, h] = inter_rows[j:j + 1, :] * c_ref[0, j, h] + z[:, j:j + 1] * vblk[j:j + 1, :]


def _cupdate(mc_all, inter_all, dk_all, v_all):
    depth, ns = mc_all.shape[:2]
    cspec = pl.BlockSpec((1, CBLK, ML_HEADS, ML_HD, ML_HD), lambda l, i: (l, i, 0, 0, 0))
    kvspec = pl.BlockSpec((1, ML_HEADS, CBLK, ML_HD), lambda l, i: (l, 0, i, 0))
    return pl.pallas_call(
        _cupdate_body,
        grid=(depth, ns // CBLK),
        in_specs=[cspec, pl.BlockSpec((1, CBLK, GATE_ROWS), lambda l, i: (l, i, 0)), kvspec, kvspec],
        out_specs=cspec,
        out_shape=jax.ShapeDtypeStruct(mc_all.shape, F32),
        compiler_params=pltpu.CompilerParams(dimension_semantics=("arbitrary", "arbitrary"),
                                             vmem_limit_bytes=VMEM_LIMIT),
        name="cupdate_s",
    )(mc_all, inter_all, dk_all, v_all)


def _block_diag_pairs(w):
    depth = w.shape[0]
    w = w.reshape(depth, N_GROUPS, 2, RG_BLOCK, RG_BLOCK)
    zero = jnp.zeros((depth, N_GROUPS, RG_BLOCK, RG_BLOCK), w.dtype)
    top = jnp.concatenate([w[:, :, 0], zero], axis=-1)
    bottom = jnp.concatenate([zero, w[:, :, 1]], axis=-1)
    return jnp.concatenate([top, bottom], axis=-2)


def _gate_rows(w_cols):
    depth = w_cols.shape[0]
    return jnp.concatenate([jnp.swapaxes(w_cols, 1, 2),
                            jnp.zeros((depth, GATE_ROWS - ML_HEADS, D_MODEL), w_cols.dtype)], axis=1)


def _stacked_weights(ln1_g, w_in, rg_conv_w, rg_conv_b, rg_w_a, rg_w_x, rg_b_a, rg_b_x, rg_lambda,
                     m_conv_w, m_conv_b, m_w_q, m_w_k, m_w_v, m_b_i, m_b_f, m_norm_g, w_out, ln2_g,
                     w_ff1, w_ff2, nb_prompt, ns):
    depth = ln1_g.shape[0]
    row = lambda a: a.reshape(depth, 1, -1)
    pad8 = lambda a: jnp.concatenate([a, jnp.zeros((depth, GATE_ROWS - ML_HEADS), F32)], axis=1)
    bias8, bfor8 = pad8(m_b_i), pad8(m_b_f)
    lanes = lambda a: jnp.concatenate([jnp.tile(a, (1, nb_prompt)),
                                       jnp.zeros((depth, LANES - nb_prompt * GATE_ROWS), F32)], axis=1)[:, None, :]
    return dict(
        ln1=row(ln1_g),
        w_in=w_in.astype(BF16),
        wgi=_gate_rows(w_in[:, :, D_MAIN:D_MAIN + ML_HEADS]).astype(BF16),
        wgf=_gate_rows(w_in[:, :, D_MAIN + ML_HEADS:]).astype(BF16),
        rcw=rg_conv_w, rcb=row(rg_conv_b),
        wa=_block_diag_pairs(rg_w_a).astype(BF16), wx=_block_diag_pairs(rg_w_x).astype(BF16),
        rba=row(rg_b_a), rbx=row(rg_b_x), lam=row(rg_lambda),
        mcw=m_conv_w, mcb=row(m_conv_b),
        wq=m_w_q.astype(BF16), wk=m_w_k.astype(BF16), wkt=jnp.swapaxes(m_w_k, 2, 3).astype(BF16),
        wv=m_w_v.astype(BF16),
        bi=lanes(bias8), bf=lanes(bfor8),
        bi_s=jnp.broadcast_to(bias8[:, :, None], (depth, GATE_ROWS, ns)),
        bf_s=jnp.broadcast_to(bfor8[:, :, None], (depth, GATE_ROWS, ns)),
        bi_c=bias8.reshape(depth, 1, GATE_ROWS), bf_c=bfor8.reshape(depth, 1, GATE_ROWS),
        mng=row(m_norm_g),
        w_out=w_out.astype(BF16), ln2=row(ln2_g),
        w_ff1=w_ff1.astype(BF16), w_ff2=w_ff2.astype(BF16),
    )


def kernel(x_prompt, x_sample, state_rg_h, state_rg_conv, state_m_conv, state_m_C, state_m_n, state_m_m,
           meta_tokens, ln1_g, w_in, rg_conv_w, rg_conv_b, rg_w_a, rg_w_x, rg_b_a, rg_b_x, rg_lambda,
           m_conv_w, m_conv_b, m_w_q, m_w_k, m_w_v, m_b_i, m_b_f, m_norm_g, w_out, ln2_g,
           w_ff1, w_ff2, ln_f_g):
    nb, seq, d = x_prompt.shape
    ns = x_sample.shape[0]
    depth = ln1_g.shape[0]
    t_real = N_META + seq
    nc = -(-t_real // CHUNK)
    pad = nc * CHUNK - t_real
    assert (pad + N_META) % CHUNK == 0 and x_sample.shape[1] == 1 and ns == LANES and depth == DEPTH
    assert CHUNK == LANES and nb * GATE_ROWS <= LANES

    lw = _stacked_weights(ln1_g, w_in, rg_conv_w, rg_conv_b, rg_w_a, rg_w_x, rg_b_a, rg_b_x, rg_lambda,
                          m_conv_w, m_conv_b, m_w_q, m_w_k, m_w_v, m_b_i, m_b_f, m_norm_g, w_out, ln2_g,
                          w_ff1, w_ff2, nb, ns)
    lnf = ln_f_g.reshape(1, d)
    meta = meta_tokens.astype(F32)
    xp = x_prompt
    xs = x_sample.reshape(1, ns, d)
    m_cols = jnp.concatenate([state_m_m, jnp.zeros((depth, ns, GATE_ROWS - ML_HEADS), F32)], axis=2)
    st_in = (state_rg_h, jnp.swapaxes(state_rg_conv, 1, 2), jnp.swapaxes(state_m_conv, 1, 2), state_m_C,
             jnp.swapaxes(state_m_n, 1, 2), m_cols, jnp.swapaxes(m_cols, 1, 2))

    p_states = [[] for _ in range(6)]
    s_states = [[] for _ in range(6)]
    for l in range(depth):
        last = l == depth - 1

        y, rgh, rgc, mcv, st, mrow = _pmix(xp, meta, lw, l, pad=pad, lead=l == 0, name=f"mixer_p{l}")
        lt_out = CHUNK // 2
        skip = (pad + N_META) // lt_out if last else 0
        xp = _outmlp(xp, y, lw, l, lnf, lt=lt_out, final=last, skip=skip,
                     nzero=0 if last else pad // lt_out, name=f"outmlp_p{l}", meta=meta if l == 0 else None)
        st = st.reshape(nb, ML_HEADS, ML_HD, S_COLS)
        p_states[0].append(rgh)
        p_states[1].append(jnp.swapaxes(rgc.reshape(CONV_W - 1, nb, D_RG), 0, 1))
        p_states[2].append(jnp.swapaxes(mcv.reshape(CONV_W - 1, nb, D_ML), 0, 1))
        p_states[3].append(st[..., :ML_HD])
        p_states[4].append(st[..., ML_HD])
        p_states[5].append(mrow[0, :nb * GATE_ROWS].reshape(nb, GATE_ROWS)[:, :ML_HEADS])

        stm, sbm, gis, gfs = _inproj(xs, lw, l, lt=ns, name=f"inproj_s{l}")
        ys, srgh, srgc, smcv, smn, smrow, sinter, sdk, sv = _smix(stm, sbm, gis, gfs, st_in, lw, l,
                                                                  name=f"mixer_s{l}")
        xs = _outmlp(xs, ys, lw, l, lnf, lt=ns, final=last, skip=0, nzero=0, name=f"outmlp_s{l}")
        s_states[0].append(srgh)
        s_states[1].append(jnp.swapaxes(srgc, 0, 1))
        s_states[2].append(jnp.swapaxes(smcv, 0, 1))
        s_states[3].append((sinter, sdk, sv))
        s_states[4].append(jnp.swapaxes(smn, 0, 1))
        s_states[5].append(smrow[:ML_HEADS].T)

    y_prompt = xp
    y_sample = xs.reshape(ns, 1, d)
    ps_out = [jnp.stack(s) for s in p_states]
    c_new = _cupdate(state_m_C, *(jnp.stack([t[j] for t in s_states[3]]) for j in range(3)))
    ss_out = [c_new if j == 3 else jnp.stack(s) for j, s in enumerate(s_states)]
    return (y_prompt, y_sample, *ps_out, *ss_out)
```

```python
import functools

import jax
import jax.numpy as jnp
from jax import lax
from jax.experimental import pallas as pl
from jax.experimental.pallas import tpu as pltpu

F32 = jnp.float32
BF16 = jnp.bfloat16

D_MODEL = 1024
N_META = 16
D_RG = 512
RG_BLOCK = 64
RG_C = 8.0
D_ML = 512
ML_HEADS = 4
ML_HD = 128
CONV_W = 4
D_FF = 4096
EPS = 1e-6
DEPTH = 4

LANES = 128
SUBLANES = 8
CHUNK = 128
PITCH = CHUNK + SUBLANES
N_GROUPS = D_RG // LANES
D_MAIN = 2 * D_RG + 2 * D_ML
GATE_ROWS = SUBLANES
S_COLS = 2 * ML_HD
N_SLABS = 3 * N_GROUPS
FF_TILE = 1024
VMEM_LIMIT = 56 * 1024 * 1024

_NT = (((1,), (1,)), ((), ()))


def _const_spec(shape):
    zeros = (0,) * len(shape)
    return pl.BlockSpec(shape, lambda *_: zeros, pipeline_mode=pl.Buffered(1))


def _layer_spec(arr, l):
    tail = (0,) * (arr.ndim - 1)
    return pl.BlockSpec((None,) + arr.shape[1:], lambda *_: (l,) + tail, pipeline_mode=pl.Buffered(1))


def _rms(x, g):
    return x * lax.rsqrt(jnp.mean(x * x, axis=-1, keepdims=True) + EPS) * g


def _sigmoid(x):
    return 0.5 * jnp.tanh(0.5 * x) + 0.5


_GELU_K0 = 0.7978845608028654
_GELU_K1 = _GELU_K0 * 0.044715


def _gelu(x):
    hx = 0.5 * x
    return hx * jnp.tanh(x * (_GELU_K0 + _GELU_K1 * (x * x))) + hx


def _sqrt_nonneg(s):
    return jnp.where(s > 0.0, s * lax.rsqrt(s), 0.0)


def _project(u, w_ref, n, dst_ref, base, nb, lt):
    pr = jnp.dot(u, w_ref[:, n * 512:(n + 1) * 512], preferred_element_type=F32)
    if n == D_MAIN // 512 - 1:
        dst_ref[...] = pr
        return
    for g in range(N_GROUPS):
        cols = slice(g * LANES, (g + 1) * LANES)
        if nb == 1:
            dst_ref[base + g] = pr[:, cols]
            continue
        for b in range(nb):
            dst_ref[base + g, pl.ds(b, lt, stride=nb), :] = pr[b * lt:(b + 1) * lt, cols]


def _gate_preacts(u, wg_ref, nb, lt):
    return jnp.concatenate([lax.dot_general(wg_ref[...], u[b * lt:(b + 1) * lt], _NT, preferred_element_type=F32)
                            for b in range(nb)], axis=0)


def _lead_block(meta_ref, nb, lt):
    meta = meta_ref[...]
    blk = jnp.concatenate([jnp.zeros((lt - meta.shape[0], meta.shape[1]), F32), meta], axis=0)
    return jnp.broadcast_to(blk[None], (nb, lt, meta.shape[1]))


def _outmlp_body(x_ref, y_ref, wo_ref, g2_ref, w1_ref, w2_ref, gf_ref, *rest, final, nzero, lead):
    o_ref = rest[-1]
    nb, lt, d = y_ref.shape
    rows = nb * lt

    def compute():
        x = x_ref[...]
        if lead:
            x = jnp.where(pl.program_id(0) == nzero, _lead_block(rest[0], nb, lt), x)
        x = x.reshape(rows, d)
        y = y_ref[...].reshape(rows, d)
        x1 = x + jnp.dot(y, wo_ref[...], preferred_element_type=F32)
        u2 = _rms(x1, g2_ref[...]).astype(BF16)
        acc = x1
        for c in range(D_FF // FF_TILE):
            cols = slice(c * FF_TILE, (c + 1) * FF_TILE)
            h = jnp.dot(u2, w1_ref[:, cols], preferred_element_type=F32)
            h = jnp.square(jnp.maximum(h, 0.0)).astype(BF16)
            acc = acc + jnp.dot(h, w2_ref[cols, :], preferred_element_type=F32)
        if final:
            acc = _rms(acc, gf_ref[...])
        o_ref[...] = acc.reshape(nb, lt, d)

    def zero():
        o_ref[...] = jnp.zeros(o_ref.shape, F32)

    if nzero:
        i = pl.program_id(0)
        pl.when(i < nzero)(zero)
        pl.when(i >= nzero)(compute)
    else:
        compute()


def _outmlp(x, y, lw, l, gf, *, lt, final, skip, nzero, name, meta=None):
    nb, t, d = y.shape
    nsteps = t // lt - skip
    lead = meta is not None
    x_map = (lambda i: (0, jnp.maximum(i - nzero - 1, 0), 0)) if lead else (lambda i: (0, i + skip, 0))
    return pl.pallas_call(
        functools.partial(_outmlp_body, final=final, nzero=nzero, lead=lead),
        grid=(nsteps,),
        in_specs=[
            pl.BlockSpec((nb, lt, d), x_map),
            pl.BlockSpec((nb, lt, d), lambda i: (0, i + skip, 0)),
            _layer_spec(lw["w_out"], l),
            _layer_spec(lw["ln2"], l),
            _layer_spec(lw["w_ff1"], l),
            _layer_spec(lw["w_ff2"], l),
            _const_spec((1, d)),
        ] + ([_const_spec(meta.shape)] if lead else []),
        out_specs=pl.BlockSpec((nb, lt, d), lambda i: (0, i, 0)),
        out_shape=jax.ShapeDtypeStruct((nb, nsteps * lt, d), F32),
        compiler_params=pltpu.CompilerParams(dimension_semantics=("arbitrary",), vmem_limit_bytes=VMEM_LIMIT),
        name=name,
    )(x, y, lw["w_out"], lw["ln2"], lw["w_ff1"], lw["w_ff2"], gf, *([meta] if lead else []))


def _outmlp_stream_body(x_ref, y_ref, wo_ref, g2_ref, w1_ref, w2_ref, gf_ref, o_ref, u2_s, acc_s, *, final):
    j = pl.program_id(0)

    @pl.when(j == 0)
    def _head():
        x1 = x_ref[0] + jnp.dot(y_ref[0], wo_ref[...], preferred_element_type=F32)
        u2_s[...] = _rms(x1, g2_ref[...]).astype(BF16)
        acc_s[...] = x1

    h = jnp.dot(u2_s[...], w1_ref[...], preferred_element_type=F32)
    h = jnp.square(jnp.maximum(h, 0.0)).astype(BF16)
    acc_s[...] += jnp.dot(h, w2_ref[...], preferred_element_type=F32)

    @pl.when(j == pl.num_programs(0) - 1)
    def _tail():
        acc = acc_s[...]
        o_ref[0] = _rms(acc, gf_ref[...]) if final else acc


def _outmlp_stream(x, y, lw, l, gf, *, final, name):
    _, rows, d = x.shape
    return pl.pallas_call(
        functools.partial(_outmlp_stream_body, final=final),
        grid=(D_FF // FF_TILE,),
        in_specs=[
            _const_spec(x.shape),
            _const_spec(y.shape),
            _layer_spec(lw["w_out"], l),
            _layer_spec(lw["ln2"], l),
            pl.BlockSpec((None, d, FF_TILE), lambda j: (l, 0, j)),
            pl.BlockSpec((None, FF_TILE, d), lambda j: (l, j, 0)),
            _const_spec((1, d)),
        ],
        out_specs=pl.BlockSpec(x.shape, lambda j: (0, 0, 0)),
        out_shape=jax.ShapeDtypeStruct(x.shape, F32),
        scratch_shapes=[pltpu.VMEM((rows, d), BF16), pltpu.VMEM((rows, d), F32)],
        compiler_params=pltpu.CompilerParams(dimension_semantics=("arbitrary",), vmem_limit_bytes=VMEM_LIMIT),
        name=name,
    )(x, y, lw["w_out"], lw["ln2"], lw["w_ff1"], lw["w_ff2"], gf)


def _rg_gates(xc, wa_ref, wx_ref, rba_ref, rbx_ref, lam_ref, g):
    sl = slice(g * LANES, (g + 1) * LANES)
    xg = xc[:, sl]
    xb = xg.astype(BF16)
    i = _sigmoid(jnp.dot(xb, wx_ref[g], preferred_element_type=F32) + rbx_ref[:, sl])
    c = (0.5 * RG_C) * jax.nn.log_sigmoid(lam_ref[:, sl])
    pre = jnp.dot(xb, wa_ref[g], preferred_element_type=F32) + rba_ref[:, sl]
    a = jnp.exp(c * jnp.tanh(0.5 * pre) + c)
    gx = _sqrt_nonneg(1.0 - a * a) * (i * xg)
    return a, gx


def _row_scan(x, op, fill):
    row = lax.broadcasted_iota(jnp.int32, x.shape, 0)
    sh = 1
    while sh < x.shape[0]:
        x = op(x, jnp.where(row >= sh, pltpu.roll(x, sh, 0), fill))
        sh *= 2
    return x


def _slabs(ptm_ref, first):
    return jnp.concatenate([ptm_ref[first + g] for g in range(N_GROUPS)], axis=1)


def _pmix_body(x_ref, meta_ref, g1_ref, w_ref, wgi_ref, wgf_ref,
               rcw_ref, rcb_ref, wa_ref, wx_ref, rba_ref, rbx_ref, lam_ref,
               mcw_ref, mcb_ref, wq_ref, wkt_ref, wv_ref, bi_ref, bf_ref, mng_ref,
               y_ref, rgh_ref, rgc_ref, mcv_ref, st_ref, m_ref,
               px_s, pg_s, pm_s, po_s, a_s, g_s, bmc_s, bmx_s, yb_s, q_s, kt_s, v_s, ho_s, col_s, row_s, *, pad, lead):
    nb = y_ref.shape[0]
    lt = CHUNK
    rows = nb * lt
    ntail = (CONV_W - 1) * nb
    c = pl.program_id(0)

    @pl.when(c == 0)
    def _init():
        rgh_ref[...] = jnp.zeros_like(rgh_ref)
        rgc_ref[...] = jnp.zeros_like(rgc_ref)
        mcv_ref[...] = jnp.zeros_like(mcv_ref)
        st_ref[...] = jnp.zeros_like(st_ref)
        m_ref[...] = jnp.zeros_like(m_ref)
        ones_col = (lax.broadcasted_iota(jnp.int32, (rows, ML_HD), 1) == 0).astype(BF16)
        for h in range(ML_HEADS):
            v_s[h, :, ML_HD:S_COLS] = ones_col

    def conv(x, w_ref, b_ref, tail_ref):
        tail = tail_ref[...]
        acc = b_ref[...] + w_ref[CONV_W - 1:CONV_W, :] * x
        for k in range(1, CONV_W):
            shifted = jnp.concatenate([tail[ntail - k * nb:], x[:rows - k * nb]], axis=0)
            acc = acc + w_ref[CONV_W - 1 - k:CONV_W - k, :] * shifted
        tail_ref[...] = x[rows - ntail:]
        return acc

    x = x_ref[...]
    if lead:
        x = jnp.where(c == 0, _lead_block(meta_ref, nb, lt), x)
    u = _rms(x.reshape(rows, x.shape[2]), g1_ref[...]).astype(BF16)
    _project(u, w_ref, 2, pm_s, 0, nb, lt)
    gi = _gate_preacts(u, wgi_ref, nb, lt)
    gf = _gate_preacts(u, wgf_ref, nb, lt)

    mx = _slabs(pm_s, 0)
    mconv = conv(mx, mcw_ref, mcb_ref, mcv_ref)
    mc = mconv * _sigmoid(mconv)
    _project(u, w_ref, 0, px_s, 0, nb, lt)
    for t in range(lt):
        rs = slice(t * nb, (t + 1) * nb)
        idx = pl.ds(t, nb, stride=PITCH)
        for g in range(N_GROUPS):
            cols = slice(g * LANES, (g + 1) * LANES)
            bmc_s[g, idx, :] = mc[rs, cols]
            bmx_s[g, idx, :] = mx[rs, cols]

    def seq_major(src):
        return jnp.concatenate(
            [jnp.concatenate([src[g, b * PITCH:b * PITCH + lt, :] for b in range(nb)], axis=0)
             for g in range(N_GROUPS)], axis=1).astype(BF16)

    mcb = seq_major(bmc_s)
    mxb = seq_major(bmx_s)
    for h in range(ML_HEADS):
        sl = slice(h * ML_HD, (h + 1) * ML_HD)
        q_s[h] = jnp.dot(mcb[:, sl], wq_ref[h], preferred_element_type=F32) * (ML_HD ** -0.5)
        kt = lax.dot_general(wkt_ref[h], mcb[:, sl], _NT, preferred_element_type=F32)
        for b in range(nb):
            kt_s[h, b] = kt[:, b * lt:(b + 1) * lt]
        v_s[h, :, 0:ML_HD] = jnp.dot(mxb[:, sl], wv_ref[h], preferred_element_type=F32).astype(BF16)

    def to_cols(r):
        return jnp.concatenate([r, jnp.zeros((LANES - r.shape[0], lt), F32)], axis=0).T

    ig = to_cols(gi) + bi_ref[...]
    lf = jax.nn.log_sigmoid(to_cols(gf) + bf_ref[...])
    if pad:
        trow = lax.broadcasted_iota(jnp.int32, ig.shape, 0)
        is_pad = trow < jnp.where(c == 0, pad, 0)
        ig = jnp.where(is_pad, -1e30, ig)
        lf = jnp.where(is_pad, 0.0, lf)
    bcs = _row_scan(lf, jnp.add, 0.0)
    gg = ig - bcs
    cm = _row_scan(gg, jnp.maximum, -jnp.inf)
    m0 = m_ref[0:1, :]
    mm = jnp.maximum(m0, cm)
    mt = bcs + mm
    b_last = bcs[lt - 1:lt, :]
    m_last = mt[lt - 1:lt, :]
    col_s[0] = mm
    col_s[1] = jnp.exp(m0 - mm)
    col_s[2] = jnp.exp(-mt)
    row_s[0] = gg.T
    row_s[1] = jnp.exp(b_last - m_last + gg).T
    row_s[2] = jnp.broadcast_to(jnp.exp(b_last + m0 - m_last), (lt, LANES)).T
    m_ref[...] = jnp.broadcast_to(m_last, m_ref.shape)

    rg = {}

    def rg_conv():
        rg["xc"] = conv(_slabs(px_s, 0), rcw_ref, rcb_ref, rgc_ref)
        if pad:
            rowi = lax.broadcasted_iota(jnp.int32, (rows, 1), 0)
            rg["keep"] = rowi >= jnp.where(c == 0, pad * nb, 0)

    def rg_gates(g):
        a, gx = _rg_gates(rg["xc"], wa_ref, wx_ref, rba_ref, rbx_ref, lam_ref, g)
        if pad:
            gx = jnp.where(rg["keep"], gx, 0.0)
        a_s[g] = a
        g_s[g] = gx

    def rg_scan(g):
        sl = slice(g * LANES, (g + 1) * LANES)
        gate = _gelu(pg_s[g])
        h = rgh_ref[:, sl]
        for t in range(lt):
            rs = slice(t * nb, (t + 1) * nb)
            h = a_s[g, rs, :] * h + g_s[g, rs, :]
            yb_s[g, pl.ds(t, nb, stride=PITCH), :] = h * gate[rs]
        rgh_ref[:, sl] = h

    def rg_out(g):
        for b in range(nb):
            y_ref[b, :, g * LANES:(g + 1) * LANES] = yb_s[g, b * PITCH:b * PITCH + lt, :].astype(BF16)

    def gate_out(b):
        trows = slice(b * lt, (b + 1) * lt)
        for h in range(ML_HEADS):
            sl = slice(h * ML_HD, (h + 1) * ML_HD)
            hm = _sigmoid(po_s[trows, sl]) * ho_s[b * ML_HEADS + h]
            hm = hm * lax.rsqrt(jnp.mean(hm * hm, axis=-1, keepdims=True) + EPS)
            y_ref[b, :, D_RG + h * ML_HD:D_RG + (h + 1) * ML_HD] = (hm * mng_ref[:, sl]).astype(BF16)

    pieces = [[] for _ in range(nb + 1)]
    pieces[0].append(rg_conv)
    for g in range(N_GROUPS):
        pieces[g].append(functools.partial(rg_gates, g))
        pieces[g + 1].append(functools.partial(rg_scan, g))
        pieces[g + 2].append(functools.partial(rg_out, g))
    pieces[0].append(functools.partial(_project, u, w_ref, 1, pg_s, 0, nb, lt))
    pieces[1].append(functools.partial(_project, u, w_ref, 3, po_s, 0, nb, lt))
    for b in range(nb):
        pieces[max(b + 1, 2)].append(functools.partial(gate_out, b))

    t_i = lax.broadcasted_iota(jnp.int32, (lt, lt), 0)
    s_i = lax.broadcasted_iota(jnp.int32, (lt, lt), 1)
    causal = s_i <= t_i

    for b in range(nb):
        for piece in pieces[b]:
            piece()
        trows = slice(b * lt, (b + 1) * lt)
        heads = range(ML_HEADS)
        qb = [q_s[h, trows, :].astype(BF16) for h in heads]
        kt = [kt_s[h, b] for h in heads]
        s0 = [st_ref[b * ML_HEADS + h] for h in heads]
        sc = [jnp.dot(qb[h], kt[h].astype(BF16), preferred_element_type=F32) for h in heads]
        qs = [jnp.dot(qb[h], s0[h].astype(BF16), preferred_element_type=F32) for h in heads]
        pm = []
        for h in heads:
            r = b * GATE_ROWS + h
            d = jnp.exp(jnp.where(causal, row_s[0, r:r + 1, :] - col_s[0, :, r:r + 1], -jnp.inf))
            pm.append((sc[h] * d).astype(BF16))
        pv = [jnp.dot(pm[h], v_s[h, trows, :], preferred_element_type=F32) for h in heads]
        for h in heads:
            r = b * GATE_ROWS + h
            numx = pv[h] + col_s[1, :, r:r + 1] * qs[h]
            den = numx[:, ML_HD:ML_HD + 1]
            ho_s[b * ML_HEADS + h] = numx[:, 0:ML_HD] / jnp.maximum(jnp.abs(den), col_s[2, :, r:r + 1])
        for h in heads:
            r = b * GATE_ROWS + h
            wkt = (kt[h] * row_s[1, r:r + 1, :]).astype(BF16)
            sc_row = jnp.concatenate([row_s[2, r:r + 1, :], row_s[2, r:r + 1, :]], axis=1)
            st_ref[b * ML_HEADS + h] = sc_row * s0[h] + jnp.dot(wkt, v_s[h, trows, :], preferred_element_type=F32)
    for piece in pieces[nb]:
        piece()

    if pad:
        @pl.when(c == 0)
        def _zero_pad_rows():
            y_ref[:, 0:pad, :] = jnp.zeros((nb, pad, y_ref.shape[2]), BF16)


def _pmix(x, meta, lw, l, *, pad, lead, name):
    nb, t, d = x.shape
    nc = t // CHUNK + (1 if lead else 0)
    rows = nb * CHUNK
    ntail = (CONV_W - 1) * nb
    in_arrays = [x, meta, lw["ln1"], lw["w_in"], lw["wgi"], lw["wgf"],
                 lw["rcw"], lw["rcb"], lw["wa"], lw["wx"], lw["rba"], lw["rbx"], lw["lam"],
                 lw["mcw"], lw["mcb"], lw["wq"], lw["wkt"], lw["wv"], lw["bi"], lw["bf"], lw["mng"]]
    x_map = (lambda c: (0, jnp.maximum(c - 1, 0), 0)) if lead else (lambda c: (0, c, 0))
    in_specs = [pl.BlockSpec((nb, CHUNK, d), x_map), _const_spec(meta.shape)] + [
        _layer_spec(a, l) for a in in_arrays[2:]]
    in_specs[3] = pl.BlockSpec((None, d, D_MAIN), lambda c: (l, 0, 0), pipeline_mode=pl.Buffered(1))
    out_shape = [
        jax.ShapeDtypeStruct((nb, nc * CHUNK, D_MODEL), BF16),
        jax.ShapeDtypeStruct((nb, D_RG), F32),
        jax.ShapeDtypeStruct((ntail, D_RG), F32),
        jax.ShapeDtypeStruct((ntail, D_ML), F32),
        jax.ShapeDtypeStruct((nb * ML_HEADS, ML_HD, S_COLS), F32),
        jax.ShapeDtypeStruct((SUBLANES, LANES), F32),
    ]
    out_specs = [pl.BlockSpec((nb, CHUNK, D_MODEL), lambda c: (0, c, 0))] + [
        pl.BlockSpec(s.shape, lambda c, n=len(s.shape): (0,) * n) for s in out_shape[1:]]
    scratch = [
        pltpu.VMEM((N_GROUPS, rows, LANES), F32),
        pltpu.VMEM((N_GROUPS, rows, LANES), F32),
        pltpu.VMEM((N_GROUPS, rows, LANES), F32),
        pltpu.VMEM((rows, D_ML), F32),
        pltpu.VMEM((N_GROUPS, rows, LANES), F32),
        pltpu.VMEM((N_GROUPS, rows, LANES), F32),
        pltpu.VMEM((N_GROUPS, nb * PITCH, LANES), F32),
        pltpu.VMEM((N_GROUPS, nb * PITCH, LANES), F32),
        pltpu.VMEM((N_GROUPS, nb * PITCH, LANES), F32),
        pltpu.VMEM((ML_HEADS, rows, ML_HD), F32),
        pltpu.VMEM((ML_HEADS, nb, ML_HD, CHUNK), F32),
        pltpu.VMEM((ML_HEADS, rows, S_COLS), BF16),
        pltpu.VMEM((nb * ML_HEADS, CHUNK, ML_HD), F32),
        pltpu.VMEM((3, CHUNK, LANES), F32),
        pltpu.VMEM((3, LANES, CHUNK), F32),
    ]
    return pl.pallas_call(
        functools.partial(_pmix_body, pad=pad, lead=lead),
        grid=(nc,),
        in_specs=in_specs,
        out_specs=out_specs,
        out_shape=out_shape,
        scratch_shapes=scratch,
        compiler_params=pltpu.CompilerParams(dimension_semantics=("arbitrary",), vmem_limit_bytes=VMEM_LIMIT),
        name=name,
    )(*in_arrays)


SBLK = 2 * SUBLANES


def _smix_body(x_ref, g1_ref, w_ref, wgi_ref, wgf_ref,
               rgh_ref, rgc_ref, mcv_ref, c_ref, n_ref, mcol_ref, mrow_ref,
               rcw_ref, rcb_ref, wa_ref, wx_ref, rba_ref, rbx_ref, lam_ref,
               mcw_ref, mcb_ref, wq_ref, wk_ref, wv_ref, bi_ref, bf_ref, bic_ref, bfc_ref, mng_ref,
               y_ref, rgh_o, rgc_o, mcv_o, n_o, m_o, inter_o, dk_o, v_o,
               ptm_ref, pbm_ref, q_s, qc_s, col_s):
    i = pl.program_id(0)
    ns = pbm_ref.shape[0]

    def conv(first, w_ref, b_ref, tail_ref, tail_o):
        x = _slabs(ptm_ref, first)
        acc = b_ref[...] + w_ref[CONV_W - 1:CONV_W, :] * x
        for j in range(CONV_W - 1):
            acc = acc + w_ref[j:j + 1, :] * tail_ref[j]
        for j in range(CONV_W - 2):
            tail_o[j] = tail_ref[j + 1]
        tail_o[CONV_W - 2] = x
        return acc, x

    @pl.when(i == 0)
    def _rowwise():
        u = _rms(x_ref[0], g1_ref[...]).astype(BF16)
        for n in range(D_MAIN // 512 - 1):
            _project(u, w_ref, n, ptm_ref, n * N_GROUPS, 1, ns)
        _project(u, w_ref, D_MAIN // 512 - 1, pbm_ref, 0, 1, ns)
        gi = _gate_preacts(u, wgi_ref, 1, ns)
        gf = _gate_preacts(u, wgf_ref, 1, ns)
        xc, _ = conv(0, rcw_ref, rcb_ref, rgc_ref, rgc_o)
        for g in range(N_GROUPS):
            sl = slice(g * LANES, (g + 1) * LANES)
            a, gx = _rg_gates(xc, wa_ref, wx_ref, rba_ref, rbx_ref, lam_ref, g)
            hn = a * rgh_ref[:, sl] + gx
            rgh_o[:, sl] = hn
            y_ref[0, :, sl] = (hn * _gelu(ptm_ref[N_GROUPS + g])).astype(BF16)

        mconv, mx = conv(2 * N_GROUPS, mcw_ref, mcb_ref, mcv_ref, mcv_o)
        mcb = (mconv * _sigmoid(mconv)).astype(BF16)
        mxb = mx.astype(BF16)

        ig_r = gi + bi_ref[...]
        lf_r = jax.nn.log_sigmoid(gf + bf_ref[...])
        m0_r = mrow_ref[...]
        m_o[...] = jnp.maximum(lf_r + m0_r, ig_r)
        z = jnp.concatenate([gi, gf, jnp.zeros((ns - 2 * GATE_ROWS, ns), F32)], axis=0).T
        ig_c = z[:, 0:GATE_ROWS] + bic_ref[...]
        lf_c = jax.nn.log_sigmoid(z[:, GATE_ROWS:2 * GATE_ROWS] + bfc_ref[...])
        m0_c = mcol_ref[...]
        m_c = jnp.maximum(lf_c + m0_c, ig_c)
        inter_c = jnp.exp(lf_c + m0_c - m_c)
        dd_c = jnp.exp(ig_c - m_c)
        inter_o[...] = inter_c
        col_s[0] = inter_c
        col_s[1] = jnp.exp(-m_c)
        for h in range(ML_HEADS):
            sl = slice(h * ML_HD, (h + 1) * ML_HD)
            q = jnp.dot(mcb[:, sl], wq_ref[h], preferred_element_type=F32) * (ML_HD ** -0.5)
            k = jnp.dot(mcb[:, sl], wk_ref[h], preferred_element_type=F32)
            v = jnp.dot(mxb[:, sl], wv_ref[h], preferred_element_type=F32)
            q_s[h] = q
            dk_o[h] = dd_c[:, h:h + 1] * k
            v_o[h] = v
            col_s[2, :, h:h + 1] = jnp.sum(q * k, axis=-1, keepdims=True) * dd_c[:, h:h + 1]
            n0 = n_ref[h]
            col_s[3, :, h:h + 1] = jnp.sum(q * n0, axis=-1, keepdims=True)
            n_o[h] = inter_c[:, h:h + 1] * n0 + dd_c[:, h:h + 1] * k

    blk = pl.ds(pl.multiple_of(i * SBLK, SBLK), SBLK)
    for h in range(ML_HEADS):
        z = jnp.concatenate([q_s[h, blk, :], jnp.zeros((ML_HD - SBLK, ML_HD), F32)], axis=0).T
        qc_s[h, blk, :] = jnp.concatenate(
            [jnp.sum(z[:, j:j + 1] * c_ref[0, j, h], axis=0, keepdims=True) for j in range(SBLK)], axis=0)

    @pl.when(i == pl.num_programs(0) - 1)
    def _finish():
        for h in range(ML_HEADS):
            sl = slice(h * ML_HD, (h + 1) * ML_HD)
            inter = col_s[0, :, h:h + 1]
            s = col_s[2, :, h:h + 1]
            num = s * v_o[h] + inter * qc_s[h]
            den = s + inter * col_s[3, :, h:h + 1]
            hout = num / jnp.maximum(jnp.abs(den), col_s[1, :, h:h + 1])
            hm = _sigmoid(pbm_ref[:, sl]) * hout
            hm = hm * lax.rsqrt(jnp.mean(hm * hm, axis=-1, keepdims=True) + EPS)
            y_ref[0, :, D_RG + h * ML_HD:D_RG + (h + 1) * ML_HD] = (hm * mng_ref[:, sl]).astype(BF16)


def _smix(x, st, lw, l, *, name):
    ns, d = x.shape[1:]
    rgh, rgc, mcv, mc_all, mn, mcol, mrow = st
    in_arrays = [x, lw["ln1"], lw["w_in"], lw["wgi"], lw["wgf"], rgh, rgc, mcv, mc_all, mn, mcol, mrow,
                 lw["rcw"], lw["rcb"], lw["wa"], lw["wx"], lw["rba"], lw["rbx"], lw["lam"],
                 lw["mcw"], lw["mcb"], lw["wq"], lw["wk"], lw["wv"], lw["bi_s"], lw["bf_s"],
                 lw["bi_c"], lw["bf_c"], lw["mng"]]
    in_specs = [_const_spec(x.shape)] + [_layer_spec(a, l) for a in in_arrays[1:]]
    in_specs[2] = pl.BlockSpec((None, d, D_MAIN), lambda i: (l, 0, 0), pipeline_mode=pl.Buffered(1))
    in_specs[8] = pl.BlockSpec((1, SBLK, ML_HEADS, ML_HD, ML_HD), lambda i: (l, i, 0, 0, 0))
    out_shape = [
        jax.ShapeDtypeStruct((1, ns, D_MODEL), BF16),
        jax.ShapeDtypeStruct((ns, D_RG), F32),
        jax.ShapeDtypeStruct((CONV_W - 1, ns, D_RG), F32),
        jax.ShapeDtypeStruct((CONV_W - 1, ns, D_ML), F32),
        jax.ShapeDtypeStruct((ML_HEADS, ns, ML_HD), F32),
        jax.ShapeDtypeStruct((GATE_ROWS, ns), F32),
        jax.ShapeDtypeStruct((ns, GATE_ROWS), F32),
        jax.ShapeDtypeStruct((ML_HEADS, ns, ML_HD), F32),
        jax.ShapeDtypeStruct((ML_HEADS, ns, ML_HD), F32),
    ]
    out_specs = [pl.BlockSpec(s.shape, lambda i, n=len(s.shape): (0,) * n) for s in out_shape]
    scratch = [
        pltpu.VMEM((N_SLABS, ns, LANES), F32),
        pltpu.VMEM((ns, D_ML), F32),
        pltpu.VMEM((ML_HEADS, ns, ML_HD), F32),
        pltpu.VMEM((ML_HEADS, ns, ML_HD), F32),
        pltpu.VMEM((4, ns, GATE_ROWS), F32),
    ]
    return pl.pallas_call(
        _smix_body,
        grid=(ns // SBLK,),
        in_specs=in_specs,
        out_specs=out_specs,
        out_shape=out_shape,
        scratch_shapes=scratch,
        compiler_params=pltpu.CompilerParams(dimension_semantics=("arbitrary",), vmem_limit_bytes=VMEM_LIMIT),
        name=name,
    )(*in_arrays)


CBLK = 2 * SUBLANES


def _cupdate_body(c_ref, inter_ref, dk_ref, v_ref, o_ref):
    for h in range(ML_HEADS):
        z = jnp.concatenate([dk_ref[0, h], jnp.zeros((ML_HD - CBLK, ML_HD), F32)], axis=0).T
        inter_rows = jnp.broadcast_to(inter_ref[0, :, h:h + 1], (CBLK, LANES))
        vblk = v_ref[0, h]
        for j in range(CBLK):
            o_ref[0, j, h] = inter_rows[j:j + 1, :] * c_ref[0, j, h] + z[:, j:j + 1] * vblk[j:j + 1, :]


def _cupdate(mc_all, inter_all, dk_all, v_all):
    depth, ns = mc_all.shape[:2]
    cspec = pl.BlockSpec((1, CBLK, ML_HEADS, ML_HD, ML_HD), lambda l, i: (l, i, 0, 0, 0))
    kvspec = pl.BlockSpec((1, ML_HEADS, CBLK, ML_HD), lambda l, i: (l, 0, i, 0))
    return pl.pallas_call(
        _cupdate_body,
        grid=(depth, ns // CBLK),
        in_specs=[cspec, pl.BlockSpec((1, CBLK, GATE_ROWS), lambda l, i: (l, i, 0)), kvspec, kvspec],
        out_specs=cspec,
        out_shape=jax.ShapeDtypeStruct(mc_all.shape, F32),
        compiler_params=pltpu.CompilerParams(dimension_semantics=("arbitrary", "arbitrary"),
                                             vmem_limit_bytes=VMEM_LIMIT),
        name="cupdate_s",
    )(mc_all, inter_all, dk_all, v_all)


def _block_diag_pairs(w):
    depth = w.shape[0]
    w = w.reshape(depth, N_GROUPS, 2, RG_BLOCK, RG_BLOCK)
    zero = jnp.zeros((depth, N_GROUPS, RG_BLOCK, RG_BLOCK), w.dtype)
    top = jnp.concatenate([w[:, :, 0], zero], axis=-1)
    bottom = jnp.concatenate([zero, w[:, :, 1]], axis=-1)
    return jnp.concatenate([top, bottom], axis=-2)


def _gate_rows(w_cols):
    depth = w_cols.shape[0]
    return jnp.concatenate([jnp.swapaxes(w_cols, 1, 2),
                            jnp.zeros((depth, GATE_ROWS - ML_HEADS, D_MODEL), w_cols.dtype)], axis=1)


def _stacked_weights(ln1_g, w_in, rg_conv_w, rg_conv_b, rg_w_a, rg_w_x, rg_b_a, rg_b_x, rg_lambda,
                     m_conv_w, m_conv_b, m_w_q, m_w_k, m_w_v, m_b_i, m_b_f, m_norm_g, w_out, ln2_g,
                     w_ff1, w_ff2, nb_prompt, ns):
    depth = ln1_g.shape[0]
    row = lambda a: a.reshape(depth, 1, -1)
    pad8 = lambda a: jnp.concatenate([a, jnp.zeros((depth, GATE_ROWS - ML_HEADS), F32)], axis=1)
    bias8, bfor8 = pad8(m_b_i), pad8(m_b_f)
    lanes = lambda a: jnp.concatenate([jnp.tile(a, (1, nb_prompt)),
                                       jnp.zeros((depth, LANES - nb_prompt * GATE_ROWS), F32)], axis=1)[:, None, :]
    return dict(
        ln1=row(ln1_g),
        w_in=w_in.astype(BF16),
        wgi=_gate_rows(w_in[:, :, D_MAIN:D_MAIN + ML_HEADS]).astype(BF16),
        wgf=_gate_rows(w_in[:, :, D_MAIN + ML_HEADS:]).astype(BF16),
        rcw=rg_conv_w, rcb=row(rg_conv_b),
        wa=_block_diag_pairs(rg_w_a).astype(BF16), wx=_block_diag_pairs(rg_w_x).astype(BF16),
        rba=row(rg_b_a), rbx=row(rg_b_x), lam=row(rg_lambda),
        mcw=m_conv_w, mcb=row(m_conv_b),
        wq=m_w_q.astype(BF16), wk=m_w_k.astype(BF16), wkt=jnp.swapaxes(m_w_k, 2, 3).astype(BF16),
        wv=m_w_v.astype(BF16),
        bi=lanes(bias8), bf=lanes(bfor8),
        bi_s=jnp.broadcast_to(bias8[:, :, None], (depth, GATE_ROWS, ns)),
        bf_s=jnp.broadcast_to(bfor8[:, :, None], (depth, GATE_ROWS, ns)),
        bi_c=bias8.reshape(depth, 1, GATE_ROWS), bf_c=bfor8.reshape(depth, 1, GATE_ROWS),
        mng=row(m_norm_g),
        w_out=w_out.astype(BF16), ln2=row(ln2_g),
        w_ff1=w_ff1.astype(BF16), w_ff2=w_ff2.astype(BF16),
    )


def kernel(x_prompt, x_sample, state_rg_h, state_rg_conv, state_m_conv, state_m_C, state_m_n, state_m_m,
           meta_tokens, ln1_g, w_in, rg_conv_w, rg_conv_b, rg_w_a, rg_w_x, rg_b_a, rg_b_x, rg_lambda,
           m_conv_w, m_conv_b, m_w_q, m_w_k, m_w_v, m_b_i, m_b_f, m_norm_g, w_out, ln2_g,
           w_ff1, w_ff2, ln_f_g):
    nb, seq, d = x_prompt.shape
    ns = x_sample.shape[0]
    depth = ln1_g.shape[0]
    t_real = N_META + seq
    nc = -(-t_real // CHUNK)
    pad = nc * CHUNK - t_real
    assert (pad + N_META) % CHUNK == 0 and x_sample.shape[1] == 1 and ns == LANES and depth == DEPTH
    assert CHUNK == LANES and nb * GATE_ROWS <= LANES

    lw = _stacked_weights(ln1_g, w_in, rg_conv_w, rg_conv_b, rg_w_a, rg_w_x, rg_b_a, rg_b_x, rg_lambda,
                          m_conv_w, m_conv_b, m_w_q, m_w_k, m_w_v, m_b_i, m_b_f, m_norm_g, w_out, ln2_g,
                          w_ff1, w_ff2, nb, ns)
    lnf = ln_f_g.reshape(1, d)
    meta = meta_tokens.astype(F32)
    xp = x_prompt
    xs = x_sample.reshape(1, ns, d)
    m_cols = jnp.concatenate([state_m_m, jnp.zeros((depth, ns, GATE_ROWS - ML_HEADS), F32)], axis=2)
    st_in = (state_rg_h, jnp.swapaxes(state_rg_conv, 1, 2), jnp.swapaxes(state_m_conv, 1, 2), state_m_C,
             jnp.swapaxes(state_m_n, 1, 2), m_cols, jnp.swapaxes(m_cols, 1, 2))

    p_states = [[] for _ in range(6)]
    s_states = [[] for _ in range(6)]
    for l in range(depth):
        last = l == depth - 1

        y, rgh, rgc, mcv, st, mrow = _pmix(xp, meta, lw, l, pad=pad, lead=l == 0, name=f"mixer_p{l}")
        lt_out = CHUNK // 2
        skip = (pad + N_META) // lt_out if last else 0
        xp = _outmlp(xp, y, lw, l, lnf, lt=lt_out, final=last, skip=skip,
                     nzero=0 if last else pad // lt_out, name=f"outmlp_p{l}", meta=meta if l == 0 else None)
        st = st.reshape(nb, ML_HEADS, ML_HD, S_COLS)
        p_states[0].append(rgh)
        p_states[1].append(jnp.swapaxes(rgc.reshape(CONV_W - 1, nb, D_RG), 0, 1))
        p_states[2].append(jnp.swapaxes(mcv.reshape(CONV_W - 1, nb, D_ML), 0, 1))
        p_states[3].append(st[..., :ML_HD])
        p_states[4].append(st[..., ML_HD])
        p_states[5].append(mrow[0, :nb * GATE_ROWS].reshape(nb, GATE_ROWS)[:, :ML_HEADS])

        ys, srgh, srgc, smcv, smn, smrow, sinter, sdk, sv = _smix(xs, st_in, lw, l, name=f"mixer_s{l}")
        xs = _outmlp_stream(xs, ys, lw, l, lnf, final=last, name=f"outmlp_s{l}")
        s_states[0].append(srgh)
        s_states[1].append(jnp.swapaxes(srgc, 0, 1))
        s_states[2].append(jnp.swapaxes(smcv, 0, 1))
        s_states[3].append((sinter, sdk, sv))
        s_states[4].append(jnp.swapaxes(smn, 0, 1))
        s_states[5].append(smrow[:ML_HEADS].T)

    y_prompt = xp
    y_sample = xs.reshape(ns, 1, d)
    ps_out = [jnp.stack(s) for s in p_states]
    c_new = _cupdate(state_m_C, *(jnp.stack([t[j] for t in s_states[3]]) for j in range(3)))
    ss_out = [c_new if j == 3 else jnp.stack(s) for j, s in enumerate(s_states)]
    return (y_prompt, y_sample, *ps_out, *ss_out)
```

```python
import functools

import jax
import jax.numpy as jnp
from jax import lax
from jax.experimental import pallas as pl
from jax.experimental.pallas import tpu as pltpu

F32 = jnp.float32
BF16 = jnp.bfloat16

D_MODEL = 1024
N_META = 16
D_RG = 512
RG_BLOCK = 64
RG_C = 8.0
D_ML = 512
ML_HEADS = 4
ML_HD = 128
CONV_W = 4
D_FF = 4096
EPS = 1e-6
DEPTH = 4

LANES = 128
SUBLANES = 8
CHUNK = 128
PITCH = CHUNK + SUBLANES
N_GROUPS = D_RG // LANES
D_MAIN = 2 * D_RG + 2 * D_ML
GATE_ROWS = SUBLANES
S_COLS = 2 * ML_HD
N_SLABS = 3 * N_GROUPS
FF_TILE = 1024
VMEM_LIMIT = 56 * 1024 * 1024

_NT = (((1,), (1,)), ((), ()))


def _const_spec(shape):
    zeros = (0,) * len(shape)
    return pl.BlockSpec(shape, lambda *_: zeros, pipeline_mode=pl.Buffered(1))


def _layer_spec(arr, l):
    tail = (0,) * (arr.ndim - 1)
    return pl.BlockSpec((None,) + arr.shape[1:], lambda *_: (l,) + tail, pipeline_mode=pl.Buffered(1))


def _rms(x, g):
    return x * lax.rsqrt(jnp.mean(x * x, axis=-1, keepdims=True) + EPS) * g


def _sigmoid(x):
    return 0.5 * jnp.tanh(0.5 * x) + 0.5


_GELU_K0 = 0.7978845608028654
_GELU_K1 = _GELU_K0 * 0.044715


def _gelu(x):
    hx = 0.5 * x
    return hx * jnp.tanh(x * (_GELU_K0 + _GELU_K1 * (x * x))) + hx


def _sqrt_nonneg(s):
    return jnp.where(s > 0.0, s * lax.rsqrt(s), 0.0)


def _project(u, w_ref, n, dst_ref, base, nb, lt):
    pr = jnp.dot(u, w_ref[:, n * 512:(n + 1) * 512], preferred_element_type=F32)
    if n == D_MAIN // 512 - 1:
        dst_ref[...] = pr
        return
    for g in range(N_GROUPS):
        cols = slice(g * LANES, (g + 1) * LANES)
        if nb == 1:
            dst_ref[base + g] = pr[:, cols]
            continue
        for b in range(nb):
            dst_ref[base + g, pl.ds(b, lt, stride=nb), :] = pr[b * lt:(b + 1) * lt, cols]


def _gate_preacts(u, wg_ref, nb, lt):
    return jnp.concatenate([lax.dot_general(wg_ref[...], u[b * lt:(b + 1) * lt], _NT, preferred_element_type=F32)
                            for b in range(nb)], axis=0)


def _lead_block(meta_ref, nb, lt):
    meta = meta_ref[...]
    blk = jnp.concatenate([jnp.zeros((lt - meta.shape[0], meta.shape[1]), F32), meta], axis=0)
    return jnp.broadcast_to(blk[None], (nb, lt, meta.shape[1]))


def _outmlp_body(x_ref, y_ref, wo_ref, g2_ref, w1_ref, w2_ref, gf_ref, *rest, final, nzero, lead):
    o_ref = rest[-1]
    nb, lt, d = y_ref.shape
    rows = nb * lt

    def compute():
        x = x_ref[...]
        if lead:
            x = jnp.where(pl.program_id(0) == nzero, _lead_block(rest[0], nb, lt), x)
        x = x.reshape(rows, d)
        y = y_ref[...].reshape(rows, d)
        x1 = x + jnp.dot(y, wo_ref[...], preferred_element_type=F32)
        u2 = _rms(x1, g2_ref[...]).astype(BF16)
        acc = x1
        for c in range(D_FF // FF_TILE):
            cols = slice(c * FF_TILE, (c + 1) * FF_TILE)
            h = jnp.dot(u2, w1_ref[:, cols], preferred_element_type=F32)
            h = jnp.square(jnp.maximum(h, 0.0)).astype(BF16)
            acc = acc + jnp.dot(h, w2_ref[cols, :], preferred_element_type=F32)
        if final:
            acc = _rms(acc, gf_ref[...])
        o_ref[...] = acc.reshape(nb, lt, d)

    def zero():
        o_ref[...] = jnp.zeros(o_ref.shape, F32)

    if nzero:
        i = pl.program_id(0)
        pl.when(i < nzero)(zero)
        pl.when(i >= nzero)(compute)
    else:
        compute()


def _outmlp(x, y, lw, l, gf, *, lt, final, skip, nzero, name, meta=None):
    nb, t, d = y.shape
    nsteps = t // lt - skip
    lead = meta is not None
    x_map = (lambda i: (0, jnp.maximum(i - nzero - 1, 0), 0)) if lead else (lambda i: (0, i + skip, 0))
    return pl.pallas_call(
        functools.partial(_outmlp_body, final=final, nzero=nzero, lead=lead),
        grid=(nsteps,),
        in_specs=[
            pl.BlockSpec((nb, lt, d), x_map),
            pl.BlockSpec((nb, lt, d), lambda i: (0, i + skip, 0)),
            _layer_spec(lw["w_out"], l),
            _layer_spec(lw["ln2"], l),
            _layer_spec(lw["w_ff1"], l),
            _layer_spec(lw["w_ff2"], l),
            _const_spec((1, d)),
        ] + ([_const_spec(meta.shape)] if lead else []),
        out_specs=pl.BlockSpec((nb, lt, d), lambda i: (0, i, 0)),
        out_shape=jax.ShapeDtypeStruct((nb, nsteps * lt, d), F32),
        compiler_params=pltpu.CompilerParams(dimension_semantics=("arbitrary",), vmem_limit_bytes=VMEM_LIMIT),
        name=name,
    )(x, y, lw["w_out"], lw["ln2"], lw["w_ff1"], lw["w_ff2"], gf, *([meta] if lead else []))


def _outmlp_stream_body(x_ref, y_ref, wo_ref, g2_ref, w1_ref, w2_ref, gf_ref, o_ref, u2_s, acc_s, *, final):
    j = pl.program_id(0)

    @pl.when(j == 0)
    def _head():
        x1 = x_ref[0] + jnp.dot(y_ref[0], wo_ref[...], preferred_element_type=F32)
        u2_s[...] = _rms(x1, g2_ref[...]).astype(BF16)
        acc_s[...] = x1

    h = jnp.dot(u2_s[...], w1_ref[...], preferred_element_type=F32)
    h = jnp.square(jnp.maximum(h, 0.0)).astype(BF16)
    acc_s[...] += jnp.dot(h, w2_ref[...], preferred_element_type=F32)

    @pl.when(j == pl.num_programs(0) - 1)
    def _tail():
        acc = acc_s[...]
        o_ref[0] = _rms(acc, gf_ref[...]) if final else acc


def _outmlp_stream(x, y, lw, l, gf, *, final, name):
    _, rows, d = x.shape
    return pl.pallas_call(
        functools.partial(_outmlp_stream_body, final=final),
        grid=(D_FF // FF_TILE,),
        in_specs=[
            _const_spec(x.shape),
            _const_spec(y.shape),
            _layer_spec(lw["w_out"], l),
            _layer_spec(lw["ln2"], l),
            pl.BlockSpec((None, d, FF_TILE), lambda j: (l, 0, j)),
            pl.BlockSpec((None, FF_TILE, d), lambda j: (l, j, 0)),
            _const_spec((1, d)),
        ],
        out_specs=pl.BlockSpec(x.shape, lambda j: (0, 0, 0)),
        out_shape=jax.ShapeDtypeStruct(x.shape, F32),
        scratch_shapes=[pltpu.VMEM((rows, d), BF16), pltpu.VMEM((rows, d), F32)],
        compiler_params=pltpu.CompilerParams(dimension_semantics=("arbitrary",), vmem_limit_bytes=VMEM_LIMIT),
        name=name,
    )(x, y, lw["w_out"], lw["ln2"], lw["w_ff1"], lw["w_ff2"], gf)


def _rg_gates(xc, wa_ref, wx_ref, rba_ref, rbx_ref, lam_ref, g):
    sl = slice(g * LANES, (g + 1) * LANES)
    xg = xc[:, sl]
    xb = xg.astype(BF16)
    i = _sigmoid(jnp.dot(xb, wx_ref[g], preferred_element_type=F32) + rbx_ref[:, sl])
    c = (0.5 * RG_C) * jax.nn.log_sigmoid(lam_ref[:, sl])
    pre = jnp.dot(xb, wa_ref[g], preferred_element_type=F32) + rba_ref[:, sl]
    a = jnp.exp(c * jnp.tanh(0.5 * pre) + c)
    gx = _sqrt_nonneg(1.0 - a * a) * (i * xg)
    return a, gx


def _row_scan(x, op, fill):
    row = lax.broadcasted_iota(jnp.int32, x.shape, 0)
    sh = 1
    while sh < x.shape[0]:
        x = op(x, jnp.where(row >= sh, pltpu.roll(x, sh, 0), fill))
        sh *= 2
    return x


def _slabs(ptm_ref, first):
    return jnp.concatenate([ptm_ref[first + g] for g in range(N_GROUPS)], axis=1)


def _pmix_body(x_ref, meta_ref, g1_ref, w_ref, wgi_ref, wgf_ref,
               rcw_ref, rcb_ref, wa_ref, wx_ref, rba_ref, rbx_ref, lam_ref,
               mcw_ref, mcb_ref, wq_ref, wkt_ref, wv_ref, bi_ref, bf_ref, mng_ref,
               y_ref, rgh_ref, rgc_ref, mcv_ref, st_ref, m_ref,
               px_s, pg_s, pm_s, po_s, a_s, g_s, bmc_s, bmx_s, yb_s, q_s, kt_s, v_s, ho_s, col_s, row_s, *, pad, lead):
    nb = y_ref.shape[0]
    lt = CHUNK
    rows = nb * lt
    ntail = (CONV_W - 1) * nb
    c = pl.program_id(0)

    @pl.when(c == 0)
    def _init():
        rgh_ref[...] = jnp.zeros_like(rgh_ref)
        rgc_ref[...] = jnp.zeros_like(rgc_ref)
        mcv_ref[...] = jnp.zeros_like(mcv_ref)
        st_ref[...] = jnp.zeros_like(st_ref)
        m_ref[...] = jnp.zeros_like(m_ref)
        ones_col = (lax.broadcasted_iota(jnp.int32, (rows, ML_HD), 1) == 0).astype(BF16)
        for h in range(ML_HEADS):
            v_s[h, :, ML_HD:S_COLS] = ones_col

    def conv(x, w_ref, b_ref, tail_ref):
        tail = tail_ref[...]
        acc = b_ref[...] + w_ref[CONV_W - 1:CONV_W, :] * x
        for k in range(1, CONV_W):
            shifted = jnp.concatenate([tail[ntail - k * nb:], x[:rows - k * nb]], axis=0)
            acc = acc + w_ref[CONV_W - 1 - k:CONV_W - k, :] * shifted
        tail_ref[...] = x[rows - ntail:]
        return acc

    x = x_ref[...]
    if lead:
        x = jnp.where(c == 0, _lead_block(meta_ref, nb, lt), x)
    u = _rms(x.reshape(rows, x.shape[2]), g1_ref[...]).astype(BF16)
    _project(u, w_ref, 2, pm_s, 0, nb, lt)
    gi = _gate_preacts(u, wgi_ref, nb, lt)
    gf = _gate_preacts(u, wgf_ref, nb, lt)

    mx = _slabs(pm_s, 0)
    mconv = conv(mx, mcw_ref, mcb_ref, mcv_ref)
    mc = mconv * _sigmoid(mconv)
    _project(u, w_ref, 0, px_s, 0, nb, lt)
    for t in range(lt):
        rs = slice(t * nb, (t + 1) * nb)
        idx = pl.ds(t, nb, stride=PITCH)
        for g in range(N_GROUPS):
            cols = slice(g * LANES, (g + 1) * LANES)
            bmc_s[g, idx, :] = mc[rs, cols]
            bmx_s[g, idx, :] = mx[rs, cols]

    def seq_major(src):
        return jnp.concatenate(
            [jnp.concatenate([src[g, b * PITCH:b * PITCH + lt, :] for b in range(nb)], axis=0)
             for g in range(N_GROUPS)], axis=1).astype(BF16)

    mcb = seq_major(bmc_s)
    mxb = seq_major(bmx_s)
    for h in range(ML_HEADS):
        sl = slice(h * ML_HD, (h + 1) * ML_HD)
        q_s[h] = jnp.dot(mcb[:, sl], wq_ref[h], preferred_element_type=F32) * (ML_HD ** -0.5)
        kt = lax.dot_general(wkt_ref[h], mcb[:, sl], _NT, preferred_element_type=F32)
        for b in range(nb):
            kt_s[h, b] = kt[:, b * lt:(b + 1) * lt]
        v_s[h, :, 0:ML_HD] = jnp.dot(mxb[:, sl], wv_ref[h], preferred_element_type=F32).astype(BF16)

    def to_cols(r):
        return jnp.concatenate([r, jnp.zeros((LANES - r.shape[0], lt), F32)], axis=0).T

    ig = to_cols(gi) + bi_ref[...]
    lf = jax.nn.log_sigmoid(to_cols(gf) + bf_ref[...])
    if pad:
        trow = lax.broadcasted_iota(jnp.int32, ig.shape, 0)
        is_pad = trow < jnp.where(c == 0, pad, 0)
        ig = jnp.where(is_pad, -1e30, ig)
        lf = jnp.where(is_pad, 0.0, lf)
    bcs = _row_scan(lf, jnp.add, 0.0)
    gg = ig - bcs
    cm = _row_scan(gg, jnp.maximum, -jnp.inf)
    m0 = m_ref[0:1, :]
    mm = jnp.maximum(m0, cm)
    mt = bcs + mm
    b_last = bcs[lt - 1:lt, :]
    m_last = mt[lt - 1:lt, :]
    col_s[0] = mm
    col_s[1] = jnp.exp(m0 - mm)
    col_s[2] = jnp.exp(-mt)
    row_s[0] = gg.T
    row_s[1] = jnp.exp(b_last - m_last + gg).T
    row_s[2] = jnp.broadcast_to(jnp.exp(b_last + m0 - m_last), (lt, LANES)).T
    m_ref[...] = jnp.broadcast_to(m_last, m_ref.shape)

    rg = {}

    def rg_conv():
        rg["xc"] = conv(_slabs(px_s, 0), rcw_ref, rcb_ref, rgc_ref)
        if pad:
            rowi = lax.broadcasted_iota(jnp.int32, (rows, 1), 0)
            rg["keep"] = rowi >= jnp.where(c == 0, pad * nb, 0)

    def rg_gates(g):
        a, gx = _rg_gates(rg["xc"], wa_ref, wx_ref, rba_ref, rbx_ref, lam_ref, g)
        if pad:
            gx = jnp.where(rg["keep"], gx, 0.0)
        a_s[g] = a
        g_s[g] = gx

    def rg_scan(g):
        sl = slice(g * LANES, (g + 1) * LANES)
        gate = _gelu(pg_s[g])
        h = rgh_ref[:, sl]
        for t in range(lt):
            rs = slice(t * nb, (t + 1) * nb)
            h = a_s[g, rs, :] * h + g_s[g, rs, :]
            yb_s[g, pl.ds(t, nb, stride=PITCH), :] = h * gate[rs]
        rgh_ref[:, sl] = h

    def rg_out(g):
        for b in range(nb):
            y_ref[b, :, g * LANES:(g + 1) * LANES] = yb_s[g, b * PITCH:b * PITCH + lt, :].astype(BF16)

    def gate_out(b):
        trows = slice(b * lt, (b + 1) * lt)
        for h in range(ML_HEADS):
            sl = slice(h * ML_HD, (h + 1) * ML_HD)
            hm = _sigmoid(po_s[trows, sl]) * ho_s[b * ML_HEADS + h]
            hm = hm * lax.rsqrt(jnp.mean(hm * hm, axis=-1, keepdims=True) + EPS)
            y_ref[b, :, D_RG + h * ML_HD:D_RG + (h + 1) * ML_HD] = (hm * mng_ref[:, sl]).astype(BF16)

    pieces = [[] for _ in range(nb + 1)]
    pieces[0].append(rg_conv)
    for g in range(N_GROUPS):
        pieces[g].append(functools.partial(rg_gates, g))
        pieces[g + 1].append(functools.partial(rg_scan, g))
        pieces[g + 2].append(functools.partial(rg_out, g))
    pieces[0].append(functools.partial(_project, u, w_ref, 1, pg_s, 0, nb, lt))
    pieces[1].append(functools.partial(_project, u, w_ref, 3, po_s, 0, nb, lt))
    for b in range(nb):
        pieces[max(b + 1, 2)].append(functools.partial(gate_out, b))

    t_i = lax.broadcasted_iota(jnp.int32, (lt, lt), 0)
    s_i = lax.broadcasted_iota(jnp.int32, (lt, lt), 1)
    causal = s_i <= t_i

    for b in range(nb):
        for piece in pieces[b]:
            piece()
        trows = slice(b * lt, (b + 1) * lt)
        heads = range(ML_HEADS)
        qb = [q_s[h, trows, :].astype(BF16) for h in heads]
        kt = [kt_s[h, b] for h in heads]
        s0 = [st_ref[b * ML_HEADS + h] for h in heads]
        sc = [jnp.dot(qb[h], kt[h].astype(BF16), preferred_element_type=F32) for h in heads]
        qs = [jnp.dot(qb[h], s0[h].astype(BF16), preferred_element_type=F32) for h in heads]
        pm = []
        for h in heads:
            r = b * GATE_ROWS + h
            d = jnp.exp(jnp.where(causal, row_s[0, r:r + 1, :] - col_s[0, :, r:r + 1], -jnp.inf))
            pm.append((sc[h] * d).astype(BF16))
        pv = [jnp.dot(pm[h], v_s[h, trows, :], preferred_element_type=F32) for h in heads]
        for h in heads:
            r = b * GATE_ROWS + h
            numx = pv[h] + col_s[1, :, r:r + 1] * qs[h]
            den = numx[:, ML_HD:ML_HD + 1]
            ho_s[b * ML_HEADS + h] = numx[:, 0:ML_HD] / jnp.maximum(jnp.abs(den), col_s[2, :, r:r + 1])
        for h in heads:
            r = b * GATE_ROWS + h
            wkt = (kt[h] * row_s[1, r:r + 1, :]).astype(BF16)
            sc_row = jnp.concatenate([row_s[2, r:r + 1, :], row_s[2, r:r + 1, :]], axis=1)
            st_ref[b * ML_HEADS + h] = sc_row * s0[h] + jnp.dot(wkt, v_s[h, trows, :], preferred_element_type=F32)
    for piece in pieces[nb]:
        piece()

    if pad:
        @pl.when(c == 0)
        def _zero_pad_rows():
            y_ref[:, 0:pad, :] = jnp.zeros((nb, pad, y_ref.shape[2]), BF16)


def _pmix(x, meta, lw, l, *, pad, lead, name):
    nb, t, d = x.shape
    nc = t // CHUNK + (1 if lead else 0)
    rows = nb * CHUNK
    ntail = (CONV_W - 1) * nb
    in_arrays = [x, meta, lw["ln1"], lw["w_in"], lw["wgi"], lw["wgf"],
                 lw["rcw"], lw["rcb"], lw["wa"], lw["wx"], lw["rba"], lw["rbx"], lw["lam"],
                 lw["mcw"], lw["mcb"], lw["wq"], lw["wkt"], lw["wv"], lw["bi"], lw["bf"], lw["mng"]]
    x_map = (lambda c: (0, jnp.maximum(c - 1, 0), 0)) if lead else (lambda c: (0, c, 0))
    in_specs = [pl.BlockSpec((nb, CHUNK, d), x_map), _const_spec(meta.shape)] + [
        _layer_spec(a, l) for a in in_arrays[2:]]
    in_specs[3] = pl.BlockSpec((None, d, D_MAIN), lambda c: (l, 0, 0), pipeline_mode=pl.Buffered(1))
    out_shape = [
        jax.ShapeDtypeStruct((nb, nc * CHUNK, D_MODEL), BF16),
        jax.ShapeDtypeStruct((nb, D_RG), F32),
        jax.ShapeDtypeStruct((ntail, D_RG), F32),
        jax.ShapeDtypeStruct((ntail, D_ML), F32),
        jax.ShapeDtypeStruct((nb * ML_HEADS, ML_HD, S_COLS), F32),
        jax.ShapeDtypeStruct((SUBLANES, LANES), F32),
    ]
    out_specs = [pl.BlockSpec((nb, CHUNK, D_MODEL), lambda c: (0, c, 0))] + [
        pl.BlockSpec(s.shape, lambda c, n=len(s.shape): (0,) * n) for s in out_shape[1:]]
    scratch = [
        pltpu.VMEM((N_GROUPS, rows, LANES), F32),
        pltpu.VMEM((N_GROUPS, rows, LANES), F32),
        pltpu.VMEM((N_GROUPS, rows, LANES), F32),
        pltpu.VMEM((rows, D_ML), F32),
        pltpu.VMEM((N_GROUPS, rows, LANES), F32),
        pltpu.VMEM((N_GROUPS, rows, LANES), F32),
        pltpu.VMEM((N_GROUPS, nb * PITCH, LANES), F32),
        pltpu.VMEM((N_GROUPS, nb * PITCH, LANES), F32),
        pltpu.VMEM((N_GROUPS, nb * PITCH, LANES), F32),
        pltpu.VMEM((ML_HEADS, rows, ML_HD), F32),
        pltpu.VMEM((ML_HEADS, nb, ML_HD, CHUNK), F32),
        pltpu.VMEM((ML_HEADS, rows, S_COLS), BF16),
        pltpu.VMEM((nb * ML_HEADS, CHUNK, ML_HD), F32),
        pltpu.VMEM((3, CHUNK, LANES), F32),
        pltpu.VMEM((3, LANES, CHUNK), F32),
    ]
    return pl.pallas_call(
        functools.partial(_pmix_body, pad=pad, lead=lead),
        grid=(nc,),
        in_specs=in_specs,
        out_specs=out_specs,
        out_shape=out_shape,
        scratch_shapes=scratch,
        compiler_params=pltpu.CompilerParams(dimension_semantics=("arbitrary",), vmem_limit_bytes=VMEM_LIMIT),
        name=name,
    )(*in_arrays)


SBLK = 4 * SUBLANES


def _smix_body(x_ref, g1_ref, w_ref, wgi_ref, wgf_ref,
               rgh_ref, rgc_ref, mcv_ref, c_ref, n_ref, mcol_ref, mrow_ref,
               rcw_ref, rcb_ref, wa_ref, wx_ref, rba_ref, rbx_ref, lam_ref,
               mcw_ref, mcb_ref, wq_ref, wk_ref, wv_ref, bi_ref, bf_ref, bic_ref, bfc_ref, mng_ref,
               y_ref, rgh_o, rgc_o, mcv_o, n_o, m_o, inter_o, dk_o, v_o,
               ptm_ref, pbm_ref, q_s, qc_s, col_s):
    i = pl.program_id(0)
    ns = pbm_ref.shape[0]

    def conv(first, w_ref, b_ref, tail_ref, tail_o):
        x = _slabs(ptm_ref, first)
        acc = b_ref[...] + w_ref[CONV_W - 1:CONV_W, :] * x
        for j in range(CONV_W - 1):
            acc = acc + w_ref[j:j + 1, :] * tail_ref[j]
        for j in range(CONV_W - 2):
            tail_o[j] = tail_ref[j + 1]
        tail_o[CONV_W - 2] = x
        return acc, x

    @pl.when(i == 0)
    def _rowwise():
        u = _rms(x_ref[0], g1_ref[...]).astype(BF16)
        for n in range(D_MAIN // 512 - 1):
            _project(u, w_ref, n, ptm_ref, n * N_GROUPS, 1, ns)
        _project(u, w_ref, D_MAIN // 512 - 1, pbm_ref, 0, 1, ns)
        gi = _gate_preacts(u, wgi_ref, 1, ns)
        gf = _gate_preacts(u, wgf_ref, 1, ns)
        xc, _ = conv(0, rcw_ref, rcb_ref, rgc_ref, rgc_o)
        for g in range(N_GROUPS):
            sl = slice(g * LANES, (g + 1) * LANES)
            a, gx = _rg_gates(xc, wa_ref, wx_ref, rba_ref, rbx_ref, lam_ref, g)
            hn = a * rgh_ref[:, sl] + gx
            rgh_o[:, sl] = hn
            y_ref[0, :, sl] = (hn * _gelu(ptm_ref[N_GROUPS + g])).astype(BF16)

        mconv, mx = conv(2 * N_GROUPS, mcw_ref, mcb_ref, mcv_ref, mcv_o)
        mcb = (mconv * _sigmoid(mconv)).astype(BF16)
        mxb = mx.astype(BF16)

        ig_r = gi + bi_ref[...]
        lf_r = jax.nn.log_sigmoid(gf + bf_ref[...])
        m0_r = mrow_ref[...]
        m_o[...] = jnp.maximum(lf_r + m0_r, ig_r)
        z = jnp.concatenate([gi, gf, jnp.zeros((ns - 2 * GATE_ROWS, ns), F32)], axis=0).T
        ig_c = z[:, 0:GATE_ROWS] + bic_ref[...]
        lf_c = jax.nn.log_sigmoid(z[:, GATE_ROWS:2 * GATE_ROWS] + bfc_ref[...])
        m0_c = mcol_ref[...]
        m_c = jnp.maximum(lf_c + m0_c, ig_c)
        inter_c = jnp.exp(lf_c + m0_c - m_c)
        dd_c = jnp.exp(ig_c - m_c)
        inter_o[...] = inter_c
        col_s[0] = inter_c
        col_s[1] = jnp.exp(-m_c)
        for h in range(ML_HEADS):
            sl = slice(h * ML_HD, (h + 1) * ML_HD)
            q = jnp.dot(mcb[:, sl], wq_ref[h], preferred_element_type=F32) * (ML_HD ** -0.5)
            k = jnp.dot(mcb[:, sl], wk_ref[h], preferred_element_type=F32)
            v = jnp.dot(mxb[:, sl], wv_ref[h], preferred_element_type=F32)
            q_s[h] = q
            dk_o[h] = dd_c[:, h:h + 1] * k
            v_o[h] = v
            col_s[2, :, h:h + 1] = jnp.sum(q * k, axis=-1, keepdims=True) * dd_c[:, h:h + 1]
            n0 = n_ref[h]
            col_s[3, :, h:h + 1] = jnp.sum(q * n0, axis=-1, keepdims=True)
            n_o[h] = inter_c[:, h:h + 1] * n0 + dd_c[:, h:h + 1] * k

    blk = pl.ds(pl.multiple_of(i * SBLK, SBLK), SBLK)
    for h in range(ML_HEADS):
        z = jnp.concatenate([q_s[h, blk, :], jnp.zeros((ML_HD - SBLK, ML_HD), F32)], axis=0).T
        qc_s[h, blk, :] = jnp.concatenate(
            [jnp.sum(z[:, j:j + 1] * c_ref[0, j, h], axis=0, keepdims=True) for j in range(SBLK)], axis=0)

    @pl.when(i == pl.num_programs(0) - 1)
    def _finish():
        for h in range(ML_HEADS):
            sl = slice(h * ML_HD, (h + 1) * ML_HD)
            inter = col_s[0, :, h:h + 1]
            s = col_s[2, :, h:h + 1]
            num = s * v_o[h] + inter * qc_s[h]
            den = s + inter * col_s[3, :, h:h + 1]
            hout = num / jnp.maximum(jnp.abs(den), col_s[1, :, h:h + 1])
            hm = _sigmoid(pbm_ref[:, sl]) * hout
            hm = hm * lax.rsqrt(jnp.mean(hm * hm, axis=-1, keepdims=True) + EPS)
            y_ref[0, :, D_RG + h * ML_HD:D_RG + (h + 1) * ML_HD] = (hm * mng_ref[:, sl]).astype(BF16)


def _smix(x, st, lw, l, *, name):
    ns, d = x.shape[1:]
    rgh, rgc, mcv, mc_all, mn, mcol, mrow = st
    in_arrays = [x, lw["ln1"], lw["w_in"], lw["wgi"], lw["wgf"], rgh, rgc, mcv, mc_all, mn, mcol, mrow,
                 lw["rcw"], lw["rcb"], lw["wa"], lw["wx"], lw["rba"], lw["rbx"], lw["lam"],
                 lw["mcw"], lw["mcb"], lw["wq"], lw["wk"], lw["wv"], lw["bi_s"], lw["bf_s"],
                 lw["bi_c"], lw["bf_c"], lw["mng"]]
    in_specs = [_const_spec(x.shape)] + [_layer_spec(a, l) for a in in_arrays[1:]]
    in_specs[2] = pl.BlockSpec((None, d, D_MAIN), lambda i: (l, 0, 0), pipeline_mode=pl.Buffered(1))
    in_specs[8] = pl.BlockSpec((1, SBLK, ML_HEADS, ML_HD, ML_HD), lambda i: (l, i, 0, 0, 0))
    out_shape = [
        jax.ShapeDtypeStruct((1, ns, D_MODEL), BF16),
        jax.ShapeDtypeStruct((ns, D_RG), F32),
        jax.ShapeDtypeStruct((CONV_W - 1, ns, D_RG), F32),
        jax.ShapeDtypeStruct((CONV_W - 1, ns, D_ML), F32),
        jax.ShapeDtypeStruct((ML_HEADS, ns, ML_HD), F32),
        jax.ShapeDtypeStruct((GATE_ROWS, ns), F32),
        jax.ShapeDtypeStruct((ns, GATE_ROWS), F32),
        jax.ShapeDtypeStruct((ML_HEADS, ns, ML_HD), F32),
        jax.ShapeDtypeStruct((ML_HEADS, ns, ML_HD), F32),
    ]
    out_specs = [pl.BlockSpec(s.shape, lambda i, n=len(s.shape): (0,) * n) for s in out_shape]
    scratch = [
        pltpu.VMEM((N_SLABS, ns, LANES), F32),
        pltpu.VMEM((ns, D_ML), F32),
        pltpu.VMEM((ML_HEADS, ns, ML_HD), F32),
        pltpu.VMEM((ML_HEADS, ns, ML_HD), F32),
        pltpu.VMEM((4, ns, GATE_ROWS), F32),
    ]
    return pl.pallas_call(
        _smix_body,
        grid=(ns // SBLK,),
        in_specs=in_specs,
        out_specs=out_specs,
        out_shape=out_shape,
        scratch_shapes=scratch,
        compiler_params=pltpu.CompilerParams(dimension_semantics=("arbitrary",), vmem_limit_bytes=VMEM_LIMIT),
        name=name,
    )(*in_arrays)


CBLK = 4 * SUBLANES


def _cupdate_body(c_ref, inter_ref, dk_ref, v_ref, o_ref):
    for h in range(ML_HEADS):
        z = jnp.concatenate([dk_ref[0, h], jnp.zeros((ML_HD - CBLK, ML_HD), F32)], axis=0).T
        inter_rows = jnp.broadcast_to(inter_ref[0, :, h:h + 1], (CBLK, LANES))
        vblk = v_ref[0, h]
        for j in range(CBLK):
            o_ref[0, j, h] = inter_rows[j:j + 1, :] * c_ref[0, j, h] + z[:, j:j + 1] * vblk[j:j + 1, :]


def _cupdate(mc_all, inter_all, dk_all, v_all):
    depth, ns = mc_all.shape[:2]
    cspec = pl.BlockSpec((1, CBLK, ML_HEADS, ML_HD, ML_HD), lambda l, i: (l, i, 0, 0, 0))
    kvspec = pl.BlockSpec((1, ML_HEADS, CBLK, ML_HD), lambda l, i: (l, 0, i, 0))
    return pl.pallas_call(
        _cupdate_body,
        grid=(depth, ns // CBLK),
        in_specs=[cspec, pl.BlockSpec((1, CBLK, GATE_ROWS), lambda l, i: (l, i, 0)), kvspec, kvspec],
        out_specs=cspec,
        out_shape=jax.ShapeDtypeStruct(mc_all.shape, F32),
        compiler_params=pltpu.CompilerParams(dimension_semantics=("arbitrary", "arbitrary"),
                                             vmem_limit_bytes=VMEM_LIMIT),
        name="cupdate_s",
    )(mc_all, inter_all, dk_all, v_all)


def _block_diag_pairs(w):
    depth = w.shape[0]
    w = w.reshape(depth, N_GROUPS, 2, RG_BLOCK, RG_BLOCK)
    zero = jnp.zeros((depth, N_GROUPS, RG_BLOCK, RG_BLOCK), w.dtype)
    top = jnp.concatenate([w[:, :, 0], zero], axis=-1)
    bottom = jnp.concatenate([zero, w[:, :, 1]], axis=-1)
    return jnp.concatenate([top, bottom], axis=-2)


def _gate_rows(w_cols):
    depth = w_cols.shape[0]
    return jnp.concatenate([jnp.swapaxes(w_cols, 1, 2),
                            jnp.zeros((depth, GATE_ROWS - ML_HEADS, D_MODEL), w_cols.dtype)], axis=1)


def _stacked_weights(ln1_g, w_in, rg_conv_w, rg_conv_b, rg_w_a, rg_w_x, rg_b_a, rg_b_x, rg_lambda,
                     m_conv_w, m_conv_b, m_w_q, m_w_k, m_w_v, m_b_i, m_b_f, m_norm_g, w_out, ln2_g,
                     w_ff1, w_ff2, nb_prompt, ns):
    depth = ln1_g.shape[0]
    row = lambda a: a.reshape(depth, 1, -1)
    pad8 = lambda a: jnp.concatenate([a, jnp.zeros((depth, GATE_ROWS - ML_HEADS), F32)], axis=1)
    bias8, bfor8 = pad8(m_b_i), pad8(m_b_f)
    lanes = lambda a: jnp.concatenate([jnp.tile(a, (1, nb_prompt)),
                                       jnp.zeros((depth, LANES - nb_prompt * GATE_ROWS), F32)], axis=1)[:, None, :]
    return dict(
        ln1=row(ln1_g),
        w_in=w_in.astype(BF16),
        wgi=_gate_rows(w_in[:, :, D_MAIN:D_MAIN + ML_HEADS]).astype(BF16),
        wgf=_gate_rows(w_in[:, :, D_MAIN + ML_HEADS:]).astype(BF16),
        rcw=rg_conv_w, rcb=row(rg_conv_b),
        wa=_block_diag_pairs(rg_w_a).astype(BF16), wx=_block_diag_pairs(rg_w_x).astype(BF16),
        rba=row(rg_b_a), rbx=row(rg_b_x), lam=row(rg_lambda),
        mcw=m_conv_w, mcb=row(m_conv_b),
        wq=m_w_q.astype(BF16), wk=m_w_k.astype(BF16), wkt=jnp.swapaxes(m_w_k, 2, 3).astype(BF16),
        wv=m_w_v.astype(BF16),
        bi=lanes(bias8), bf=lanes(bfor8),
        bi_s=jnp.broadcast_to(bias8[:, :, None], (depth, GATE_ROWS, ns)),
        bf_s=jnp.broadcast_to(bfor8[:, :, None], (depth, GATE_ROWS, ns)),
        bi_c=bias8.reshape(depth, 1, GATE_ROWS), bf_c=bfor8.reshape(depth, 1, GATE_ROWS),
        mng=row(m_norm_g),
        w_out=w_out.astype(BF16), ln2=row(ln2_g),
        w_ff1=w_ff1.astype(BF16), w_ff2=w_ff2.astype(BF16),
    )


def kernel(x_prompt, x_sample, state_rg_h, state_rg_conv, state_m_conv, state_m_C, state_m_n, state_m_m,
           meta_tokens, ln1_g, w_in, rg_conv_w, rg_conv_b, rg_w_a, rg_w_x, rg_b_a, rg_b_x, rg_lambda,
           m_conv_w, m_conv_b, m_w_q, m_w_k, m_w_v, m_b_i, m_b_f, m_norm_g, w_out, ln2_g,
           w_ff1, w_ff2, ln_f_g):
    nb, seq, d = x_prompt.shape
    ns = x_sample.shape[0]
    depth = ln1_g.shape[0]
    t_real = N_META + seq
    nc = -(-t_real // CHUNK)
    pad = nc * CHUNK - t_real
    assert (pad + N_META) % CHUNK == 0 and x_sample.shape[1] == 1 and ns == LANES and depth == DEPTH
    assert CHUNK == LANES and nb * GATE_ROWS <= LANES

    lw = _stacked_weights(ln1_g, w_in, rg_conv_w, rg_conv_b, rg_w_a, rg_w_x, rg_b_a, rg_b_x, rg_lambda,
                          m_conv_w, m_conv_b, m_w_q, m_w_k, m_w_v, m_b_i, m_b_f, m_norm_g, w_out, ln2_g,
                          w_ff1, w_ff2, nb, ns)
    lnf = ln_f_g.reshape(1, d)
    meta = meta_tokens.astype(F32)
    xp = x_prompt
    xs = x_sample.reshape(1, ns, d)
    m_cols = jnp.concatenate([state_m_m, jnp.zeros((depth, ns, GATE_ROWS - ML_HEADS), F32)], axis=2)
    st_in = (state_rg_h, jnp.swapaxes(state_rg_conv, 1, 2), jnp.swapaxes(state_m_conv, 1, 2), state_m_C,
             jnp.swapaxes(state_m_n, 1, 2), m_cols, jnp.swapaxes(m_cols, 1, 2))

    p_states = [[] for _ in range(6)]
    s_states = [[] for _ in range(6)]
    for l in range(depth):
        last = l == depth - 1

        y, rgh, rgc, mcv, st, mrow = _pmix(xp, meta, lw, l, pad=pad, lead=l == 0, name=f"mixer_p{l}")
        lt_out = CHUNK // 2
        skip = (pad + N_META) // lt_out if last else 0
        xp = _outmlp(xp, y, lw, l, lnf, lt=lt_out, final=last, skip=skip,
                     nzero=0 if last else pad // lt_out, name=f"outmlp_p{l}", meta=meta if l == 0 else None)
        st = st.reshape(nb, ML_HEADS, ML_HD, S_COLS)
        p_states[0].append(rgh)
        p_states[1].append(jnp.swapaxes(rgc.reshape(CONV_W - 1, nb, D_RG), 0, 1))
        p_states[2].append(jnp.swapaxes(mcv.reshape(CONV_W - 1, nb, D_ML), 0, 1))
        p_states[3].append(st[..., :ML_HD])
        p_states[4].append(st[..., ML_HD])
        p_states[5].append(mrow[0, :nb * GATE_ROWS].reshape(nb, GATE_ROWS)[:, :ML_HEADS])

        ys, srgh, srgc, smcv, smn, smrow, sinter, sdk, sv = _smix(xs, st_in, lw, l, name=f"mixer_s{l}")
        xs = _outmlp_stream(xs, ys, lw, l, lnf, final=last, name=f"outmlp_s{l}")
        s_states[0].append(srgh)
        s_states[1].append(jnp.swapaxes(srgc, 0, 1))
        s_states[2].append(jnp.swapaxes(smcv, 0, 1))
        s_states[3].append((sinter, sdk, sv))
        s_states[4].append(jnp.swapaxes(smn, 0, 1))
        s_states[5].append(smrow[:ML_HEADS].T)

    y_prompt = xp
    y_sample = xs.reshape(ns, 1, d)
    ps_out = [jnp.stack(s) for s in p_states]
    c_new = _cupdate(state_m_C, *(jnp.stack([t[j] for t in s_states[3]]) for j in range(3)))
    ss_out = [c_new if j == 3 else jnp.stack(s) for j, s in enumerate(s_states)]
    return (y_prompt, y_sample, *ps_out, *ss_out)
```

```python
import functools

import jax
import jax.numpy as jnp
from jax import lax
from jax.experimental import pallas as pl
from jax.experimental.pallas import tpu as pltpu

F32 = jnp.float32
BF16 = jnp.bfloat16

D_MODEL = 1024
N_META = 16
D_RG = 512
RG_BLOCK = 64
RG_C = 8.0
D_ML = 512
ML_HEADS = 4
ML_HD = 128
CONV_W = 4
D_FF = 4096
EPS = 1e-6
DEPTH = 4

LANES = 128
SUBLANES = 8
CHUNK = 128
PITCH = CHUNK + SUBLANES
N_GROUPS = D_RG // LANES
D_MAIN = 2 * D_RG + 2 * D_ML
GATE_ROWS = SUBLANES
S_COLS = 2 * ML_HD
N_SLABS = 3 * N_GROUPS
FF_TILE = 1024
VMEM_LIMIT = 56 * 1024 * 1024

_NT = (((1,), (1,)), ((), ()))


def _const_spec(shape):
    zeros = (0,) * len(shape)
    return pl.BlockSpec(shape, lambda *_: zeros, pipeline_mode=pl.Buffered(1))


def _layer_spec(arr, l):
    tail = (0,) * (arr.ndim - 1)
    return pl.BlockSpec((None,) + arr.shape[1:], lambda *_: (l,) + tail, pipeline_mode=pl.Buffered(1))


def _rms(x, g):
    return x * lax.rsqrt(jnp.mean(x * x, axis=-1, keepdims=True) + EPS) * g


def _sigmoid(x):
    return 0.5 * jnp.tanh(0.5 * x) + 0.5


_GELU_K0 = 0.7978845608028654
_GELU_K1 = _GELU_K0 * 0.044715


def _gelu(x):
    hx = 0.5 * x
    return hx * jnp.tanh(x * (_GELU_K0 + _GELU_K1 * (x * x))) + hx


def _sqrt_nonneg(s):
    return jnp.where(s > 0.0, s * lax.rsqrt(s), 0.0)


def _project(u, w_ref, n, dst_ref, base, nb, lt):
    pr = jnp.dot(u, w_ref[:, n * 512:(n + 1) * 512], preferred_element_type=F32)
    if n == D_MAIN // 512 - 1:
        dst_ref[...] = pr
        return
    for g in range(N_GROUPS):
        cols = slice(g * LANES, (g + 1) * LANES)
        if nb == 1:
            dst_ref[base + g] = pr[:, cols]
            continue
        for b in range(nb):
            dst_ref[base + g, pl.ds(b, lt, stride=nb), :] = pr[b * lt:(b + 1) * lt, cols]


def _gate_preacts(u, wg_ref, nb, lt):
    return jnp.concatenate([lax.dot_general(wg_ref[...], u[b * lt:(b + 1) * lt], _NT, preferred_element_type=F32)
                            for b in range(nb)], axis=0)


def _lead_block(meta_ref, nb, lt):
    meta = meta_ref[...]
    blk = jnp.concatenate([jnp.zeros((lt - meta.shape[0], meta.shape[1]), F32), meta], axis=0)
    return jnp.broadcast_to(blk[None], (nb, lt, meta.shape[1]))


def _outmlp_body(x_ref, y_ref, wo_ref, g2_ref, w1_ref, w2_ref, gf_ref, *rest, final, nreal0, lead):
    o_ref = rest[-1]
    nb, lt, d = y_ref.shape

    def mlp(x, y):
        x1 = x + jnp.dot(y, wo_ref[...], preferred_element_type=F32)
        u2 = _rms(x1, g2_ref[...]).astype(BF16)
        acc = x1
        for c in range(D_FF // FF_TILE):
            cols = slice(c * FF_TILE, (c + 1) * FF_TILE)
            h = jnp.dot(u2, w1_ref[:, cols], preferred_element_type=F32)
            h = jnp.square(jnp.maximum(h, 0.0)).astype(BF16)
            acc = acc + jnp.dot(h, w2_ref[cols, :], preferred_element_type=F32)
        return _rms(acc, gf_ref[...]) if final else acc

    def compute():
        o_ref[...] = mlp(x_ref[...].reshape(nb * lt, d), y_ref[...].reshape(nb * lt, d)).reshape(nb, lt, d)

    def compute_block0():
        npad = lt - nreal0
        if lead:
            x = jnp.broadcast_to(rest[0][...][None], (nb, nreal0, d))
        else:
            x = x_ref[:, npad:, :]
        out = mlp(x.reshape(nb * nreal0, d), y_ref[:, npad:, :].reshape(nb * nreal0, d))
        o_ref[:, :npad, :] = jnp.zeros((nb, npad, d), F32)
        o_ref[:, npad:, :] = out.reshape(nb, nreal0, d)

    if nreal0:
        i = pl.program_id(0)
        pl.when(i == 0)(compute_block0)
        pl.when(i > 0)(compute)
    else:
        compute()


def _outmlp(x, y, lw, l, gf, *, lt, final, skip, nreal0, name, meta=None):
    nb, t, d = y.shape
    nsteps = t // lt - skip
    lead = meta is not None
    x_map = (lambda i: (0, jnp.maximum(i - 1, 0), 0)) if lead else (lambda i: (0, i + skip, 0))
    return pl.pallas_call(
        functools.partial(_outmlp_body, final=final, nreal0=nreal0, lead=lead),
        grid=(nsteps,),
        in_specs=[
            pl.BlockSpec((nb, lt, d), x_map),
            pl.BlockSpec((nb, lt, d), lambda i: (0, i + skip, 0)),
            _layer_spec(lw["w_out"], l),
            _layer_spec(lw["ln2"], l),
            _layer_spec(lw["w_ff1"], l),
            _layer_spec(lw["w_ff2"], l),
            _const_spec((1, d)),
        ] + ([_const_spec(meta.shape)] if lead else []),
        out_specs=pl.BlockSpec((nb, lt, d), lambda i: (0, i, 0)),
        out_shape=jax.ShapeDtypeStruct((nb, nsteps * lt, d), F32),
        compiler_params=pltpu.CompilerParams(dimension_semantics=("arbitrary",), vmem_limit_bytes=VMEM_LIMIT),
        name=name,
    )(x, y, lw["w_out"], lw["ln2"], lw["w_ff1"], lw["w_ff2"], gf, *([meta] if lead else []))


def _outmlp_stream_body(x_ref, y_ref, wo_ref, g2_ref, w1_ref, w2_ref, gf_ref, o_ref, u2_s, acc_s, *, final):
    j = pl.program_id(0)

    @pl.when(j == 0)
    def _head():
        x1 = x_ref[0] + jnp.dot(y_ref[0], wo_ref[...], preferred_element_type=F32)
        u2_s[...] = _rms(x1, g2_ref[...]).astype(BF16)
        acc_s[...] = x1

    h = jnp.dot(u2_s[...], w1_ref[...], preferred_element_type=F32)
    h = jnp.square(jnp.maximum(h, 0.0)).astype(BF16)
    acc_s[...] += jnp.dot(h, w2_ref[...], preferred_element_type=F32)

    @pl.when(j == pl.num_programs(0) - 1)
    def _tail():
        acc = acc_s[...]
        o_ref[0] = _rms(acc, gf_ref[...]) if final else acc


def _outmlp_stream(x, y, lw, l, gf, *, final, name):
    _, rows, d = x.shape
    return pl.pallas_call(
        functools.partial(_outmlp_stream_body, final=final),
        grid=(D_FF // FF_TILE,),
        in_specs=[
            _const_spec(x.shape),
            _const_spec(y.shape),
            _layer_spec(lw["w_out"], l),
            _layer_spec(lw["ln2"], l),
            pl.BlockSpec((None, d, FF_TILE), lambda j: (l, 0, j)),
            pl.BlockSpec((None, FF_TILE, d), lambda j: (l, j, 0)),
            _const_spec((1, d)),
        ],
        out_specs=pl.BlockSpec(x.shape, lambda j: (0, 0, 0)),
        out_shape=jax.ShapeDtypeStruct(x.shape, F32),
        scratch_shapes=[pltpu.VMEM((rows, d), BF16), pltpu.VMEM((rows, d), F32)],
        compiler_params=pltpu.CompilerParams(dimension_semantics=("arbitrary",), vmem_limit_bytes=VMEM_LIMIT),
        name=name,
    )(x, y, lw["w_out"], lw["ln2"], lw["w_ff1"], lw["w_ff2"], gf)


def _rg_gates(xc, wa_ref, wx_ref, rba_ref, rbx_ref, lam_ref, g):
    sl = slice(g * LANES, (g + 1) * LANES)
    xg = xc[:, sl]
    xb = xg.astype(BF16)
    i = _sigmoid(jnp.dot(xb, wx_ref[g], preferred_element_type=F32) + rbx_ref[:, sl])
    c = (0.5 * RG_C) * jax.nn.log_sigmoid(lam_ref[:, sl])
    pre = jnp.dot(xb, wa_ref[g], preferred_element_type=F32) + rba_ref[:, sl]
    a = jnp.exp(c * jnp.tanh(0.5 * pre) + c)
    gx = _sqrt_nonneg(1.0 - a * a) * (i * xg)
    return a, gx


def _row_scan(x, op, fill):
    row = lax.broadcasted_iota(jnp.int32, x.shape, 0)
    sh = 1
    while sh < x.shape[0]:
        x = op(x, jnp.where(row >= sh, pltpu.roll(x, sh, 0), fill))
        sh *= 2
    return x


def _slabs(ptm_ref, first):
    return jnp.concatenate([ptm_ref[first + g] for g in range(N_GROUPS)], axis=1)


def _pmix_body(x_ref, meta_ref, g1_ref, w_ref, wgi_ref, wgf_ref,
               rcw_ref, rcb_ref, wa_ref, wx_ref, rba_ref, rbx_ref, lam_ref,
               mcw_ref, mcb_ref, wq_ref, wkt_ref, wv_ref, bi_ref, bf_ref, mng_ref,
               y_ref, rgh_ref, rgc_ref, mcv_ref, st_ref, m_ref,
               px_s, pg_s, pm_s, po_s, a_s, g_s, bmc_s, bmx_s, yb_s, q_s, kt_s, v_s, ho_s, col_s, row_s, *, pad, lead):
    nb = y_ref.shape[0]
    lt = CHUNK
    rows = nb * lt
    ntail = (CONV_W - 1) * nb
    c = pl.program_id(0)

    @pl.when(c == 0)
    def _init():
        rgh_ref[...] = jnp.zeros_like(rgh_ref)
        rgc_ref[...] = jnp.zeros_like(rgc_ref)
        mcv_ref[...] = jnp.zeros_like(mcv_ref)
        st_ref[...] = jnp.zeros_like(st_ref)
        m_ref[...] = jnp.zeros_like(m_ref)
        ones_col = (lax.broadcasted_iota(jnp.int32, (rows, ML_HD), 1) == 0).astype(BF16)
        for h in range(ML_HEADS):
            v_s[h, :, ML_HD:S_COLS] = ones_col

    def conv(x, w_ref, b_ref, tail_ref):
        tail = tail_ref[...]
        acc = b_ref[...] + w_ref[CONV_W - 1:CONV_W, :] * x
        for k in range(1, CONV_W):
            shifted = jnp.concatenate([tail[ntail - k * nb:], x[:rows - k * nb]], axis=0)
            acc = acc + w_ref[CONV_W - 1 - k:CONV_W - k, :] * shifted
        tail_ref[...] = x[rows - ntail:]
        return acc

    x = x_ref[...]
    if lead:
        x = jnp.where(c == 0, _lead_block(meta_ref, nb, lt), x)
    u = _rms(x.reshape(rows, x.shape[2]), g1_ref[...]).astype(BF16)
    _project(u, w_ref, 2, pm_s, 0, nb, lt)
    gi = _gate_preacts(u, wgi_ref, nb, lt)
    gf = _gate_preacts(u, wgf_ref, nb, lt)

    mx = _slabs(pm_s, 0)
    mconv = conv(mx, mcw_ref, mcb_ref, mcv_ref)
    mc = mconv * _sigmoid(mconv)
    _project(u, w_ref, 0, px_s, 0, nb, lt)
    for t in range(lt):
        rs = slice(t * nb, (t + 1) * nb)
        idx = pl.ds(t, nb, stride=PITCH)
        for g in range(N_GROUPS):
            cols = slice(g * LANES, (g + 1) * LANES)
            bmc_s[g, idx, :] = mc[rs, cols]
            bmx_s[g, idx, :] = mx[rs, cols]

    def seq_major(src):
        return jnp.concatenate(
            [jnp.concatenate([src[g, b * PITCH:b * PITCH + lt, :] for b in range(nb)], axis=0)
             for g in range(N_GROUPS)], axis=1).astype(BF16)

    mcb = seq_major(bmc_s)
    mxb = seq_major(bmx_s)
    for h in range(ML_HEADS):
        sl = slice(h * ML_HD, (h + 1) * ML_HD)
        q_s[h] = jnp.dot(mcb[:, sl], wq_ref[h], preferred_element_type=F32) * (ML_HD ** -0.5)
        kt = lax.dot_general(wkt_ref[h], mcb[:, sl], _NT, preferred_element_type=F32)
        for b in range(nb):
            kt_s[h, b] = kt[:, b * lt:(b + 1) * lt]
        v_s[h, :, 0:ML_HD] = jnp.dot(mxb[:, sl], wv_ref[h], preferred_element_type=F32).astype(BF16)

    def to_cols(r):
        return jnp.concatenate([r, jnp.zeros((LANES - r.shape[0], lt), F32)], axis=0).T

    ig = to_cols(gi) + bi_ref[...]
    lf = jax.nn.log_sigmoid(to_cols(gf) + bf_ref[...])
    if pad:
        trow = lax.broadcasted_iota(jnp.int32, ig.shape, 0)
        is_pad = trow < jnp.where(c == 0, pad, 0)
        ig = jnp.where(is_pad, -1e30, ig)
        lf = jnp.where(is_pad, 0.0, lf)
    bcs = _row_scan(lf, jnp.add, 0.0)
    gg = ig - bcs
    cm = _row_scan(gg, jnp.maximum, -jnp.inf)
    m0 = m_ref[0:1, :]
    mm = jnp.maximum(m0, cm)
    mt = bcs + mm
    b_last = bcs[lt - 1:lt, :]
    m_last = mt[lt - 1:lt, :]
    col_s[0] = mm
    col_s[1] = jnp.exp(m0 - mm)
    col_s[2] = jnp.exp(-mt)
    row_s[0] = gg.T
    row_s[1] = jnp.exp(b_last - m_last + gg).T
    row_s[2] = jnp.broadcast_to(jnp.exp(b_last + m0 - m_last), (lt, LANES)).T
    m_ref[...] = jnp.broadcast_to(m_last, m_ref.shape)

    rg = {}

    def rg_conv():
        rg["xc"] = conv(_slabs(px_s, 0), rcw_ref, rcb_ref, rgc_ref)
        if pad:
            rowi = lax.broadcasted_iota(jnp.int32, (rows, 1), 0)
            rg["keep"] = rowi >= jnp.where(c == 0, pad * nb, 0)

    def rg_gates(g):
        a, gx = _rg_gates(rg["xc"], wa_ref, wx_ref, rba_ref, rbx_ref, lam_ref, g)
        if pad:
            gx = jnp.where(rg["keep"], gx, 0.0)
        a_s[g] = a
        g_s[g] = gx

    def rg_scan(g):
        sl = slice(g * LANES, (g + 1) * LANES)
        gate = _gelu(pg_s[g])
        h = rgh_ref[:, sl]
        for t in range(lt):
            rs = slice(t * nb, (t + 1) * nb)
            h = a_s[g, rs, :] * h + g_s[g, rs, :]
            yb_s[g, pl.ds(t, nb, stride=PITCH), :] = h * gate[rs]
        rgh_ref[:, sl] = h

    def rg_out(g):
        for b in range(nb):
            y_ref[b, :, g * LANES:(g + 1) * LANES] = yb_s[g, b * PITCH:b * PITCH + lt, :].astype(BF16)

    def gate_out(b):
        trows = slice(b * lt, (b + 1) * lt)
        for h in range(ML_HEADS):
            sl = slice(h * ML_HD, (h + 1) * ML_HD)
            hm = _sigmoid(po_s[trows, sl]) * ho_s[b * ML_HEADS + h]
            hm = hm * lax.rsqrt(jnp.mean(hm * hm, axis=-1, keepdims=True) + EPS)
            y_ref[b, :, D_RG + h * ML_HD:D_RG + (h + 1) * ML_HD] = (hm * mng_ref[:, sl]).astype(BF16)

    pieces = [[] for _ in range(nb + 1)]
    pieces[0].append(rg_conv)
    for g in range(N_GROUPS):
        pieces[g].append(functools.partial(rg_gates, g))
        pieces[g + 1].append(functools.partial(rg_scan, g))
        pieces[g + 2].append(functools.partial(rg_out, g))
    pieces[0].append(functools.partial(_project, u, w_ref, 1, pg_s, 0, nb, lt))
    pieces[1].append(functools.partial(_project, u, w_ref, 3, po_s, 0, nb, lt))
    for b in range(nb):
        pieces[max(b + 1, 2)].append(functools.partial(gate_out, b))

    t_i = lax.broadcasted_iota(jnp.int32, (lt, lt), 0)
    s_i = lax.broadcasted_iota(jnp.int32, (lt, lt), 1)
    causal = s_i <= t_i

    for b in range(nb):
        for piece in pieces[b]:
            piece()
        trows = slice(b * lt, (b + 1) * lt)
        heads = range(ML_HEADS)
        qb = [q_s[h, trows, :].astype(BF16) for h in heads]
        kt = [kt_s[h, b] for h in heads]
        s0 = [st_ref[b * ML_HEADS + h] for h in heads]
        sc = [jnp.dot(qb[h], kt[h].astype(BF16), preferred_element_type=F32) for h in heads]
        qs = [jnp.dot(qb[h], s0[h].astype(BF16), preferred_element_type=F32) for h in heads]
        pm = []
        for h in heads:
            r = b * GATE_ROWS + h
            d = jnp.exp(jnp.where(causal, row_s[0, r:r + 1, :] - col_s[0, :, r:r + 1], -jnp.inf))
            pm.append((sc[h] * d).astype(BF16))
        pv = [jnp.dot(pm[h], v_s[h, trows, :], preferred_element_type=F32) for h in heads]
        for h in heads:
            r = b * GATE_ROWS + h
            numx = pv[h] + col_s[1, :, r:r + 1] * qs[h]
            den = numx[:, ML_HD:ML_HD + 1]
            ho_s[b * ML_HEADS + h] = numx[:, 0:ML_HD] / jnp.maximum(jnp.abs(den), col_s[2, :, r:r + 1])
        for h in heads:
            r = b * GATE_ROWS + h
            wkt = (kt[h] * row_s[1, r:r + 1, :]).astype(BF16)
            sc_row = jnp.concatenate([row_s[2, r:r + 1, :], row_s[2, r:r + 1, :]], axis=1)
            st_ref[b * ML_HEADS + h] = sc_row * s0[h] + jnp.dot(wkt, v_s[h, trows, :], preferred_element_type=F32)
    for piece in pieces[nb]:
        piece()

    if pad:
        @pl.when(c == 0)
        def _zero_pad_rows():
            y_ref[:, 0:pad, :] = jnp.zeros((nb, pad, y_ref.shape[2]), BF16)


def _pmix(x, meta, lw, l, *, pad, lead, name):
    nb, t, d = x.shape
    nc = t // CHUNK + (1 if lead else 0)
    rows = nb * CHUNK
    ntail = (CONV_W - 1) * nb
    in_arrays = [x, meta, lw["ln1"], lw["w_in"], lw["wgi"], lw["wgf"],
                 lw["rcw"], lw["rcb"], lw["wa"], lw["wx"], lw["rba"], lw["rbx"], lw["lam"],
                 lw["mcw"], lw["mcb"], lw["wq"], lw["wkt"], lw["wv"], lw["bi"], lw["bf"], lw["mng"]]
    x_map = (lambda c: (0, jnp.maximum(c - 1, 0), 0)) if lead else (lambda c: (0, c, 0))
    in_specs = [pl.BlockSpec((nb, CHUNK, d), x_map), _const_spec(meta.shape)] + [
        _layer_spec(a, l) for a in in_arrays[2:]]
    in_specs[3] = pl.BlockSpec((None, d, D_MAIN), lambda c: (l, 0, 0), pipeline_mode=pl.Buffered(1))
    out_shape = [
        jax.ShapeDtypeStruct((nb, nc * CHUNK, D_MODEL), BF16),
        jax.ShapeDtypeStruct((nb, D_RG), F32),
        jax.ShapeDtypeStruct((ntail, D_RG), F32),
        jax.ShapeDtypeStruct((ntail, D_ML), F32),
        jax.ShapeDtypeStruct((nb * ML_HEADS, ML_HD, S_COLS), F32),
        jax.ShapeDtypeStruct((SUBLANES, LANES), F32),
    ]
    out_specs = [pl.BlockSpec((nb, CHUNK, D_MODEL), lambda c: (0, c, 0))] + [
        pl.BlockSpec(s.shape, lambda c, n=len(s.shape): (0,) * n) for s in out_shape[1:]]
    scratch = [
        pltpu.VMEM((N_GROUPS, rows, LANES), F32),
        pltpu.VMEM((N_GROUPS, rows, LANES), F32),
        pltpu.VMEM((N_GROUPS, rows, LANES), F32),
        pltpu.VMEM((rows, D_ML), F32),
        pltpu.VMEM((N_GROUPS, rows, LANES), F32),
        pltpu.VMEM((N_GROUPS, rows, LANES), F32),
        pltpu.VMEM((N_GROUPS, nb * PITCH, LANES), F32),
        pltpu.VMEM((N_GROUPS, nb * PITCH, LANES), F32),
        pltpu.VMEM((N_GROUPS, nb * PITCH, LANES), F32),
        pltpu.VMEM((ML_HEADS, rows, ML_HD), F32),
        pltpu.VMEM((ML_HEADS, nb, ML_HD, CHUNK), F32),
        pltpu.VMEM((ML_HEADS, rows, S_COLS), BF16),
        pltpu.VMEM((nb * ML_HEADS, CHUNK, ML_HD), F32),
        pltpu.VMEM((3, CHUNK, LANES), F32),
        pltpu.VMEM((3, LANES, CHUNK), F32),
    ]
    return pl.pallas_call(
        functools.partial(_pmix_body, pad=pad, lead=lead),
        grid=(nc,),
        in_specs=in_specs,
        out_specs=out_specs,
        out_shape=out_shape,
        scratch_shapes=scratch,
        compiler_params=pltpu.CompilerParams(dimension_semantics=("arbitrary",), vmem_limit_bytes=VMEM_LIMIT),
        name=name,
    )(*in_arrays)


SBLK = 4 * SUBLANES


def _smix_body(x_ref, g1_ref, w_ref, wgi_ref, wgf_ref,
               rgh_ref, rgc_ref, mcv_ref, c_ref, n_ref, mcol_ref, mrow_ref,
               rcw_ref, rcb_ref, wa_ref, wx_ref, rba_ref, rbx_ref, lam_ref,
               mcw_ref, mcb_ref, wq_ref, wk_ref, wv_ref, bi_ref, bf_ref, bic_ref, bfc_ref, mng_ref,
               y_ref, rgh_o, rgc_o, mcv_o, n_o, m_o, inter_o, dk_o, v_o,
               ptm_ref, pbm_ref, q_s, qc_s, col_s):
    i = pl.program_id(0)
    ns = pbm_ref.shape[0]

    def conv(first, w_ref, b_ref, tail_ref, tail_o):
        x = _slabs(ptm_ref, first)
        acc = b_ref[...] + w_ref[CONV_W - 1:CONV_W, :] * x
        for j in range(CONV_W - 1):
            acc = acc + w_ref[j:j + 1, :] * tail_ref[j]
        for j in range(CONV_W - 2):
            tail_o[j] = tail_ref[j + 1]
        tail_o[CONV_W - 2] = x
        return acc, x

    @pl.when(i == 0)
    def _rowwise():
        u = _rms(x_ref[0], g1_ref[...]).astype(BF16)
        for n in range(D_MAIN // 512 - 1):
            _project(u, w_ref, n, ptm_ref, n * N_GROUPS, 1, ns)
        _project(u, w_ref, D_MAIN // 512 - 1, pbm_ref, 0, 1, ns)
        gi = _gate_preacts(u, wgi_ref, 1, ns)
        gf = _gate_preacts(u, wgf_ref, 1, ns)
        xc, _ = conv(0, rcw_ref, rcb_ref, rgc_ref, rgc_o)
        for g in range(N_GROUPS):
            sl = slice(g * LANES, (g + 1) * LANES)
            a, gx = _rg_gates(xc, wa_ref, wx_ref, rba_ref, rbx_ref, lam_ref, g)
            hn = a * rgh_ref[:, sl] + gx
            rgh_o[:, sl] = hn
            y_ref[0, :, sl] = (hn * _gelu(ptm_ref[N_GROUPS + g])).astype(BF16)

        mconv, mx = conv(2 * N_GROUPS, mcw_ref, mcb_ref, mcv_ref, mcv_o)
        mcb = (mconv * _sigmoid(mconv)).astype(BF16)
        mxb = mx.astype(BF16)

        ig_r = gi + bi_ref[...]
        lf_r = jax.nn.log_sigmoid(gf + bf_ref[...])
        m0_r = mrow_ref[...]
        m_o[...] = jnp.maximum(lf_r + m0_r, ig_r)
        z = jnp.concatenate([gi, gf, jnp.zeros((ns - 2 * GATE_ROWS, ns), F32)], axis=0).T
        ig_c = z[:, 0:GATE_ROWS] + bic_ref[...]
        lf_c = jax.nn.log_sigmoid(z[:, GATE_ROWS:2 * GATE_ROWS] + bfc_ref[...])
        m0_c = mcol_ref[...]
        m_c = jnp.maximum(lf_c + m0_c, ig_c)
        inter_c = jnp.exp(lf_c + m0_c - m_c)
        dd_c = jnp.exp(ig_c - m_c)
        inter_o[...] = inter_c
        col_s[0] = inter_c
        col_s[1] = jnp.exp(-m_c)
        for h in range(ML_HEADS):
            sl = slice(h * ML_HD, (h + 1) * ML_HD)
            q = jnp.dot(mcb[:, sl], wq_ref[h], preferred_element_type=F32) * (ML_HD ** -0.5)
            k = jnp.dot(mcb[:, sl], wk_ref[h], preferred_element_type=F32)
            v = jnp.dot(mxb[:, sl], wv_ref[h], preferred_element_type=F32)
            q_s[h] = q
            dk_o[h] = dd_c[:, h:h + 1] * k
            v_o[h] = v
            col_s[2, :, h:h + 1] = jnp.sum(q * k, axis=-1, keepdims=True) * dd_c[:, h:h + 1]
            n0 = n_ref[h]
            col_s[3, :, h:h + 1] = jnp.sum(q * n0, axis=-1, keepdims=True)
            n_o[h] = inter_c[:, h:h + 1] * n0 + dd_c[:, h:h + 1] * k

    blk = pl.ds(pl.multiple_of(i * SBLK, SBLK), SBLK)
    for h in range(ML_HEADS):
        z = jnp.concatenate([q_s[h, blk, :], jnp.zeros((ML_HD - SBLK, ML_HD), F32)], axis=0).T
        qc_s[h, blk, :] = jnp.concatenate(
            [jnp.sum(z[:, j:j + 1] * c_ref[0, j, h], axis=0, keepdims=True) for j in range(SBLK)], axis=0)

    @pl.when(i == pl.num_programs(0) - 1)
    def _finish():
        for h in range(ML_HEADS):
            sl = slice(h * ML_HD, (h + 1) * ML_HD)
            inter = col_s[0, :, h:h + 1]
            s = col_s[2, :, h:h + 1]
            num = s * v_o[h] + inter * qc_s[h]
            den = s + inter * col_s[3, :, h:h + 1]
            hout = num / jnp.maximum(jnp.abs(den), col_s[1, :, h:h + 1])
            hm = _sigmoid(pbm_ref[:, sl]) * hout
            hm = hm * lax.rsqrt(jnp.mean(hm * hm, axis=-1, keepdims=True) + EPS)
            y_ref[0, :, D_RG + h * ML_HD:D_RG + (h + 1) * ML_HD] = (hm * mng_ref[:, sl]).astype(BF16)


def _smix(x, st, lw, l, *, name):
    ns, d = x.shape[1:]
    rgh, rgc, mcv, mc_all, mn, mcol, mrow = st
    in_arrays = [x, lw["ln1"], lw["w_in"], lw["wgi"], lw["wgf"], rgh, rgc, mcv, mc_all, mn, mcol, mrow,
                 lw["rcw"], lw["rcb"], lw["wa"], lw["wx"], lw["rba"], lw["rbx"], lw["lam"],
                 lw["mcw"], lw["mcb"], lw["wq"], lw["wk"], lw["wv"], lw["bi_s"], lw["bf_s"],
                 lw["bi_c"], lw["bf_c"], lw["mng"]]
    in_specs = [_const_spec(x.shape)] + [_layer_spec(a, l) for a in in_arrays[1:]]
    in_specs[2] = pl.BlockSpec((None, d, D_MAIN), lambda i: (l, 0, 0), pipeline_mode=pl.Buffered(1))
    in_specs[8] = pl.BlockSpec((1, SBLK, ML_HEADS, ML_HD, ML_HD), lambda i: (l, i, 0, 0, 0))
    out_shape = [
        jax.ShapeDtypeStruct((1, ns, D_MODEL), BF16),
        jax.ShapeDtypeStruct((ns, D_RG), F32),
        jax.ShapeDtypeStruct((CONV_W - 1, ns, D_RG), F32),
        jax.ShapeDtypeStruct((CONV_W - 1, ns, D_ML), F32),
        jax.ShapeDtypeStruct((ML_HEADS, ns, ML_HD), F32),
        jax.ShapeDtypeStruct((GATE_ROWS, ns), F32),
        jax.ShapeDtypeStruct((ns, GATE_ROWS), F32),
        jax.ShapeDtypeStruct((ML_HEADS, ns, ML_HD), F32),
        jax.ShapeDtypeStruct((ML_HEADS, ns, ML_HD), F32),
    ]
    out_specs = [pl.BlockSpec(s.shape, lambda i, n=len(s.shape): (0,) * n) for s in out_shape]
    scratch = [
        pltpu.VMEM((N_SLABS, ns, LANES), F32),
        pltpu.VMEM((ns, D_ML), F32),
        pltpu.VMEM((ML_HEADS, ns, ML_HD), F32),
        pltpu.VMEM((ML_HEADS, ns, ML_HD), F32),
        pltpu.VMEM((4, ns, GATE_ROWS), F32),
    ]
    return pl.pallas_call(
        _smix_body,
        grid=(ns // SBLK,),
        in_specs=in_specs,
        out_specs=out_specs,
        out_shape=out_shape,
        scratch_shapes=scratch,
        compiler_params=pltpu.CompilerParams(dimension_semantics=("arbitrary",), vmem_limit_bytes=VMEM_LIMIT),
        name=name,
    )(*in_arrays)


CBLK = 4 * SUBLANES


def _cupdate_body(c_ref, inter_ref, dk_ref, v_ref, o_ref):
    for h in range(ML_HEADS):
        z = jnp.concatenate([dk_ref[0, h], jnp.zeros((ML_HD - CBLK, ML_HD), F32)], axis=0).T
        inter_rows = jnp.broadcast_to(inter_ref[0, :, h:h + 1], (CBLK, LANES))
        vblk = v_ref[0, h]
        for j in range(CBLK):
            o_ref[0, j, h] = inter_rows[j:j + 1, :] * c_ref[0, j, h] + z[:, j:j + 1] * vblk[j:j + 1, :]


def _cupdate(mc_all, inter_all, dk_all, v_all):
    depth, ns = mc_all.shape[:2]
    cspec = pl.BlockSpec((1, CBLK, ML_HEADS, ML_HD, ML_HD), lambda l, i: (l, i, 0, 0, 0))
    kvspec = pl.BlockSpec((1, ML_HEADS, CBLK, ML_HD), lambda l, i: (l, 0, i, 0))
    return pl.pallas_call(
        _cupdate_body,
        grid=(depth, ns // CBLK),
        in_specs=[cspec, pl.BlockSpec((1, CBLK, GATE_ROWS), lambda l, i: (l, i, 0)), kvspec, kvspec],
        out_specs=cspec,
        out_shape=jax.ShapeDtypeStruct(mc_all.shape, F32),
        compiler_params=pltpu.CompilerParams(dimension_semantics=("arbitrary", "arbitrary"),
                                             vmem_limit_bytes=VMEM_LIMIT),
        name="cupdate_s",
    )(mc_all, inter_all, dk_all, v_all)


def _block_diag_pairs(w):
    depth = w.shape[0]
    w = w.reshape(depth, N_GROUPS, 2, RG_BLOCK, RG_BLOCK)
    zero = jnp.zeros((depth, N_GROUPS, RG_BLOCK, RG_BLOCK), w.dtype)
    top = jnp.concatenate([w[:, :, 0], zero], axis=-1)
    bottom = jnp.concatenate([zero, w[:, :, 1]], axis=-1)
    return jnp.concatenate([top, bottom], axis=-2)


def _gate_rows(w_cols):
    depth = w_cols.shape[0]
    return jnp.concatenate([jnp.swapaxes(w_cols, 1, 2),
                            jnp.zeros((depth, GATE_ROWS - ML_HEADS, D_MODEL), w_cols.dtype)], axis=1)


def _stacked_weights(ln1_g, w_in, rg_conv_w, rg_conv_b, rg_w_a, rg_w_x, rg_b_a, rg_b_x, rg_lambda,
                     m_conv_w, m_conv_b, m_w_q, m_w_k, m_w_v, m_b_i, m_b_f, m_norm_g, w_out, ln2_g,
                     w_ff1, w_ff2, nb_prompt, ns):
    depth = ln1_g.shape[0]
    row = lambda a: a.reshape(depth, 1, -1)
    pad8 = lambda a: jnp.concatenate([a, jnp.zeros((depth, GATE_ROWS - ML_HEADS), F32)], axis=1)
    bias8, bfor8 = pad8(m_b_i), pad8(m_b_f)
    lanes = lambda a: jnp.concatenate([jnp.tile(a, (1, nb_prompt)),
                                       jnp.zeros((depth, LANES - nb_prompt * GATE_ROWS), F32)], axis=1)[:, None, :]
    return dict(
        ln1=row(ln1_g),
        w_in=w_in.astype(BF16),
        wgi=_gate_rows(w_in[:, :, D_MAIN:D_MAIN + ML_HEADS]).astype(BF16),
        wgf=_gate_rows(w_in[:, :, D_MAIN + ML_HEADS:]).astype(BF16),
        rcw=rg_conv_w, rcb=row(rg_conv_b),
        wa=_block_diag_pairs(rg_w_a).astype(BF16), wx=_block_diag_pairs(rg_w_x).astype(BF16),
        rba=row(rg_b_a), rbx=row(rg_b_x), lam=row(rg_lambda),
        mcw=m_conv_w, mcb=row(m_conv_b),
        wq=m_w_q.astype(BF16), wk=m_w_k.astype(BF16), wkt=jnp.swapaxes(m_w_k, 2, 3).astype(BF16),
        wv=m_w_v.astype(BF16),
        bi=lanes(bias8), bf=lanes(bfor8),
        bi_s=jnp.broadcast_to(bias8[:, :, None], (depth, GATE_ROWS, ns)),
        bf_s=jnp.broadcast_to(bfor8[:, :, None], (depth, GATE_ROWS, ns)),
        bi_c=bias8.reshape(depth, 1, GATE_ROWS), bf_c=bfor8.reshape(depth, 1, GATE_ROWS),
        mng=row(m_norm_g),
        w_out=w_out.astype(BF16), ln2=row(ln2_g),
        w_ff1=w_ff1.astype(BF16), w_ff2=w_ff2.astype(BF16),
    )


def kernel(x_prompt, x_sample, state_rg_h, state_rg_conv, state_m_conv, state_m_C, state_m_n, state_m_m,
           meta_tokens, ln1_g, w_in, rg_conv_w, rg_conv_b, rg_w_a, rg_w_x, rg_b_a, rg_b_x, rg_lambda,
           m_conv_w, m_conv_b, m_w_q, m_w_k, m_w_v, m_b_i, m_b_f, m_norm_g, w_out, ln2_g,
           w_ff1, w_ff2, ln_f_g):
    nb, seq, d = x_prompt.shape
    ns = x_sample.shape[0]
    depth = ln1_g.shape[0]
    t_real = N_META + seq
    nc = -(-t_real // CHUNK)
    pad = nc * CHUNK - t_real
    assert (pad + N_META) % CHUNK == 0 and x_sample.shape[1] == 1 and ns == LANES and depth == DEPTH
    assert CHUNK == LANES and nb * GATE_ROWS <= LANES

    lw = _stacked_weights(ln1_g, w_in, rg_conv_w, rg_conv_b, rg_w_a, rg_w_x, rg_b_a, rg_b_x, rg_lambda,
                          m_conv_w, m_conv_b, m_w_q, m_w_k, m_w_v, m_b_i, m_b_f, m_norm_g, w_out, ln2_g,
                          w_ff1, w_ff2, nb, ns)
    lnf = ln_f_g.reshape(1, d)
    meta = meta_tokens.astype(F32)
    xp = x_prompt
    xs = x_sample.reshape(1, ns, d)
    m_cols = jnp.concatenate([state_m_m, jnp.zeros((depth, ns, GATE_ROWS - ML_HEADS), F32)], axis=2)
    st_in = (state_rg_h, jnp.swapaxes(state_rg_conv, 1, 2), jnp.swapaxes(state_m_conv, 1, 2), state_m_C,
             jnp.swapaxes(state_m_n, 1, 2), m_cols, jnp.swapaxes(m_cols, 1, 2))

    p_states = [[] for _ in range(6)]
    s_states = [[] for _ in range(6)]
    for l in range(depth):
        last = l == depth - 1

        y, rgh, rgc, mcv, st, mrow = _pmix(xp, meta, lw, l, pad=pad, lead=l == 0, name=f"mixer_p{l}")
        xp = _outmlp(xp, y, lw, l, lnf, lt=CHUNK, final=last, skip=(pad + N_META) // CHUNK if last else 0,
                     nreal0=0 if last else CHUNK - pad, name=f"outmlp_p{l}", meta=meta if l == 0 else None)
        st = st.reshape(nb, ML_HEADS, ML_HD, S_COLS)
        p_states[0].append(rgh)
        p_states[1].append(jnp.swapaxes(rgc.reshape(CONV_W - 1, nb, D_RG), 0, 1))
        p_states[2].append(jnp.swapaxes(mcv.reshape(CONV_W - 1, nb, D_ML), 0, 1))
        p_states[3].append(st[..., :ML_HD])
        p_states[4].append(st[..., ML_HD])
        p_states[5].append(mrow[0, :nb * GATE_ROWS].reshape(nb, GATE_ROWS)[:, :ML_HEADS])

        ys, srgh, srgc, smcv, smn, smrow, sinter, sdk, sv = _smix(xs, st_in, lw, l, name=f"mixer_s{l}")
        xs = _outmlp_stream(xs, ys, lw, l, lnf, final=last, name=f"outmlp_s{l}")
        s_states[0].append(srgh)
        s_states[1].append(jnp.swapaxes(srgc, 0, 1))
        s_states[2].append(jnp.swapaxes(smcv, 0, 1))
        s_states[3].append((sinter, sdk, sv))
        s_states[4].append(jnp.swapaxes(smn, 0, 1))
        s_states[5].append(smrow[:ML_HEADS].T)

    y_prompt = xp
    y_sample = xs.reshape(ns, 1, d)
    ps_out = [jnp.stack(s) for s in p_states]
    c_new = _cupdate(state_m_C, *(jnp.stack([t[j] for t in s_states[3]]) for j in range(3)))
    ss_out = [c_new if j == 3 else jnp.stack(s) for j, s in enumerate(s_states)]
    return (y_prompt, y_sample, *ps_out, *ss_out)
```

```python
import functools

import jax
import jax.numpy as jnp
from jax import lax
from jax.experimental import pallas as pl
from jax.experimental.pallas import tpu as pltpu

F32 = jnp.float32
BF16 = jnp.bfloat16

D_MODEL = 1024
N_META = 16
D_RG = 512
RG_BLOCK = 64
RG_C = 8.0
D_ML = 512
ML_HEADS = 4
ML_HD = 128
CONV_W = 4
D_FF = 4096
EPS = 1e-6
DEPTH = 4

LANES = 128
SUBLANES = 8
CHUNK = 128
PITCH = CHUNK + SUBLANES
N_GROUPS = D_RG // LANES
D_MAIN = 2 * D_RG + 2 * D_ML
GATE_ROWS = SUBLANES
S_COLS = 2 * ML_HD
N_SLABS = 3 * N_GROUPS
FF_TILE = 1024
VMEM_LIMIT = 56 * 1024 * 1024

_NT = (((1,), (1,)), ((), ()))


def _const_spec(shape):
    zeros = (0,) * len(shape)
    return pl.BlockSpec(shape, lambda *_: zeros, pipeline_mode=pl.Buffered(1))


def _layer_spec(arr, l):
    tail = (0,) * (arr.ndim - 1)
    return pl.BlockSpec((None,) + arr.shape[1:], lambda *_: (l,) + tail, pipeline_mode=pl.Buffered(1))


def _rms(x, g):
    return x * lax.rsqrt(jnp.mean(x * x, axis=-1, keepdims=True) + EPS) * g


def _sigmoid(x):
    return 0.5 * jnp.tanh(0.5 * x) + 0.5


_GELU_K0 = 0.7978845608028654
_GELU_K1 = _GELU_K0 * 0.044715


def _gelu(x):
    hx = 0.5 * x
    return hx * jnp.tanh(x * (_GELU_K0 + _GELU_K1 * (x * x))) + hx


def _sqrt_nonneg(s):
    return jnp.where(s > 0.0, s * lax.rsqrt(s), 0.0)


def _project(u, w_ref, n, dst_ref, base, nb, lt):
    pr = jnp.dot(u, w_ref[:, n * 512:(n + 1) * 512], preferred_element_type=F32)
    if n == D_MAIN // 512 - 1:
        dst_ref[...] = pr
        return
    for g in range(N_GROUPS):
        cols = slice(g * LANES, (g + 1) * LANES)
        if nb == 1:
            dst_ref[base + g] = pr[:, cols]
            continue
        for b in range(nb):
            dst_ref[base + g, pl.ds(b, lt, stride=nb), :] = pr[b * lt:(b + 1) * lt, cols]


def _gate_preacts(u, wg_ref, nb, lt):
    return jnp.concatenate([lax.dot_general(wg_ref[...], u[b * lt:(b + 1) * lt], _NT, preferred_element_type=F32)
                            for b in range(nb)], axis=0)


def _lead_block(meta_ref, nb, lt):
    meta = meta_ref[...]
    blk = jnp.concatenate([jnp.zeros((lt - meta.shape[0], meta.shape[1]), F32), meta], axis=0)
    return jnp.broadcast_to(blk[None], (nb, lt, meta.shape[1]))


def _outmlp_body(x_ref, y_ref, wo_ref, g2_ref, w1_ref, w2_ref, gf_ref, *rest, final, nreal0, lead):
    xs_ref, ys_ref, o_ref, os_ref = rest[-4:]
    nb, lt, d = y_ref.shape
    i = pl.program_id(0)

    def mlp(x, y):
        x1 = x + jnp.dot(y, wo_ref[...], preferred_element_type=F32)
        u2 = _rms(x1, g2_ref[...]).astype(BF16)
        acc = x1
        for c in range(D_FF // FF_TILE):
            cols = slice(c * FF_TILE, (c + 1) * FF_TILE)
            h = jnp.dot(u2, w1_ref[:, cols], preferred_element_type=F32)
            h = jnp.square(jnp.maximum(h, 0.0)).astype(BF16)
            acc = acc + jnp.dot(h, w2_ref[cols, :], preferred_element_type=F32)
        return _rms(acc, gf_ref[...]) if final else acc

    def compute():
        o_ref[...] = mlp(x_ref[...].reshape(nb * lt, d), y_ref[...].reshape(nb * lt, d)).reshape(nb, lt, d)

    def compute_sample():
        os_ref[0] = mlp(xs_ref[0], ys_ref[0])

    def compute_block0():
        npad = lt - nreal0
        rows0 = nb * nreal0
        if lead:
            x = jnp.broadcast_to(rest[0][...][None], (nb, nreal0, d))
        else:
            x = x_ref[:, npad:, :]
        x = jnp.concatenate([x.reshape(rows0, d), xs_ref[0]], axis=0)
        y = jnp.concatenate([y_ref[:, npad:, :].reshape(rows0, d), ys_ref[0]], axis=0)
        out = mlp(x, y)
        o_ref[:, :npad, :] = jnp.zeros((nb, npad, d), F32)
        o_ref[:, npad:, :] = out[:rows0].reshape(nb, nreal0, d)
        os_ref[0] = out[rows0:]

    if nreal0:
        pl.when(i == 0)(compute_block0)
        pl.when(i > 0)(compute)
    else:
        pl.when(i == 0)(compute_sample)
        compute()


def _outmlp(x, y, xs, ys, lw, l, gf, *, lt, final, skip, nreal0, name, meta=None):
    nb, t, d = y.shape
    nsteps = t // lt - skip
    lead = meta is not None
    x_map = (lambda i: (0, jnp.maximum(i - 1, 0), 0)) if lead else (lambda i: (0, i + skip, 0))
    return pl.pallas_call(
        functools.partial(_outmlp_body, final=final, nreal0=nreal0, lead=lead),
        grid=(nsteps,),
        in_specs=[
            pl.BlockSpec((nb, lt, d), x_map),
            pl.BlockSpec((nb, lt, d), lambda i: (0, i + skip, 0)),
            _layer_spec(lw["w_out"], l),
            _layer_spec(lw["ln2"], l),
            _layer_spec(lw["w_ff1"], l),
            _layer_spec(lw["w_ff2"], l),
            _const_spec((1, d)),
        ] + ([_const_spec(meta.shape)] if lead else []) + [_const_spec(xs.shape), _const_spec(ys.shape)],
        out_specs=(pl.BlockSpec((nb, lt, d), lambda i: (0, i, 0)), pl.BlockSpec(xs.shape, lambda i: (0, 0, 0))),
        out_shape=(jax.ShapeDtypeStruct((nb, nsteps * lt, d), F32), jax.ShapeDtypeStruct(xs.shape, F32)),
        compiler_params=pltpu.CompilerParams(dimension_semantics=("arbitrary",), vmem_limit_bytes=VMEM_LIMIT),
        name=name,
    )(x, y, lw["w_out"], lw["ln2"], lw["w_ff1"], lw["w_ff2"], gf, *([meta] if lead else []), xs, ys)


def _rg_gates(xc, wa_ref, wx_ref, rba_ref, rbx_ref, lam_ref, g):
    sl = slice(g * LANES, (g + 1) * LANES)
    xg = xc[:, sl]
    xb = xg.astype(BF16)
    i = _sigmoid(jnp.dot(xb, wx_ref[g], preferred_element_type=F32) + rbx_ref[:, sl])
    c = (0.5 * RG_C) * jax.nn.log_sigmoid(lam_ref[:, sl])
    pre = jnp.dot(xb, wa_ref[g], preferred_element_type=F32) + rba_ref[:, sl]
    a = jnp.exp(c * jnp.tanh(0.5 * pre) + c)
    gx = _sqrt_nonneg(1.0 - a * a) * (i * xg)
    return a, gx


def _row_scan(x, op, fill):
    row = lax.broadcasted_iota(jnp.int32, x.shape, 0)
    sh = 1
    while sh < x.shape[0]:
        x = op(x, jnp.where(row >= sh, pltpu.roll(x, sh, 0), fill))
        sh *= 2
    return x


def _slabs(ptm_ref, first):
    return jnp.concatenate([ptm_ref[first + g] for g in range(N_GROUPS)], axis=1)


def _pmix_body(x_ref, meta_ref, g1_ref, w_ref, wgi_ref, wgf_ref,
               rcw_ref, rcb_ref, wa_ref, wx_ref, rba_ref, rbx_ref, lam_ref,
               mcw_ref, mcb_ref, wq_ref, wkt_ref, wv_ref, bi_ref, bf_ref, mng_ref,
               y_ref, rgh_ref, rgc_ref, mcv_ref, st_ref, m_ref,
               px_s, pg_s, pm_s, po_s, a_s, g_s, bmc_s, bmx_s, yb_s, q_s, kt_s, v_s, ho_s, col_s, row_s, *, pad, lead):
    nb = y_ref.shape[0]
    lt = CHUNK
    rows = nb * lt
    ntail = (CONV_W - 1) * nb
    c = pl.program_id(0)

    @pl.when(c == 0)
    def _init():
        rgh_ref[...] = jnp.zeros_like(rgh_ref)
        rgc_ref[...] = jnp.zeros_like(rgc_ref)
        mcv_ref[...] = jnp.zeros_like(mcv_ref)
        st_ref[...] = jnp.zeros_like(st_ref)
        m_ref[...] = jnp.zeros_like(m_ref)
        ones_col = (lax.broadcasted_iota(jnp.int32, (rows, ML_HD), 1) == 0).astype(BF16)
        for h in range(ML_HEADS):
            v_s[h, :, ML_HD:S_COLS] = ones_col

    def conv(x, w_ref, b_ref, tail_ref):
        tail = tail_ref[...]
        acc = b_ref[...] + w_ref[CONV_W - 1:CONV_W, :] * x
        for k in range(1, CONV_W):
            shifted = jnp.concatenate([tail[ntail - k * nb:], x[:rows - k * nb]], axis=0)
            acc = acc + w_ref[CONV_W - 1 - k:CONV_W - k, :] * shifted
        tail_ref[...] = x[rows - ntail:]
        return acc

    x = x_ref[...]
    if lead:
        x = jnp.where(c == 0, _lead_block(meta_ref, nb, lt), x)
    u = _rms(x.reshape(rows, x.shape[2]), g1_ref[...]).astype(BF16)
    _project(u, w_ref, 2, pm_s, 0, nb, lt)
    gi = _gate_preacts(u, wgi_ref, nb, lt)
    gf = _gate_preacts(u, wgf_ref, nb, lt)

    mx = _slabs(pm_s, 0)
    mconv = conv(mx, mcw_ref, mcb_ref, mcv_ref)
    mc = mconv * _sigmoid(mconv)
    _project(u, w_ref, 0, px_s, 0, nb, lt)
    for t in range(lt):
        rs = slice(t * nb, (t + 1) * nb)
        idx = pl.ds(t, nb, stride=PITCH)
        for g in range(N_GROUPS):
            cols = slice(g * LANES, (g + 1) * LANES)
            bmc_s[g, idx, :] = mc[rs, cols]
            bmx_s[g, idx, :] = mx[rs, cols]

    def seq_major(src):
        return jnp.concatenate(
            [jnp.concatenate([src[g, b * PITCH:b * PITCH + lt, :] for b in range(nb)], axis=0)
             for g in range(N_GROUPS)], axis=1).astype(BF16)

    mcb = seq_major(bmc_s)
    mxb = seq_major(bmx_s)
    for h in range(ML_HEADS):
        sl = slice(h * ML_HD, (h + 1) * ML_HD)
        q_s[h] = jnp.dot(mcb[:, sl], wq_ref[h], preferred_element_type=F32) * (ML_HD ** -0.5)
        kt = lax.dot_general(wkt_ref[h], mcb[:, sl], _NT, preferred_element_type=F32)
        for b in range(nb):
            kt_s[h, b] = kt[:, b * lt:(b + 1) * lt]
        v_s[h, :, 0:ML_HD] = jnp.dot(mxb[:, sl], wv_ref[h], preferred_element_type=F32).astype(BF16)

    def to_cols(r):
        return jnp.concatenate([r, jnp.zeros((LANES - r.shape[0], lt), F32)], axis=0).T

    ig = to_cols(gi) + bi_ref[...]
    lf = jax.nn.log_sigmoid(to_cols(gf) + bf_ref[...])
    if pad:
        trow = lax.broadcasted_iota(jnp.int32, ig.shape, 0)
        is_pad = trow < jnp.where(c == 0, pad, 0)
        ig = jnp.where(is_pad, -1e30, ig)
        lf = jnp.where(is_pad, 0.0, lf)
    bcs = _row_scan(lf, jnp.add, 0.0)
    gg = ig - bcs
    cm = _row_scan(gg, jnp.maximum, -jnp.inf)
    m0 = m_ref[0:1, :]
    mm = jnp.maximum(m0, cm)
    mt = bcs + mm
    b_last = bcs[lt - 1:lt, :]
    m_last = mt[lt - 1:lt, :]
    col_s[0] = mm
    col_s[1] = jnp.exp(m0 - mm)
    col_s[2] = jnp.exp(-mt)
    row_s[0] = gg.T
    row_s[1] = jnp.exp(b_last - m_last + gg).T
    row_s[2] = jnp.broadcast_to(jnp.exp(b_last + m0 - m_last), (lt, LANES)).T
    m_ref[...] = jnp.broadcast_to(m_last, m_ref.shape)

    rg = {}

    def rg_conv():
        rg["xc"] = conv(_slabs(px_s, 0), rcw_ref, rcb_ref, rgc_ref)
        if pad:
            rowi = lax.broadcasted_iota(jnp.int32, (rows, 1), 0)
            rg["keep"] = rowi >= jnp.where(c == 0, pad * nb, 0)

    def rg_gates(g):
        a, gx = _rg_gates(rg["xc"], wa_ref, wx_ref, rba_ref, rbx_ref, lam_ref, g)
        if pad:
            gx = jnp.where(rg["keep"], gx, 0.0)
        a_s[g] = a
        g_s[g] = gx

    def rg_scan(g):
        sl = slice(g * LANES, (g + 1) * LANES)
        gate = _gelu(pg_s[g])
        h = rgh_ref[:, sl]
        for t in range(lt):
            rs = slice(t * nb, (t + 1) * nb)
            h = a_s[g, rs, :] * h + g_s[g, rs, :]
            yb_s[g, pl.ds(t, nb, stride=PITCH), :] = h * gate[rs]
        rgh_ref[:, sl] = h

    def rg_out(g):
        for b in range(nb):
            y_ref[b, :, g * LANES:(g + 1) * LANES] = yb_s[g, b * PITCH:b * PITCH + lt, :].astype(BF16)

    def gate_out(b):
        trows = slice(b * lt, (b + 1) * lt)
        for h in range(ML_HEADS):
            sl = slice(h * ML_HD, (h + 1) * ML_HD)
            hm = _sigmoid(po_s[trows, sl]) * ho_s[b * ML_HEADS + h]
            hm = hm * lax.rsqrt(jnp.mean(hm * hm, axis=-1, keepdims=True) + EPS)
            y_ref[b, :, D_RG + h * ML_HD:D_RG + (h + 1) * ML_HD] = (hm * mng_ref[:, sl]).astype(BF16)

    pieces = [[] for _ in range(nb + 1)]
    pieces[0].append(rg_conv)
    for g in range(N_GROUPS):
        pieces[g].append(functools.partial(rg_gates, g))
        pieces[g + 1].append(functools.partial(rg_scan, g))
        pieces[g + 2].append(functools.partial(rg_out, g))
    pieces[0].append(functools.partial(_project, u, w_ref, 1, pg_s, 0, nb, lt))
    pieces[1].append(functools.partial(_project, u, w_ref, 3, po_s, 0, nb, lt))
    for b in range(nb):
        pieces[max(b + 1, 2)].append(functools.partial(gate_out, b))

    t_i = lax.broadcasted_iota(jnp.int32, (lt, lt), 0)
    s_i = lax.broadcasted_iota(jnp.int32, (lt, lt), 1)
    causal = s_i <= t_i

    for b in range(nb):
        for piece in pieces[b]:
            piece()
        trows = slice(b * lt, (b + 1) * lt)
        heads = range(ML_HEADS)
        qb = [q_s[h, trows, :].astype(BF16) for h in heads]
        kt = [kt_s[h, b] for h in heads]
        s0 = [st_ref[b * ML_HEADS + h] for h in heads]
        sc = [jnp.dot(qb[h], kt[h].astype(BF16), preferred_element_type=F32) for h in heads]
        qs = [jnp.dot(qb[h], s0[h].astype(BF16), preferred_element_type=F32) for h in heads]
        pm = []
        for h in heads:
            r = b * GATE_ROWS + h
            d = jnp.exp(jnp.where(causal, row_s[0, r:r + 1, :] - col_s[0, :, r:r + 1], -jnp.inf))
            pm.append((sc[h] * d).astype(BF16))
        pv = [jnp.dot(pm[h], v_s[h, trows, :], preferred_element_type=F32) for h in heads]
        for h in heads:
            r = b * GATE_ROWS + h
            numx = pv[h] + col_s[1, :, r:r + 1] * qs[h]
            den = numx[:, ML_HD:ML_HD + 1]
            ho_s[b * ML_HEADS + h] = numx[:, 0:ML_HD] / jnp.maximum(jnp.abs(den), col_s[2, :, r:r + 1])
        for h in heads:
            r = b * GATE_ROWS + h
            wkt = (kt[h] * row_s[1, r:r + 1, :]).astype(BF16)
            sc_row = jnp.concatenate([row_s[2, r:r + 1, :], row_s[2, r:r + 1, :]], axis=1)
            st_ref[b * ML_HEADS + h] = sc_row * s0[h] + jnp.dot(wkt, v_s[h, trows, :], preferred_element_type=F32)
    for piece in pieces[nb]:
        piece()

    if pad:
        @pl.when(c == 0)
        def _zero_pad_rows():
            y_ref[:, 0:pad, :] = jnp.zeros((nb, pad, y_ref.shape[2]), BF16)


def _pmix(x, meta, lw, l, *, pad, lead, name):
    nb, t, d = x.shape
    nc = t // CHUNK + (1 if lead else 0)
    rows = nb * CHUNK
    ntail = (CONV_W - 1) * nb
    in_arrays = [x, meta, lw["ln1"], lw["w_in"], lw["wgi"], lw["wgf"],
                 lw["rcw"], lw["rcb"], lw["wa"], lw["wx"], lw["rba"], lw["rbx"], lw["lam"],
                 lw["mcw"], lw["mcb"], lw["wq"], lw["wkt"], lw["wv"], lw["bi"], lw["bf"], lw["mng"]]
    x_map = (lambda c: (0, jnp.maximum(c - 1, 0), 0)) if lead else (lambda c: (0, c, 0))
    in_specs = [pl.BlockSpec((nb, CHUNK, d), x_map), _const_spec(meta.shape)] + [
        _layer_spec(a, l) for a in in_arrays[2:]]
    in_specs[3] = pl.BlockSpec((None, d, D_MAIN), lambda c: (l, 0, 0), pipeline_mode=pl.Buffered(1))
    out_shape = [
        jax.ShapeDtypeStruct((nb, nc * CHUNK, D_MODEL), BF16),
        jax.ShapeDtypeStruct((nb, D_RG), F32),
        jax.ShapeDtypeStruct((ntail, D_RG), F32),
        jax.ShapeDtypeStruct((ntail, D_ML), F32),
        jax.ShapeDtypeStruct((nb * ML_HEADS, ML_HD, S_COLS), F32),
        jax.ShapeDtypeStruct((SUBLANES, LANES), F32),
    ]
    out_specs = [pl.BlockSpec((nb, CHUNK, D_MODEL), lambda c: (0, c, 0))] + [
        pl.BlockSpec(s.shape, lambda c, n=len(s.shape): (0,) * n) for s in out_shape[1:]]
    scratch = [
        pltpu.VMEM((N_GROUPS, rows, LANES), F32),
        pltpu.VMEM((N_GROUPS, rows, LANES), F32),
        pltpu.VMEM((N_GROUPS, rows, LANES), F32),
        pltpu.VMEM((rows, D_ML), F32),
        pltpu.VMEM((N_GROUPS, rows, LANES), F32),
        pltpu.VMEM((N_GROUPS, rows, LANES), F32),
        pltpu.VMEM((N_GROUPS, nb * PITCH, LANES), F32),
        pltpu.VMEM((N_GROUPS, nb * PITCH, LANES), F32),
        pltpu.VMEM((N_GROUPS, nb * PITCH, LANES), F32),
        pltpu.VMEM((ML_HEADS, rows, ML_HD), F32),
        pltpu.VMEM((ML_HEADS, nb, ML_HD, CHUNK), F32),
        pltpu.VMEM((ML_HEADS, rows, S_COLS), BF16),
        pltpu.VMEM((nb * ML_HEADS, CHUNK, ML_HD), F32),
        pltpu.VMEM((3, CHUNK, LANES), F32),
        pltpu.VMEM((3, LANES, CHUNK), F32),
    ]
    return pl.pallas_call(
        functools.partial(_pmix_body, pad=pad, lead=lead),
        grid=(nc,),
        in_specs=in_specs,
        out_specs=out_specs,
        out_shape=out_shape,
        scratch_shapes=scratch,
        compiler_params=pltpu.CompilerParams(dimension_semantics=("arbitrary",), vmem_limit_bytes=VMEM_LIMIT),
        name=name,
    )(*in_arrays)


SBLK = 4 * SUBLANES


def _smix_body(x_ref, g1_ref, w_ref, wgi_ref, wgf_ref,
               rgh_ref, rgc_ref, mcv_ref, c_ref, n_ref, mcol_ref, mrow_ref,
               rcw_ref, rcb_ref, wa_ref, wx_ref, rba_ref, rbx_ref, lam_ref,
               mcw_ref, mcb_ref, wq_ref, wk_ref, wv_ref, bi_ref, bf_ref, bic_ref, bfc_ref, mng_ref,
               y_ref, rgh_o, rgc_o, mcv_o, n_o, m_o, inter_o, dk_o, v_o,
               ptm_ref, pbm_ref, q_s, qc_s, col_s):
    i = pl.program_id(0)
    ns = pbm_ref.shape[0]

    def conv(first, w_ref, b_ref, tail_ref, tail_o):
        x = _slabs(ptm_ref, first)
        acc = b_ref[...] + w_ref[CONV_W - 1:CONV_W, :] * x
        for j in range(CONV_W - 1):
            acc = acc + w_ref[j:j + 1, :] * tail_ref[j]
        for j in range(CONV_W - 2):
            tail_o[j] = tail_ref[j + 1]
        tail_o[CONV_W - 2] = x
        return acc, x

    @pl.when(i == 0)
    def _rowwise():
        u = _rms(x_ref[0], g1_ref[...]).astype(BF16)
        for n in range(D_MAIN // 512 - 1):
            _project(u, w_ref, n, ptm_ref, n * N_GROUPS, 1, ns)
        _project(u, w_ref, D_MAIN // 512 - 1, pbm_ref, 0, 1, ns)
        gi = _gate_preacts(u, wgi_ref, 1, ns)
        gf = _gate_preacts(u, wgf_ref, 1, ns)
        xc, _ = conv(0, rcw_ref, rcb_ref, rgc_ref, rgc_o)
        for g in range(N_GROUPS):
            sl = slice(g * LANES, (g + 1) * LANES)
            a, gx = _rg_gates(xc, wa_ref, wx_ref, rba_ref, rbx_ref, lam_ref, g)
            hn = a * rgh_ref[:, sl] + gx
            rgh_o[:, sl] = hn
            y_ref[0, :, sl] = (hn * _gelu(ptm_ref[N_GROUPS + g])).astype(BF16)

        mconv, mx = conv(2 * N_GROUPS, mcw_ref, mcb_ref, mcv_ref, mcv_o)
        mcb = (mconv * _sigmoid(mconv)).astype(BF16)
        mxb = mx.astype(BF16)

        ig_r = gi + bi_ref[...]
        lf_r = jax.nn.log_sigmoid(gf + bf_ref[...])
        m0_r = mrow_ref[...]
        m_o[...] = jnp.maximum(lf_r + m0_r, ig_r)
        z = jnp.concatenate([gi, gf, jnp.zeros((ns - 2 * GATE_ROWS, ns), F32)], axis=0).T
        ig_c = z[:, 0:GATE_ROWS] + bic_ref[...]
        lf_c = jax.nn.log_sigmoid(z[:, GATE_ROWS:2 * GATE_ROWS] + bfc_ref[...])
        m0_c = mcol_ref[...]
        m_c = jnp.maximum(lf_c + m0_c, ig_c)
        inter_c = jnp.exp(lf_c + m0_c - m_c)
        dd_c = jnp.exp(ig_c - m_c)
        inter_o[...] = inter_c
        col_s[0] = inter_c
        col_s[1] = jnp.exp(-m_c)
        for h in range(ML_HEADS):
            sl = slice(h * ML_HD, (h + 1) * ML_HD)
            q = jnp.dot(mcb[:, sl], wq_ref[h], preferred_element_type=F32) * (ML_HD ** -0.5)
            k = jnp.dot(mcb[:, sl], wk_ref[h], preferred_element_type=F32)
            v = jnp.dot(mxb[:, sl], wv_ref[h], preferred_element_type=F32)
            q_s[h] = q
            dk_o[h] = dd_c[:, h:h + 1] * k
            v_o[h] = v
            col_s[2, :, h:h + 1] = jnp.sum(q * k, axis=-1, keepdims=True) * dd_c[:, h:h + 1]
            n0 = n_ref[h]
            col_s[3, :, h:h + 1] = jnp.sum(q * n0, axis=-1, keepdims=True)
            n_o[h] = inter_c[:, h:h + 1] * n0 + dd_c[:, h:h + 1] * k

    blk = pl.ds(pl.multiple_of(i * SBLK, SBLK), SBLK)
    for h in range(ML_HEADS):
        z = jnp.concatenate([q_s[h, blk, :], jnp.zeros((ML_HD - SBLK, ML_HD), F32)], axis=0).T
        qc_s[h, blk, :] = jnp.concatenate(
            [jnp.sum(z[:, j:j + 1] * c_ref[0, j, h], axis=0, keepdims=True) for j in range(SBLK)], axis=0)

    @pl.when(i == pl.num_programs(0) - 1)
    def _finish():
        for h in range(ML_HEADS):
            sl = slice(h * ML_HD, (h + 1) * ML_HD)
            inter = col_s[0, :, h:h + 1]
            s = col_s[2, :, h:h + 1]
            num = s * v_o[h] + inter * qc_s[h]
            den = s + inter * col_s[3, :, h:h + 1]
            hout = num / jnp.maximum(jnp.abs(den), col_s[1, :, h:h + 1])
            hm = _sigmoid(pbm_ref[:, sl]) * hout
            hm = hm * lax.rsqrt(jnp.mean(hm * hm, axis=-1, keepdims=True) + EPS)
            y_ref[0, :, D_RG + h * ML_HD:D_RG + (h + 1) * ML_HD] = (hm * mng_ref[:, sl]).astype(BF16)


def _smix(x, st, lw, l, *, name):
    ns, d = x.shape[1:]
    rgh, rgc, mcv, mc_all, mn, mcol, mrow = st
    in_arrays = [x, lw["ln1"], lw["w_in"], lw["wgi"], lw["wgf"], rgh, rgc, mcv, mc_all, mn, mcol, mrow,
                 lw["rcw"], lw["rcb"], lw["wa"], lw["wx"], lw["rba"], lw["rbx"], lw["lam"],
                 lw["mcw"], lw["mcb"], lw["wq"], lw["wk"], lw["wv"], lw["bi_s"], lw["bf_s"],
                 lw["bi_c"], lw["bf_c"], lw["mng"]]
    in_specs = [_const_spec(x.shape)] + [_layer_spec(a, l) for a in in_arrays[1:]]
    in_specs[2] = pl.BlockSpec((None, d, D_MAIN), lambda i: (l, 0, 0), pipeline_mode=pl.Buffered(1))
    in_specs[8] = pl.BlockSpec((1, SBLK, ML_HEADS, ML_HD, ML_HD), lambda i: (l, i, 0, 0, 0))
    out_shape = [
        jax.ShapeDtypeStruct((1, ns, D_MODEL), BF16),
        jax.ShapeDtypeStruct((ns, D_RG), F32),
        jax.ShapeDtypeStruct((CONV_W - 1, ns, D_RG), F32),
        jax.ShapeDtypeStruct((CONV_W - 1, ns, D_ML), F32),
        jax.ShapeDtypeStruct((ML_HEADS, ns, ML_HD), F32),
        jax.ShapeDtypeStruct((GATE_ROWS, ns), F32),
        jax.ShapeDtypeStruct((ns, GATE_ROWS), F32),
        jax.ShapeDtypeStruct((ML_HEADS, ns, ML_HD), F32),
        jax.ShapeDtypeStruct((ML_HEADS, ns, ML_HD), F32),
    ]
    out_specs = [pl.BlockSpec(s.shape, lambda i, n=len(s.shape): (0,) * n) for s in out_shape]
    scratch = [
        pltpu.VMEM((N_SLABS, ns, LANES), F32),
        pltpu.VMEM((ns, D_ML), F32),
        pltpu.VMEM((ML_HEADS, ns, ML_HD), F32),
        pltpu.VMEM((ML_HEADS, ns, ML_HD), F32),
        pltpu.VMEM((4, ns, GATE_ROWS), F32),
    ]
    return pl.pallas_call(
        _smix_body,
        grid=(ns // SBLK,),
        in_specs=in_specs,
        out_specs=out_specs,
        out_shape=out_shape,
        scratch_shapes=scratch,
        compiler_params=pltpu.CompilerParams(dimension_semantics=("arbitrary",), vmem_limit_bytes=VMEM_LIMIT),
        name=name,
    )(*in_arrays)


CBLK = 4 * SUBLANES


def _cupdate_body(c_ref, inter_ref, dk_ref, v_ref, o_ref):
    for h in range(ML_HEADS):
        z = jnp.concatenate([dk_ref[0, h], jnp.zeros((ML_HD - CBLK, ML_HD), F32)], axis=0).T
        inter_rows = jnp.broadcast_to(inter_ref[0, :, h:h + 1], (CBLK, LANES))
        vblk = v_ref[0, h]
        for j in range(CBLK):
            o_ref[0, j, h] = inter_rows[j:j + 1, :] * c_ref[0, j, h] + z[:, j:j + 1] * vblk[j:j + 1, :]


def _cupdate(mc_all, inter_all, dk_all, v_all):
    depth, ns = mc_all.shape[:2]
    cspec = pl.BlockSpec((1, CBLK, ML_HEADS, ML_HD, ML_HD), lambda l, i: (l, i, 0, 0, 0))
    kvspec = pl.BlockSpec((1, ML_HEADS, CBLK, ML_HD), lambda l, i: (l, 0, i, 0))
    return pl.pallas_call(
        _cupdate_body,
        grid=(depth, ns // CBLK),
        in_specs=[cspec, pl.BlockSpec((1, CBLK, GATE_ROWS), lambda l, i: (l, i, 0)), kvspec, kvspec],
        out_specs=cspec,
        out_shape=jax.ShapeDtypeStruct(mc_all.shape, F32),
        compiler_params=pltpu.CompilerParams(dimension_semantics=("arbitrary", "arbitrary"),
                                             vmem_limit_bytes=VMEM_LIMIT),
        name="cupdate_s",
    )(mc_all, inter_all, dk_all, v_all)


def _block_diag_pairs(w):
    depth = w.shape[0]
    w = w.reshape(depth, N_GROUPS, 2, RG_BLOCK, RG_BLOCK)
    zero = jnp.zeros((depth, N_GROUPS, RG_BLOCK, RG_BLOCK), w.dtype)
    top = jnp.concatenate([w[:, :, 0], zero], axis=-1)
    bottom = jnp.concatenate([zero, w[:, :, 1]], axis=-1)
    return jnp.concatenate([top, bottom], axis=-2)


def _gate_rows(w_cols):
    depth = w_cols.shape[0]
    return jnp.concatenate([jnp.swapaxes(w_cols, 1, 2),
                            jnp.zeros((depth, GATE_ROWS - ML_HEADS, D_MODEL), w_cols.dtype)], axis=1)


def _stacked_weights(ln1_g, w_in, rg_conv_w, rg_conv_b, rg_w_a, rg_w_x, rg_b_a, rg_b_x, rg_lambda,
                     m_conv_w, m_conv_b, m_w_q, m_w_k, m_w_v, m_b_i, m_b_f, m_norm_g, w_out, ln2_g,
                     w_ff1, w_ff2, nb_prompt, ns):
    depth = ln1_g.shape[0]
    row = lambda a: a.reshape(depth, 1, -1)
    pad8 = lambda a: jnp.concatenate([a, jnp.zeros((depth, GATE_ROWS - ML_HEADS), F32)], axis=1)
    bias8, bfor8 = pad8(m_b_i), pad8(m_b_f)
    lanes = lambda a: jnp.concatenate([jnp.tile(a, (1, nb_prompt)),
                                       jnp.zeros((depth, LANES - nb_prompt * GATE_ROWS), F32)], axis=1)[:, None, :]
    return dict(
        ln1=row(ln1_g),
        w_in=w_in.astype(BF16),
        wgi=_gate_rows(w_in[:, :, D_MAIN:D_MAIN + ML_HEADS]).astype(BF16),
        wgf=_gate_rows(w_in[:, :, D_MAIN + ML_HEADS:]).astype(BF16),
        rcw=rg_conv_w, rcb=row(rg_conv_b),
        wa=_block_diag_pairs(rg_w_a).astype(BF16), wx=_block_diag_pairs(rg_w_x).astype(BF16),
        rba=row(rg_b_a), rbx=row(rg_b_x), lam=row(rg_lambda),
        mcw=m_conv_w, mcb=row(m_conv_b),
        wq=m_w_q.astype(BF16), wk=m_w_k.astype(BF16), wkt=jnp.swapaxes(m_w_k, 2, 3).astype(BF16),
        wv=m_w_v.astype(BF16),
        bi=lanes(bias8), bf=lanes(bfor8),
        bi_s=jnp.broadcast_to(bias8[:, :, None], (depth, GATE_ROWS, ns)),
        bf_s=jnp.broadcast_to(bfor8[:, :, None], (depth, GATE_ROWS, ns)),
        bi_c=bias8.reshape(depth, 1, GATE_ROWS), bf_c=bfor8.reshape(depth, 1, GATE_ROWS),
        mng=row(m_norm_g),
        w_out=w_out.astype(BF16), ln2=row(ln2_g),
        w_ff1=w_ff1.astype(BF16), w_ff2=w_ff2.astype(BF16),
    )


def kernel(x_prompt, x_sample, state_rg_h, state_rg_conv, state_m_conv, state_m_C, state_m_n, state_m_m,
           meta_tokens, ln1_g, w_in, rg_conv_w, rg_conv_b, rg_w_a, rg_w_x, rg_b_a, rg_b_x, rg_lambda,
           m_conv_w, m_conv_b, m_w_q, m_w_k, m_w_v, m_b_i, m_b_f, m_norm_g, w_out, ln2_g,
           w_ff1, w_ff2, ln_f_g):
    nb, seq, d = x_prompt.shape
    ns = x_sample.shape[0]
    depth = ln1_g.shape[0]
    t_real = N_META + seq
    nc = -(-t_real // CHUNK)
    pad = nc * CHUNK - t_real
    assert (pad + N_META) % CHUNK == 0 and x_sample.shape[1] == 1 and ns == LANES and depth == DEPTH
    assert CHUNK == LANES and nb * GATE_ROWS <= LANES

    lw = _stacked_weights(ln1_g, w_in, rg_conv_w, rg_conv_b, rg_w_a, rg_w_x, rg_b_a, rg_b_x, rg_lambda,
                          m_conv_w, m_conv_b, m_w_q, m_w_k, m_w_v, m_b_i, m_b_f, m_norm_g, w_out, ln2_g,
                          w_ff1, w_ff2, nb, ns)
    lnf = ln_f_g.reshape(1, d)
    meta = meta_tokens.astype(F32)
    xp = x_prompt
    xs = x_sample.reshape(1, ns, d)
    m_cols = jnp.concatenate([state_m_m, jnp.zeros((depth, ns, GATE_ROWS - ML_HEADS), F32)], axis=2)
    st_in = (state_rg_h, jnp.swapaxes(state_rg_conv, 1, 2), jnp.swapaxes(state_m_conv, 1, 2), state_m_C,
             jnp.swapaxes(state_m_n, 1, 2), m_cols, jnp.swapaxes(m_cols, 1, 2))

    p_states = [[] for _ in range(6)]
    s_states = [[] for _ in range(6)]
    for l in range(depth):
        last = l == depth - 1

        y, rgh, rgc, mcv, st, mrow = _pmix(xp, meta, lw, l, pad=pad, lead=l == 0, name=f"mixer_p{l}")
        st = st.reshape(nb, ML_HEADS, ML_HD, S_COLS)
        p_states[0].append(rgh)
        p_states[1].append(jnp.swapaxes(rgc.reshape(CONV_W - 1, nb, D_RG), 0, 1))
        p_states[2].append(jnp.swapaxes(mcv.reshape(CONV_W - 1, nb, D_ML), 0, 1))
        p_states[3].append(st[..., :ML_HD])
        p_states[4].append(st[..., ML_HD])
        p_states[5].append(mrow[0, :nb * GATE_ROWS].reshape(nb, GATE_ROWS)[:, :ML_HEADS])

        ys, srgh, srgc, smcv, smn, smrow, sinter, sdk, sv = _smix(xs, st_in, lw, l, name=f"mixer_s{l}")

        xp, xs = _outmlp(xp, y, xs, ys, lw, l, lnf, lt=CHUNK, final=last, skip=(pad + N_META) // CHUNK if last else 0,
                         nreal0=0 if last else CHUNK - pad, name=f"outmlp{l}", meta=meta if l == 0 else None)
        s_states[0].append(srgh)
        s_states[1].append(jnp.swapaxes(srgc, 0, 1))
        s_states[2].append(jnp.swapaxes(smcv, 0, 1))
        s_states[3].append((sinter, sdk, sv))
        s_states[4].append(jnp.swapaxes(smn, 0, 1))
        s_states[5].append(smrow[:ML_HEADS].T)

    y_prompt = xp
    y_sample = xs.reshape(ns, 1, d)
    ps_out = [jnp.stack(s) for s in p_states]
    c_new = _cupdate(state_m_C, *(jnp.stack([t[j] for t in s_states[3]]) for j in range(3)))
    ss_out = [c_new if j == 3 else jnp.stack(s) for j, s in enumerate(s_states)]
    return (y_prompt, y_sample, *ps_out, *ss_out)
```

```python
import functools

import jax
import jax.numpy as jnp
from jax import lax
from jax.experimental import pallas as pl
from jax.experimental.pallas import tpu as pltpu

F32 = jnp.float32
BF16 = jnp.bfloat16

D_MODEL = 1024
N_META = 16
D_RG = 512
RG_BLOCK = 64
RG_C = 8.0
D_ML = 512
ML_HEADS = 4
ML_HD = 128
CONV_W = 4
D_FF = 4096
EPS = 1e-6
DEPTH = 4

LANES = 128
SUBLANES = 8
CHUNK = 128
PITCH = CHUNK + SUBLANES
N_GROUPS = D_RG // LANES
D_MAIN = 2 * D_RG + 2 * D_ML
GATE_ROWS = SUBLANES
S_COLS = 2 * ML_HD
N_SLABS = 3 * N_GROUPS
FF_TILE = 1024
VMEM_LIMIT = 56 * 1024 * 1024
VMEM_LIMIT_SAMPLE = 28 * 1024 * 1024

_NT = (((1,), (1,)), ((), ()))


def _const_spec(shape):
    zeros = (0,) * len(shape)
    return pl.BlockSpec(shape, lambda *_: zeros, pipeline_mode=pl.Buffered(1))


def _layer_spec(arr, l):
    tail = (0,) * (arr.ndim - 1)
    return pl.BlockSpec((None,) + arr.shape[1:], lambda *_: (l,) + tail, pipeline_mode=pl.Buffered(1))


def _rms(x, g):
    return x * lax.rsqrt(jnp.mean(x * x, axis=-1, keepdims=True) + EPS) * g


def _sigmoid(x):
    return 0.5 * jnp.tanh(0.5 * x) + 0.5


_GELU_K0 = 0.7978845608028654
_GELU_K1 = _GELU_K0 * 0.044715


def _gelu(x):
    hx = 0.5 * x
    return hx * jnp.tanh(x * (_GELU_K0 + _GELU_K1 * (x * x))) + hx


def _sqrt_nonneg(s):
    return jnp.where(s > 0.0, s * lax.rsqrt(s), 0.0)


def _project(u, w_ref, n, dst_ref, base, nb, lt):
    pr = jnp.dot(u, w_ref[:, n * 512:(n + 1) * 512], preferred_element_type=F32)
    if n == D_MAIN // 512 - 1:
        dst_ref[...] = pr
        return
    for g in range(N_GROUPS):
        cols = slice(g * LANES, (g + 1) * LANES)
        if nb == 1:
            dst_ref[base + g] = pr[:, cols]
            continue
        for b in range(nb):
            dst_ref[base + g, pl.ds(b, lt, stride=nb), :] = pr[b * lt:(b + 1) * lt, cols]


def _gate_preacts(u, wg_ref, nb, lt):
    return jnp.concatenate([lax.dot_general(wg_ref[...], u[b * lt:(b + 1) * lt], _NT, preferred_element_type=F32)
                            for b in range(nb)], axis=0)


def _lead_block(meta_ref, nb, lt):
    meta = meta_ref[...]
    blk = jnp.concatenate([jnp.zeros((lt - meta.shape[0], meta.shape[1]), F32), meta], axis=0)
    return jnp.broadcast_to(blk[None], (nb, lt, meta.shape[1]))


def _outmlp_body(x_ref, y_ref, wo_ref, g2_ref, w1_ref, w2_ref, gf_ref, *rest, final, nreal0, lead):
    o_ref = rest[-1]
    nb, lt, d = y_ref.shape

    def mlp(x, y):
        x1 = x + jnp.dot(y, wo_ref[...], preferred_element_type=F32)
        u2 = _rms(x1, g2_ref[...]).astype(BF16)
        acc = x1
        for c in range(D_FF // FF_TILE):
            cols = slice(c * FF_TILE, (c + 1) * FF_TILE)
            h = jnp.dot(u2, w1_ref[:, cols], preferred_element_type=F32)
            h = jnp.square(jnp.maximum(h, 0.0)).astype(BF16)
            acc = acc + jnp.dot(h, w2_ref[cols, :], preferred_element_type=F32)
        return _rms(acc, gf_ref[...]) if final else acc

    def compute():
        o_ref[...] = mlp(x_ref[...].reshape(nb * lt, d), y_ref[...].reshape(nb * lt, d)).reshape(nb, lt, d)

    def compute_block0():
        npad = lt - nreal0
        if lead:
            x = jnp.broadcast_to(rest[0][...][None], (nb, nreal0, d))
        else:
            x = x_ref[:, npad:, :]
        out = mlp(x.reshape(nb * nreal0, d), y_ref[:, npad:, :].reshape(nb * nreal0, d))
        o_ref[:, :npad, :] = jnp.zeros((nb, npad, d), F32)
        o_ref[:, npad:, :] = out.reshape(nb, nreal0, d)

    if nreal0:
        i = pl.program_id(0)
        pl.when(i == 0)(compute_block0)
        pl.when(i > 0)(compute)
    else:
        compute()


def _outmlp(x, y, lw, l, gf, *, lt, final, skip, nreal0, name, meta=None):
    nb, t, d = y.shape
    nsteps = t // lt - skip
    lead = meta is not None
    x_map = (lambda i: (0, jnp.maximum(i - 1, 0), 0)) if lead else (lambda i: (0, i + skip, 0))
    return pl.pallas_call(
        functools.partial(_outmlp_body, final=final, nreal0=nreal0, lead=lead),
        grid=(nsteps,),
        in_specs=[
            pl.BlockSpec((nb, lt, d), x_map),
            pl.BlockSpec((nb, lt, d), lambda i: (0, i + skip, 0)),
            _layer_spec(lw["w_out"], l),
            _layer_spec(lw["ln2"], l),
            _layer_spec(lw["w_ff1"], l),
            _layer_spec(lw["w_ff2"], l),
            _const_spec((1, d)),
        ] + ([_const_spec(meta.shape)] if lead else []),
        out_specs=pl.BlockSpec((nb, lt, d), lambda i: (0, i, 0)),
        out_shape=jax.ShapeDtypeStruct((nb, nsteps * lt, d), F32),
        compiler_params=pltpu.CompilerParams(dimension_semantics=("arbitrary",), vmem_limit_bytes=VMEM_LIMIT),
        name=name,
    )(x, y, lw["w_out"], lw["ln2"], lw["w_ff1"], lw["w_ff2"], gf, *([meta] if lead else []))


def _outmlp_stream_body(x_ref, y_ref, wo_ref, g2_ref, w1_ref, w2_ref, gf_ref, o_ref, u2_s, acc_s, *, final):
    j = pl.program_id(0)

    @pl.when(j == 0)
    def _head():
        x1 = x_ref[0] + jnp.dot(y_ref[0], wo_ref[...], preferred_element_type=F32)
        u2_s[...] = _rms(x1, g2_ref[...]).astype(BF16)
        acc_s[...] = x1

    h = jnp.dot(u2_s[...], w1_ref[...], preferred_element_type=F32)
    h = jnp.square(jnp.maximum(h, 0.0)).astype(BF16)
    acc_s[...] += jnp.dot(h, w2_ref[...], preferred_element_type=F32)

    @pl.when(j == pl.num_programs(0) - 1)
    def _tail():
        acc = acc_s[...]
        o_ref[0] = _rms(acc, gf_ref[...]) if final else acc


def _outmlp_stream(x, y, lw, l, gf, *, final, name):
    _, rows, d = x.shape
    return pl.pallas_call(
        functools.partial(_outmlp_stream_body, final=final),
        grid=(D_FF // FF_TILE,),
        in_specs=[
            _const_spec(x.shape),
            _const_spec(y.shape),
            _layer_spec(lw["w_out"], l),
            _layer_spec(lw["ln2"], l),
            pl.BlockSpec((None, d, FF_TILE), lambda j: (l, 0, j)),
            pl.BlockSpec((None, FF_TILE, d), lambda j: (l, j, 0)),
            _const_spec((1, d)),
        ],
        out_specs=pl.BlockSpec(x.shape, lambda j: (0, 0, 0)),
        out_shape=jax.ShapeDtypeStruct(x.shape, F32),
        scratch_shapes=[pltpu.VMEM((rows, d), BF16), pltpu.VMEM((rows, d), F32)],
        compiler_params=pltpu.CompilerParams(dimension_semantics=("arbitrary",), vmem_limit_bytes=VMEM_LIMIT_SAMPLE),
        name=name,
    )(x, y, lw["w_out"], lw["ln2"], lw["w_ff1"], lw["w_ff2"], gf)


def _rg_gates(xc, wa_ref, wx_ref, rba_ref, rbx_ref, lam_ref, g):
    sl = slice(g * LANES, (g + 1) * LANES)
    xg = xc[:, sl]
    xb = xg.astype(BF16)
    i = _sigmoid(jnp.dot(xb, wx_ref[g], preferred_element_type=F32) + rbx_ref[:, sl])
    c = (0.5 * RG_C) * jax.nn.log_sigmoid(lam_ref[:, sl])
    pre = jnp.dot(xb, wa_ref[g], preferred_element_type=F32) + rba_ref[:, sl]
    a = jnp.exp(c * jnp.tanh(0.5 * pre) + c)
    gx = _sqrt_nonneg(1.0 - a * a) * (i * xg)
    return a, gx


def _row_scan(x, op, fill):
    row = lax.broadcasted_iota(jnp.int32, x.shape, 0)
    sh = 1
    while sh < x.shape[0]:
        x = op(x, jnp.where(row >= sh, pltpu.roll(x, sh, 0), fill))
        sh *= 2
    return x


def _slabs(ptm_ref, first):
    return jnp.concatenate([ptm_ref[first + g] for g in range(N_GROUPS)], axis=1)


def _pmix_body(x_ref, meta_ref, g1_ref, w_ref, wgi_ref, wgf_ref,
               rcw_ref, rcb_ref, wa_ref, wx_ref, rba_ref, rbx_ref, lam_ref,
               mcw_ref, mcb_ref, wq_ref, wkt_ref, wv_ref, bi_ref, bf_ref, mng_ref,
               y_ref, rgh_ref, rgc_ref, mcv_ref, st_ref, m_ref,
               px_s, pg_s, pm_s, po_s, a_s, g_s, bmc_s, bmx_s, yb_s, q_s, kt_s, v_s, ho_s, col_s, row_s, *, pad, lead):
    nb = y_ref.shape[0]
    lt = CHUNK
    rows = nb * lt
    ntail = (CONV_W - 1) * nb
    c = pl.program_id(0)

    @pl.when(c == 0)
    def _init():
        rgh_ref[...] = jnp.zeros_like(rgh_ref)
        rgc_ref[...] = jnp.zeros_like(rgc_ref)
        mcv_ref[...] = jnp.zeros_like(mcv_ref)
        st_ref[...] = jnp.zeros_like(st_ref)
        m_ref[...] = jnp.zeros_like(m_ref)
        ones_col = (lax.broadcasted_iota(jnp.int32, (rows, ML_HD), 1) == 0).astype(BF16)
        for h in range(ML_HEADS):
            v_s[h, :, ML_HD:S_COLS] = ones_col

    def conv(x, w_ref, b_ref, tail_ref):
        tail = tail_ref[...]
        acc = b_ref[...] + w_ref[CONV_W - 1:CONV_W, :] * x
        for k in range(1, CONV_W):
            shifted = jnp.concatenate([tail[ntail - k * nb:], x[:rows - k * nb]], axis=0)
            acc = acc + w_ref[CONV_W - 1 - k:CONV_W - k, :] * shifted
        tail_ref[...] = x[rows - ntail:]
        return acc

    x = x_ref[...]
    if lead:
        x = jnp.where(c == 0, _lead_block(meta_ref, nb, lt), x)
    u = _rms(x.reshape(rows, x.shape[2]), g1_ref[...]).astype(BF16)
    _project(u, w_ref, 2, pm_s, 0, nb, lt)
    gi = _gate_preacts(u, wgi_ref, nb, lt)
    gf = _gate_preacts(u, wgf_ref, nb, lt)

    mx = _slabs(pm_s, 0)
    mconv = conv(mx, mcw_ref, mcb_ref, mcv_ref)
    mc = mconv * _sigmoid(mconv)
    _project(u, w_ref, 0, px_s, 0, nb, lt)
    for t in range(lt):
        rs = slice(t * nb, (t + 1) * nb)
        idx = pl.ds(t, nb, stride=PITCH)
        for g in range(N_GROUPS):
            cols = slice(g * LANES, (g + 1) * LANES)
            bmc_s[g, idx, :] = mc[rs, cols]
            bmx_s[g, idx, :] = mx[rs, cols]

    def seq_major(src):
        return jnp.concatenate(
            [jnp.concatenate([src[g, b * PITCH:b * PITCH + lt, :] for b in range(nb)], axis=0)
             for g in range(N_GROUPS)], axis=1).astype(BF16)

    mcb = seq_major(bmc_s)
    mxb = seq_major(bmx_s)
    for h in range(ML_HEADS):
        sl = slice(h * ML_HD, (h + 1) * ML_HD)
        q_s[h] = jnp.dot(mcb[:, sl], wq_ref[h], preferred_element_type=F32) * (ML_HD ** -0.5)
        kt = lax.dot_general(wkt_ref[h], mcb[:, sl], _NT, preferred_element_type=F32)
        for b in range(nb):
            kt_s[h, b] = kt[:, b * lt:(b + 1) * lt]
        v_s[h, :, 0:ML_HD] = jnp.dot(mxb[:, sl], wv_ref[h], preferred_element_type=F32).astype(BF16)

    def to_cols(r):
        return jnp.concatenate([r, jnp.zeros((LANES - r.shape[0], lt), F32)], axis=0).T

    ig = to_cols(gi) + bi_ref[...]
    lf = jax.nn.log_sigmoid(to_cols(gf) + bf_ref[...])
    if pad:
        trow = lax.broadcasted_iota(jnp.int32, ig.shape, 0)
        is_pad = trow < jnp.where(c == 0, pad, 0)
        ig = jnp.where(is_pad, -1e30, ig)
        lf = jnp.where(is_pad, 0.0, lf)
    bcs = _row_scan(lf, jnp.add, 0.0)
    gg = ig - bcs
    cm = _row_scan(gg, jnp.maximum, -jnp.inf)
    m0 = m_ref[0:1, :]
    mm = jnp.maximum(m0, cm)
    mt = bcs + mm
    b_last = bcs[lt - 1:lt, :]
    m_last = mt[lt - 1:lt, :]
    col_s[0] = mm
    col_s[1] = jnp.exp(m0 - mm)
    col_s[2] = jnp.exp(-mt)
    row_s[0] = gg.T
    row_s[1] = jnp.exp(b_last - m_last + gg).T
    row_s[2] = jnp.broadcast_to(jnp.exp(b_last + m0 - m_last), (lt, LANES)).T
    m_ref[...] = jnp.broadcast_to(m_last, m_ref.shape)

    rg = {}

    def rg_conv():
        rg["xc"] = conv(_slabs(px_s, 0), rcw_ref, rcb_ref, rgc_ref)
        if pad:
            rowi = lax.broadcasted_iota(jnp.int32, (rows, 1), 0)
            rg["keep"] = rowi >= jnp.where(c == 0, pad * nb, 0)

    def rg_gates(g):
        a, gx = _rg_gates(rg["xc"], wa_ref, wx_ref, rba_ref, rbx_ref, lam_ref, g)
        if pad:
            gx = jnp.where(rg["keep"], gx, 0.0)
        a_s[g] = a
        g_s[g] = gx

    def rg_scan(g):
        sl = slice(g * LANES, (g + 1) * LANES)
        gate = _gelu(pg_s[g])
        h = rgh_ref[:, sl]
        for t in range(lt):
            rs = slice(t * nb, (t + 1) * nb)
            h = a_s[g, rs, :] * h + g_s[g, rs, :]
            yb_s[g, pl.ds(t, nb, stride=PITCH), :] = h * gate[rs]
        rgh_ref[:, sl] = h

    def rg_out(g):
        for b in range(nb):
            y_ref[b, :, g * LANES:(g + 1) * LANES] = yb_s[g, b * PITCH:b * PITCH + lt, :].astype(BF16)

    def gate_out(b):
        trows = slice(b * lt, (b + 1) * lt)
        for h in range(ML_HEADS):
            sl = slice(h * ML_HD, (h + 1) * ML_HD)
            hm = _sigmoid(po_s[trows, sl]) * ho_s[b * ML_HEADS + h]
            hm = hm * lax.rsqrt(jnp.mean(hm * hm, axis=-1, keepdims=True) + EPS)
            y_ref[b, :, D_RG + h * ML_HD:D_RG + (h + 1) * ML_HD] = (hm * mng_ref[:, sl]).astype(BF16)

    pieces = [[] for _ in range(nb + 1)]
    pieces[0].append(rg_conv)
    for g in range(N_GROUPS):
        pieces[g].append(functools.partial(rg_gates, g))
        pieces[g + 1].append(functools.partial(rg_scan, g))
        pieces[g + 2].append(functools.partial(rg_out, g))
    pieces[0].append(functools.partial(_project, u, w_ref, 1, pg_s, 0, nb, lt))
    pieces[1].append(functools.partial(_project, u, w_ref, 3, po_s, 0, nb, lt))
    for b in range(nb):
        pieces[max(b + 1, 2)].append(functools.partial(gate_out, b))

    t_i = lax.broadcasted_iota(jnp.int32, (lt, lt), 0)
    s_i = lax.broadcasted_iota(jnp.int32, (lt, lt), 1)
    causal = s_i <= t_i

    for b in range(nb):
        for piece in pieces[b]:
            piece()
        trows = slice(b * lt, (b + 1) * lt)
        heads = range(ML_HEADS)
        qb = [q_s[h, trows, :].astype(BF16) for h in heads]
        kt = [kt_s[h, b] for h in heads]
        s0 = [st_ref[b * ML_HEADS + h] for h in heads]
        sc = [jnp.dot(qb[h], kt[h].astype(BF16), preferred_element_type=F32) for h in heads]
        qs = [jnp.dot(qb[h], s0[h].astype(BF16), preferred_element_type=F32) for h in heads]
        pm = []
        for h in heads:
            r = b * GATE_ROWS + h
            d = jnp.exp(jnp.where(causal, row_s[0, r:r + 1, :] - col_s[0, :, r:r + 1], -jnp.inf))
            pm.append((sc[h] * d).astype(BF16))
        pv = [jnp.dot(pm[h], v_s[h, trows, :], preferred_element_type=F32) for h in heads]
        for h in heads:
            r = b * GATE_ROWS + h
            numx = pv[h] + col_s[1, :, r:r + 1] * qs[h]
            den = numx[:, ML_HD:ML_HD + 1]
            ho_s[b * ML_HEADS + h] = numx[:, 0:ML_HD] / jnp.maximum(jnp.abs(den), col_s[2, :, r:r + 1])
        for h in heads:
            r = b * GATE_ROWS + h
            wkt = (kt[h] * row_s[1, r:r + 1, :]).astype(BF16)
            sc_row = jnp.concatenate([row_s[2, r:r + 1, :], row_s[2, r:r + 1, :]], axis=1)
            st_ref[b * ML_HEADS + h] = sc_row * s0[h] + jnp.dot(wkt, v_s[h, trows, :], preferred_element_type=F32)
    for piece in pieces[nb]:
        piece()

    if pad:
        @pl.when(c == 0)
        def _zero_pad_rows():
            y_ref[:, 0:pad, :] = jnp.zeros((nb, pad, y_ref.shape[2]), BF16)


def _pmix(x, meta, lw, l, *, pad, lead, name):
    nb, t, d = x.shape
    nc = t // CHUNK + (1 if lead else 0)
    rows = nb * CHUNK
    ntail = (CONV_W - 1) * nb
    in_arrays = [x, meta, lw["ln1"], lw["w_in"], lw["wgi"], lw["wgf"],
                 lw["rcw"], lw["rcb"], lw["wa"], lw["wx"], lw["rba"], lw["rbx"], lw["lam"],
                 lw["mcw"], lw["mcb"], lw["wq"], lw["wkt"], lw["wv"], lw["bi"], lw["bf"], lw["mng"]]
    x_map = (lambda c: (0, jnp.maximum(c - 1, 0), 0)) if lead else (lambda c: (0, c, 0))
    in_specs = [pl.BlockSpec((nb, CHUNK, d), x_map), _const_spec(meta.shape)] + [
        _layer_spec(a, l) for a in in_arrays[2:]]
    in_specs[3] = pl.BlockSpec((None, d, D_MAIN), lambda c: (l, 0, 0), pipeline_mode=pl.Buffered(1))
    out_shape = [
        jax.ShapeDtypeStruct((nb, nc * CHUNK, D_MODEL), BF16),
        jax.ShapeDtypeStruct((nb, D_RG), F32),
        jax.ShapeDtypeStruct((ntail, D_RG), F32),
        jax.ShapeDtypeStruct((ntail, D_ML), F32),
        jax.ShapeDtypeStruct((nb * ML_HEADS, ML_HD, S_COLS), F32),
        jax.ShapeDtypeStruct((SUBLANES, LANES), F32),
    ]
    out_specs = [pl.BlockSpec((nb, CHUNK, D_MODEL), lambda c: (0, c, 0))] + [
        pl.BlockSpec(s.shape, lambda c, n=len(s.shape): (0,) * n) for s in out_shape[1:]]
    scratch = [
        pltpu.VMEM((N_GROUPS, rows, LANES), F32),
        pltpu.VMEM((N_GROUPS, rows, LANES), F32),
        pltpu.VMEM((N_GROUPS, rows, LANES), F32),
        pltpu.VMEM((rows, D_ML), F32),
        pltpu.VMEM((N_GROUPS, rows, LANES), F32),
        pltpu.VMEM((N_GROUPS, rows, LANES), F32),
        pltpu.VMEM((N_GROUPS, nb * PITCH, LANES), F32),
        pltpu.VMEM((N_GROUPS, nb * PITCH, LANES), F32),
        pltpu.VMEM((N_GROUPS, nb * PITCH, LANES), F32),
        pltpu.VMEM((ML_HEADS, rows, ML_HD), F32),
        pltpu.VMEM((ML_HEADS, nb, ML_HD, CHUNK), F32),
        pltpu.VMEM((ML_HEADS, rows, S_COLS), BF16),
        pltpu.VMEM((nb * ML_HEADS, CHUNK, ML_HD), F32),
        pltpu.VMEM((3, CHUNK, LANES), F32),
        pltpu.VMEM((3, LANES, CHUNK), F32),
    ]
    return pl.pallas_call(
        functools.partial(_pmix_body, pad=pad, lead=lead),
        grid=(nc,),
        in_specs=in_specs,
        out_specs=out_specs,
        out_shape=out_shape,
        scratch_shapes=scratch,
        compiler_params=pltpu.CompilerParams(dimension_semantics=("arbitrary",), vmem_limit_bytes=VMEM_LIMIT),
        name=name,
    )(*in_arrays)


SBLK = 4 * SUBLANES


def _smix_body(x_ref, g1_ref, w_ref, wgi_ref, wgf_ref,
               rgh_ref, rgc_ref, mcv_ref, c_ref, n_ref, mcol_ref, mrow_ref,
               rcw_ref, rcb_ref, wa_ref, wx_ref, rba_ref, rbx_ref, lam_ref,
               mcw_ref, mcb_ref, wq_ref, wk_ref, wv_ref, bi_ref, bf_ref, bic_ref, bfc_ref, mng_ref,
               y_ref, rgh_o, rgc_o, mcv_o, n_o, m_o, inter_o, dk_o, v_o,
               ptm_ref, pbm_ref, q_s, qc_s, col_s):
    i = pl.program_id(0)
    ns = pbm_ref.shape[0]

    def conv(first, w_ref, b_ref, tail_ref, tail_o):
        x = _slabs(ptm_ref, first)
        acc = b_ref[...] + w_ref[CONV_W - 1:CONV_W, :] * x
        for j in range(CONV_W - 1):
            acc = acc + w_ref[j:j + 1, :] * tail_ref[j]
        for j in range(CONV_W - 2):
            tail_o[j] = tail_ref[j + 1]
        tail_o[CONV_W - 2] = x
        return acc, x

    @pl.when(i == 0)
    def _rowwise():
        u = _rms(x_ref[0], g1_ref[...]).astype(BF16)
        for n in range(D_MAIN // 512 - 1):
            _project(u, w_ref, n, ptm_ref, n * N_GROUPS, 1, ns)
        _project(u, w_ref, D_MAIN // 512 - 1, pbm_ref, 0, 1, ns)
        gi = _gate_preacts(u, wgi_ref, 1, ns)
        gf = _gate_preacts(u, wgf_ref, 1, ns)
        xc, _ = conv(0, rcw_ref, rcb_ref, rgc_ref, rgc_o)
        for g in range(N_GROUPS):
            sl = slice(g * LANES, (g + 1) * LANES)
            a, gx = _rg_gates(xc, wa_ref, wx_ref, rba_ref, rbx_ref, lam_ref, g)
            hn = a * rgh_ref[:, sl] + gx
            rgh_o[:, sl] = hn
            y_ref[0, :, sl] = (hn * _gelu(ptm_ref[N_GROUPS + g])).astype(BF16)

        mconv, mx = conv(2 * N_GROUPS, mcw_ref, mcb_ref, mcv_ref, mcv_o)
        mcb = (mconv * _sigmoid(mconv)).astype(BF16)
        mxb = mx.astype(BF16)

        ig_r = gi + bi_ref[...]
        lf_r = jax.nn.log_sigmoid(gf + bf_ref[...])
        m0_r = mrow_ref[...]
        m_o[...] = jnp.maximum(lf_r + m0_r, ig_r)
        z = jnp.concatenate([gi, gf, jnp.zeros((ns - 2 * GATE_ROWS, ns), F32)], axis=0).T
        ig_c = z[:, 0:GATE_ROWS] + bic_ref[...]
        lf_c = jax.nn.log_sigmoid(z[:, GATE_ROWS:2 * GATE_ROWS] + bfc_ref[...])
        m0_c = mcol_ref[...]
        m_c = jnp.maximum(lf_c + m0_c, ig_c)
        inter_c = jnp.exp(lf_c + m0_c - m_c)
        dd_c = jnp.exp(ig_c - m_c)
        inter_o[...] = inter_c
        col_s[0] = inter_c
        col_s[1] = jnp.exp(-m_c)
        for h in range(ML_HEADS):
            sl = slice(h * ML_HD, (h + 1) * ML_HD)
            q = jnp.dot(mcb[:, sl], wq_ref[h], preferred_element_type=F32) * (ML_HD ** -0.5)
            k = jnp.dot(mcb[:, sl], wk_ref[h], preferred_element_type=F32)
            v = jnp.dot(mxb[:, sl], wv_ref[h], preferred_element_type=F32)
            q_s[h] = q
            dk_o[h] = dd_c[:, h:h + 1] * k
            v_o[h] = v
            col_s[2, :, h:h + 1] = jnp.sum(q * k, axis=-1, keepdims=True) * dd_c[:, h:h + 1]
            n0 = n_ref[h]
            col_s[3, :, h:h + 1] = jnp.sum(q * n0, axis=-1, keepdims=True)
            n_o[h] = inter_c[:, h:h + 1] * n0 + dd_c[:, h:h + 1] * k

    blk = pl.ds(pl.multiple_of(i * SBLK, SBLK), SBLK)
    for h in range(ML_HEADS):
        z = jnp.concatenate([q_s[h, blk, :], jnp.zeros((ML_HD - SBLK, ML_HD), F32)], axis=0).T
        qc_s[h, blk, :] = jnp.concatenate(
            [jnp.sum(z[:, j:j + 1] * c_ref[0, j, h], axis=0, keepdims=True) for j in range(SBLK)], axis=0)

    @pl.when(i == pl.num_programs(0) - 1)
    def _finish():
        for h in range(ML_HEADS):
            sl = slice(h * ML_HD, (h + 1) * ML_HD)
            inter = col_s[0, :, h:h + 1]
            s = col_s[2, :, h:h + 1]
            num = s * v_o[h] + inter * qc_s[h]
            den = s + inter * col_s[3, :, h:h + 1]
            hout = num / jnp.maximum(jnp.abs(den), col_s[1, :, h:h + 1])
            hm = _sigmoid(pbm_ref[:, sl]) * hout
            hm = hm * lax.rsqrt(jnp.mean(hm * hm, axis=-1, keepdims=True) + EPS)
            y_ref[0, :, D_RG + h * ML_HD:D_RG + (h + 1) * ML_HD] = (hm * mng_ref[:, sl]).astype(BF16)


def _smix(x, st, lw, l, *, name):
    ns, d = x.shape[1:]
    rgh, rgc, mcv, mc_all, mn, mcol, mrow = st
    in_arrays = [x, lw["ln1"], lw["w_in"], lw["wgi"], lw["wgf"], rgh, rgc, mcv, mc_all, mn, mcol, mrow,
                 lw["rcw"], lw["rcb"], lw["wa"], lw["wx"], lw["rba"], lw["rbx"], lw["lam"],
                 lw["mcw"], lw["mcb"], lw["wq"], lw["wk"], lw["wv"], lw["bi_s"], lw["bf_s"],
                 lw["bi_c"], lw["bf_c"], lw["mng"]]
    in_specs = [_const_spec(x.shape)] + [_layer_spec(a, l) for a in in_arrays[1:]]
    in_specs[2] = pl.BlockSpec((None, d, D_MAIN), lambda i: (l, 0, 0), pipeline_mode=pl.Buffered(1))
    in_specs[8] = pl.BlockSpec((1, SBLK, ML_HEADS, ML_HD, ML_HD), lambda i: (l, i, 0, 0, 0))
    out_shape = [
        jax.ShapeDtypeStruct((1, ns, D_MODEL), BF16),
        jax.ShapeDtypeStruct((ns, D_RG), F32),
        jax.ShapeDtypeStruct((CONV_W - 1, ns, D_RG), F32),
        jax.ShapeDtypeStruct((CONV_W - 1, ns, D_ML), F32),
        jax.ShapeDtypeStruct((ML_HEADS, ns, ML_HD), F32),
        jax.ShapeDtypeStruct((GATE_ROWS, ns), F32),
        jax.ShapeDtypeStruct((ns, GATE_ROWS), F32),
        jax.ShapeDtypeStruct((ML_HEADS, ns, ML_HD), F32),
        jax.ShapeDtypeStruct((ML_HEADS, ns, ML_HD), F32),
    ]
    out_specs = [pl.BlockSpec(s.shape, lambda i, n=len(s.shape): (0,) * n) for s in out_shape]
    scratch = [
        pltpu.VMEM((N_SLABS, ns, LANES), F32),
        pltpu.VMEM((ns, D_ML), F32),
        pltpu.VMEM((ML_HEADS, ns, ML_HD), F32),
        pltpu.VMEM((ML_HEADS, ns, ML_HD), F32),
        pltpu.VMEM((4, ns, GATE_ROWS), F32),
    ]
    return pl.pallas_call(
        _smix_body,
        grid=(ns // SBLK,),
        in_specs=in_specs,
        out_specs=out_specs,
        out_shape=out_shape,
        scratch_shapes=scratch,
        compiler_params=pltpu.CompilerParams(dimension_semantics=("arbitrary",), vmem_limit_bytes=VMEM_LIMIT_SAMPLE),
        name=name,
    )(*in_arrays)


CBLK = 4 * SUBLANES


def _cupdate_body(c_ref, inter_ref, dk_ref, v_ref, o_ref):
    for h in range(ML_HEADS):
        z = jnp.concatenate([dk_ref[0, h], jnp.zeros((ML_HD - CBLK, ML_HD), F32)], axis=0).T
        inter_rows = jnp.broadcast_to(inter_ref[0, :, h:h + 1], (CBLK, LANES))
        vblk = v_ref[0, h]
        for j in range(CBLK):
            o_ref[0, j, h] = inter_rows[j:j + 1, :] * c_ref[0, j, h] + z[:, j:j + 1] * vblk[j:j + 1, :]


def _cupdate(mc_all, inter_all, dk_all, v_all):
    depth, ns = mc_all.shape[:2]
    cspec = pl.BlockSpec((1, CBLK, ML_HEADS, ML_HD, ML_HD), lambda l, i: (l, i, 0, 0, 0))
    kvspec = pl.BlockSpec((1, ML_HEADS, CBLK, ML_HD), lambda l, i: (l, 0, i, 0))
    return pl.pallas_call(
        _cupdate_body,
        grid=(depth, ns // CBLK),
        in_specs=[cspec, pl.BlockSpec((1, CBLK, GATE_ROWS), lambda l, i: (l, i, 0)), kvspec, kvspec],
        out_specs=cspec,
        out_shape=jax.ShapeDtypeStruct(mc_all.shape, F32),
        compiler_params=pltpu.CompilerParams(dimension_semantics=("arbitrary", "arbitrary"),
                                             vmem_limit_bytes=VMEM_LIMIT),
        name="cupdate_s",
    )(mc_all, inter_all, dk_all, v_all)


def _block_diag_pairs(w):
    depth = w.shape[0]
    w = w.reshape(depth, N_GROUPS, 2, RG_BLOCK, RG_BLOCK)
    zero = jnp.zeros((depth, N_GROUPS, RG_BLOCK, RG_BLOCK), w.dtype)
    top = jnp.concatenate([w[:, :, 0], zero], axis=-1)
    bottom = jnp.concatenate([zero, w[:, :, 1]], axis=-1)
    return jnp.concatenate([top, bottom], axis=-2)


def _gate_rows(w_cols):
    depth = w_cols.shape[0]
    return jnp.concatenate([jnp.swapaxes(w_cols, 1, 2),
                            jnp.zeros((depth, GATE_ROWS - ML_HEADS, D_MODEL), w_cols.dtype)], axis=1)


def _stacked_weights(ln1_g, w_in, rg_conv_w, rg_conv_b, rg_w_a, rg_w_x, rg_b_a, rg_b_x, rg_lambda,
                     m_conv_w, m_conv_b, m_w_q, m_w_k, m_w_v, m_b_i, m_b_f, m_norm_g, w_out, ln2_g,
                     w_ff1, w_ff2, nb_prompt, ns):
    depth = ln1_g.shape[0]
    row = lambda a: a.reshape(depth, 1, -1)
    pad8 = lambda a: jnp.concatenate([a, jnp.zeros((depth, GATE_ROWS - ML_HEADS), F32)], axis=1)
    bias8, bfor8 = pad8(m_b_i), pad8(m_b_f)
    lanes = lambda a: jnp.concatenate([jnp.tile(a, (1, nb_prompt)),
                                       jnp.zeros((depth, LANES - nb_prompt * GATE_ROWS), F32)], axis=1)[:, None, :]
    return dict(
        ln1=row(ln1_g),
        w_in=w_in.astype(BF16),
        wgi=_gate_rows(w_in[:, :, D_MAIN:D_MAIN + ML_HEADS]).astype(BF16),
        wgf=_gate_rows(w_in[:, :, D_MAIN + ML_HEADS:]).astype(BF16),
        rcw=rg_conv_w, rcb=row(rg_conv_b),
        wa=_block_diag_pairs(rg_w_a).astype(BF16), wx=_block_diag_pairs(rg_w_x).astype(BF16),
        rba=row(rg_b_a), rbx=row(rg_b_x), lam=row(rg_lambda),
        mcw=m_conv_w, mcb=row(m_conv_b),
        wq=m_w_q.astype(BF16), wk=m_w_k.astype(BF16), wkt=jnp.swapaxes(m_w_k, 2, 3).astype(BF16),
        wv=m_w_v.astype(BF16),
        bi=lanes(bias8), bf=lanes(bfor8),
        bi_s=jnp.broadcast_to(bias8[:, :, None], (depth, GATE_ROWS, ns)),
        bf_s=jnp.broadcast_to(bfor8[:, :, None], (depth, GATE_ROWS, ns)),
        bi_c=bias8.reshape(depth, 1, GATE_ROWS), bf_c=bfor8.reshape(depth, 1, GATE_ROWS),
        mng=row(m_norm_g),
        w_out=w_out.astype(BF16), ln2=row(ln2_g),
        w_ff1=w_ff1.astype(BF16), w_ff2=w_ff2.astype(BF16),
    )


def kernel(x_prompt, x_sample, state_rg_h, state_rg_conv, state_m_conv, state_m_C, state_m_n, state_m_m,
           meta_tokens, ln1_g, w_in, rg_conv_w, rg_conv_b, rg_w_a, rg_w_x, rg_b_a, rg_b_x, rg_lambda,
           m_conv_w, m_conv_b, m_w_q, m_w_k, m_w_v, m_b_i, m_b_f, m_norm_g, w_out, ln2_g,
           w_ff1, w_ff2, ln_f_g):
    nb, seq, d = x_prompt.shape
    ns = x_sample.shape[0]
    depth = ln1_g.shape[0]
    t_real = N_META + seq
    nc = -(-t_real // CHUNK)
    pad = nc * CHUNK - t_real
    assert (pad + N_META) % CHUNK == 0 and x_sample.shape[1] == 1 and ns == LANES and depth == DEPTH
    assert CHUNK == LANES and nb * GATE_ROWS <= LANES

    lw = _stacked_weights(ln1_g, w_in, rg_conv_w, rg_conv_b, rg_w_a, rg_w_x, rg_b_a, rg_b_x, rg_lambda,
                          m_conv_w, m_conv_b, m_w_q, m_w_k, m_w_v, m_b_i, m_b_f, m_norm_g, w_out, ln2_g,
                          w_ff1, w_ff2, nb, ns)
    lnf = ln_f_g.reshape(1, d)
    meta = meta_tokens.astype(F32)
    xp = x_prompt
    xs = x_sample.reshape(1, ns, d)
    m_cols = jnp.concatenate([state_m_m, jnp.zeros((depth, ns, GATE_ROWS - ML_HEADS), F32)], axis=2)
    st_in = (state_rg_h, jnp.swapaxes(state_rg_conv, 1, 2), jnp.swapaxes(state_m_conv, 1, 2), state_m_C,
             jnp.swapaxes(state_m_n, 1, 2), m_cols, jnp.swapaxes(m_cols, 1, 2))

    p_states = [[] for _ in range(6)]
    s_states = [[] for _ in range(6)]
    for l in range(depth):
        last = l == depth - 1

        y, rgh, rgc, mcv, st, mrow = _pmix(xp, meta, lw, l, pad=pad, lead=l == 0, name=f"mixer_p{l}")
        xp = _outmlp(xp, y, lw, l, lnf, lt=CHUNK, final=last, skip=(pad + N_META) // CHUNK if last else 0,
                     nreal0=0 if last else CHUNK - pad, name=f"outmlp_p{l}", meta=meta if l == 0 else None)
        st = st.reshape(nb, ML_HEADS, ML_HD, S_COLS)
        p_states[0].append(rgh)
        p_states[1].append(jnp.swapaxes(rgc.reshape(CONV_W - 1, nb, D_RG), 0, 1))
        p_states[2].append(jnp.swapaxes(mcv.reshape(CONV_W - 1, nb, D_ML), 0, 1))
        p_states[3].append(st[..., :ML_HD])
        p_states[4].append(st[..., ML_HD])
        p_states[5].append(mrow[0, :nb * GATE_ROWS].reshape(nb, GATE_ROWS)[:, :ML_HEADS])

        ys, srgh, srgc, smcv, smn, smrow, sinter, sdk, sv = _smix(xs, st_in, lw, l, name=f"mixer_s{l}")
        xs = _outmlp_stream(xs, ys, lw, l, lnf, final=last, name=f"outmlp_s{l}")
        s_states[0].append(srgh)
        s_states[1].append(jnp.swapaxes(srgc, 0, 1))
        s_states[2].append(jnp.swapaxes(smcv, 0, 1))
        s_states[3].append((sinter, sdk, sv))
        s_states[4].append(jnp.swapaxes(smn, 0, 1))
        s_states[5].append(smrow[:ML_HEADS].T)

    y_prompt = xp
    y_sample = xs.reshape(ns, 1, d)
    ps_out = [jnp.stack(s) for s in p_states]
    c_new = _cupdate(state_m_C, *(jnp.stack([t[j] for t in s_states[3]]) for j in range(3)))
    ss_out = [c_new if j == 3 else jnp.stack(s) for j, s in enumerate(s_states)]
    return (y_prompt, y_sample, *ps_out, *ss_out)
```
